```python
import math
import jax, jax.numpy as jnp
from jax import lax
import numpy as np

D_MODEL = 1024
BATCH = 32
SEQ = 2048
DEPTH = 2
DEC_BATCH = 16
DEC_SEQ = 32
PAST_LEN = 4096

CHUNK = 64
HEAD_DIM = 64
H_A = 6
W_A = H_A * HEAD_DIM
DECAY_RANK = 64
ICLR_RANK = 64
GATE_RANK = 128
RWKV_PROJ = 3 * W_A + DECAY_RANK + ICLR_RANK + GATE_RANK
H_B = 6
DK_B = 64
DV_B = 64
W_B = H_B * DV_B
GDN_CONV = 4
GDN_CONV_CH = 2 * H_B * DK_B + H_B * DV_B
GDN_PROJ = GDN_CONV_CH + H_B + H_B + W_B
H_C = 4
W_C = H_C * HEAD_DIM
N_BAND_PREV = 8
MAX_REL = 128
ATT_PROJ = 3 * W_C
N_BRANCH = 3
GATE_PROJ = N_BRANCH * D_MODEL
N_IN = RWKV_PROJ + GDN_PROJ + ATT_PROJ + GATE_PROJ
D_FF = 4 * D_MODEL
EPS = 1e-6
GN_EPS = 64e-5

kernel_name = "hybrid_rwkv7_gdn_chunkband_streaming_step"


def rmsnorm(x, g):
    xf = x.astype(jnp.float32)
    y = xf * lax.rsqrt(jnp.mean(xf * xf, axis=-1, keepdims=True) + EPS)
    return (y * g.astype(jnp.float32)).astype(x.dtype)


def split_cols(x, sizes):
    return jnp.split(x, [int(s) for s in np.cumsum(sizes)[:-1]], axis=-1)


def l2norm(x):
    return x * lax.rsqrt(jnp.sum(x * x, axis=-1, keepdims=True) + 1e-6)


def band_keep(n_frames):
    return min(N_BAND_PREV * CHUNK, n_frames)


def rwkv7_scan(r, w, k, v, kk, a, S0):
    def step(S, inp):
        r_t, w_t, k_t, v_t, kk_t, a_t = inp
        sa = jnp.einsum('bhvk,bhk->bhv', S, -kk_t)
        S = (S * w_t[:, :, None, :] + sa[..., None] * (kk_t * a_t)[:, :, None, :]
             + v_t[..., None] * k_t[:, :, None, :])
        y = jnp.einsum('bhvk,bhk->bhv', S, r_t)
        return S, y
    xs = tuple(jnp.moveaxis(t, 1, 0) for t in (r, w, k, v, kk, a))
    S, ys = lax.scan(step, S0, xs)
    return jnp.moveaxis(ys, 0, 1), S


def rwkv7_mix(c, shift_prev, S0, p):
    B, T, _ = c.shape
    f32 = jnp.float32
    c = c.astype(f32)
    c_prev = jnp.concatenate([shift_prev.astype(f32)[:, None], c[:, :-1]], axis=1)
    cm = c + (c_prev - c) * p["rwkv_mu"]
    r, k, v, wd, ad, gd = split_cols(cm, [W_A, W_A, W_A, DECAY_RANK, ICLR_RANK, GATE_RANK])
    w_log = -jax.nn.softplus(-(p["rwkv_w0"] + jnp.tanh(wd) @ p["rwkv_w2"])) - 0.5
    w = jnp.exp(-jnp.exp(w_log.astype(f32)))
    a = jax.nn.sigmoid(p["rwkv_a0"] + ad @ p["rwkv_a2"])
    g = jax.nn.sigmoid(gd) @ p["rwkv_g2"]
    kk = l2norm((k * p["rwkv_kk"]).reshape(B, T, H_A, HEAD_DIM))
    k = k * (1.0 + (a - 1.0) * p["rwkv_ka"])
    heads = lambda t: t.reshape(B, T, H_A, HEAD_DIM).astype(f32)
    r_h, w_h, k_h, v_h, a_h = heads(r), heads(w), heads(k), heads(v), heads(a)
    y, S = rwkv7_scan(r_h, w_h, k_h, v_h, kk.astype(f32), a_h, S0.astype(f32))
    mu = jnp.mean(y, axis=-1, keepdims=True)
    var = jnp.mean(jnp.square(y - mu), axis=-1, keepdims=True)
    y = ((y - mu) * lax.rsqrt(var + GN_EPS)).reshape(B, T, W_A) * p["rwkv_ln_w"] + p["rwkv_ln_b"]
    bonus = jnp.sum(r_h * k_h * p["rwkv_rk"], axis=-1, keepdims=True) * v_h
    y = (y + bonus.reshape(B, T, W_A)) * g
    return y, c[:, -1], S


def chunk_gated_delta(q, k, v, g, beta, S0):
    B, T, H, DK = q.shape
    DV = v.shape[-1]
    L = CHUNK
    n = -(-T // L)
    pad = n * L - T

    def blocks(t):
        t = jnp.pad(t, [(0, 0), (0, pad)] + [(0, 0)] * (t.ndim - 2))
        t = t.reshape((B, n, L) + t.shape[2:])
        return jnp.moveaxis(jnp.moveaxis(t, 1, 0), 2, 3)

    qb, kb, vb, gb, bb = blocks(q), blocks(k), blocks(v), blocks(g), blocks(beta)
    gc = jnp.cumsum(gb, axis=-1)
    idx = jnp.arange(L)
    causal = idx[:, None] >= idx[None, :]
    strict = idx[:, None] > idx[None, :]
    gamma = jnp.exp(jnp.where(causal, gc[..., :, None] - gc[..., None, :], -jnp.inf))
    kbeta = kb * bb[..., None]
    lower = jnp.where(strict, jnp.einsum('nbhid,nbhjd->nbhij', kbeta, kb) * gamma, 0.0)
    rhs = jnp.concatenate([vb * bb[..., None], kbeta * jnp.exp(gc)[..., None]], axis=-1)
    sol = lax.linalg.triangular_solve(lower + jnp.eye(L, dtype=lower.dtype), rhs,
                                      left_side=True, lower=True)
    u_pre, w = sol[..., :DV], sol[..., DV:]
    att = jnp.einsum('nbhid,nbhjd->nbhij', qb, kb) * gamma

    def step(S, inp):
        q_c, k_c, u_c, w_c, gc_c, att_c = inp
        u = u_c - jnp.einsum('bhld,bhdv->bhlv', w_c, S)
        o = (jnp.einsum('bhld,bhdv->bhlv', q_c * jnp.exp(gc_c)[..., None], S)
             + jnp.einsum('bhij,bhjv->bhiv', att_c, u))
        g_last = gc_c[..., -1:]
        S = S * jnp.exp(g_last)[..., None] + jnp.einsum(
            'bhld,bhlv->bhdv', k_c * jnp.exp(g_last - gc_c)[..., None], u)
        return S, o

    S, o = lax.scan(step, S0, (qb, kb, u_pre, w, gc, att))
    o = jnp.swapaxes(jnp.moveaxis(o, 0, 1), 2, 3).reshape(B, n * L, H, DV)[:, :T]
    return o, S


def gdn_mix(c, conv_prev, S0, p):
    B, T, _ = c.shape
    f32 = jnp.float32
    c = c.astype(f32)
    qkv, a_raw, b_raw, z = split_cols(c, [GDN_CONV_CH, H_B, H_B, W_B])
    xpad = jnp.concatenate([conv_prev.astype(f32), qkv], axis=1)
    conv_w = p["gdn_conv_w"].astype(f32)
    conv = xpad[:, 0:T] * conv_w[0]
    for j in range(1, GDN_CONV):
        conv = conv + xpad[:, j:j + T] * conv_w[j]
    qkv = jax.nn.silu(conv)
    q, k, v = split_cols(qkv, [H_B * DK_B, H_B * DK_B, W_B])
    q = l2norm(q.reshape(B, T, H_B, DK_B)) * (DK_B ** -0.5)
    k = l2norm(k.reshape(B, T, H_B, DK_B))
    v = v.reshape(B, T, H_B, DV_B)
    g = -jnp.exp(p["gdn_A_log"].astype(f32)) * jax.nn.softplus(a_raw + p["gdn_dt_bias"])
    beta = jax.nn.sigmoid(b_raw)
    o, S = chunk_gated_delta(q, k, v, g.astype(f32), beta.astype(f32), S0.astype(f32))
    o = rmsnorm(o, p["gdn_norm_w"]) * jax.nn.silu(z.reshape(B, T, H_B, DV_B))
    return o.reshape(B, T, W_B), xpad[:, -(GDN_CONV - 1):], S


def band_attend(q, k, v, q_pos, k_pos, k_valid, rel_bias):
    s = jnp.einsum('bqhd,bkhd->bhqk', q, k).astype(jnp.float32) * (HEAD_DIM ** -0.5)
    rel = jnp.clip(q_pos[:, None] - k_pos[None, :], -MAX_REL, MAX_REL) + MAX_REL
    s = s + rel_bias[:, rel].astype(jnp.float32)[None]
    s = jnp.where(k_valid[None, None, None, :], s, -jnp.inf)
    prob = jax.nn.softmax(s, axis=-1)
    return jnp.einsum('bhqk,bkhd->bqhd', prob.astype(v.dtype), v)


def band_attention_prompt(q, k, v, rel_bias):
    B, T, H, D = q.shape
    nc = T // CHUNK
    lead = N_BAND_PREV * CHUNK
    band = (N_BAND_PREV + 1) * CHUNK
    kp = jnp.pad(k, [(0, 0), (lead, 0), (0, 0), (0, 0)])
    vp = jnp.pad(v, [(0, 0), (lead, 0), (0, 0), (0, 0)])

    def one_chunk(ci):
        qc = lax.dynamic_slice_in_dim(q, ci * CHUNK, CHUNK, axis=1)
        kc = lax.dynamic_slice_in_dim(kp, ci * CHUNK, band, axis=1)
        vc = lax.dynamic_slice_in_dim(vp, ci * CHUNK, band, axis=1)
        q_pos = ci * CHUNK + jnp.arange(CHUNK)
        k_pos = (ci - N_BAND_PREV) * CHUNK + jnp.arange(band)
        return band_attend(qc, kc, vc, q_pos, k_pos, k_pos >= 0, rel_bias)

    out = lax.map(one_chunk, jnp.arange(nc))
    return jnp.moveaxis(out, 0, 1).reshape(B, T, H, D)


def band_attention_step(q, k, v, cache_k, cache_v, rel_bias):
    T = q.shape[1]
    P = cache_k.shape[1]
    k_all = jnp.concatenate([cache_k.astype(k.dtype), k], axis=1)
    v_all = jnp.concatenate([cache_v.astype(v.dtype), v], axis=1)
    q_pos = PAST_LEN + jnp.arange(T)
    k_pos = PAST_LEN - P + jnp.arange(P + T)
    valid = jnp.ones((P + T,), dtype=bool)
    return band_attend(q, k_all, v_all, q_pos, k_pos, valid, rel_bias)


def trunk_layer(x, p, rwkv_shift, rwkv_S, gdn_conv, gdn_S, band_k, band_v):
    B, T, _ = x.shape
    f32 = jnp.float32
    h = rmsnorm(x, p["g_pre_mix"])
    proj = h @ p["w_in"]
    c_a, c_b, c_c, c_g = split_cols(proj, [RWKV_PROJ, GDN_PROJ, ATT_PROJ, GATE_PROJ])
    y_a, shift_new, S_a = rwkv7_mix(c_a, rwkv_shift, rwkv_S, p)
    y_b, conv_new, S_b = gdn_mix(c_b, gdn_conv, gdn_S, p)
    q, k, v = [t.reshape(B, T, H_C, HEAD_DIM) for t in split_cols(c_c, [W_C, W_C, W_C])]
    if band_k is None:
        y_c = band_attention_prompt(q, k, v, p["att_rel_bias"])
        keep = band_keep(T)
        k_out, v_out = k[:, T - keep:], v[:, T - keep:]
    else:
        y_c = band_attention_step(q, k, v, band_k, band_v, p["att_rel_bias"])
        k_out, v_out = k, v
    y_c = y_c.reshape(B, T, W_C)
    gates = jax.nn.sigmoid(c_g.astype(f32)).reshape(B, T, N_BRANCH, D_MODEL)
    merged = (gates[:, :, 0] * (y_a @ p["w_br_a"]) + gates[:, :, 1] * (y_b @ p["w_br_b"])
              + gates[:, :, 2] * (y_c @ p["w_br_c"]).astype(f32)).astype(x.dtype)
    x = x + rmsnorm(merged @ p["w_out"], p["g_post_mix"])
    h = rmsnorm(x, p["g_pre_ffn"])
    f = jnp.square(jax.nn.relu(h @ p["w_ff_up"])) @ p["w_ff_down"]
    x = x + rmsnorm(f, p["g_post_ffn"])
    return x, (shift_new, S_a, conv_new, S_b, k_out, v_out)


def setup_inputs(seed: int = 0) -> dict:
    key = jax.random.key(seed)
    keys = iter(jax.random.split(key, 40))
    nrm = lambda shape, scale: jax.random.normal(next(keys), shape, jnp.float32) * scale
    uni = lambda shape, lo, hi: jax.random.uniform(next(keys), shape, jnp.float32, lo, hi)
    keep = band_keep(PAST_LEN)
    dt = jnp.exp(uni((DEPTH, H_B), math.log(1e-3), math.log(1e-1)))
    return {
        "x_prompt": nrm((BATCH, SEQ, D_MODEL), 1.0),
        "x_sample": nrm((DEC_BATCH, DEC_SEQ, D_MODEL), 1.0),
        "state_rwkv_shift": nrm((DEPTH, DEC_BATCH, RWKV_PROJ), 1.0),
        "state_rwkv_wkv": nrm((DEPTH, DEC_BATCH, H_A, HEAD_DIM, HEAD_DIM), 0.5),
        "state_gdn_conv": nrm((DEPTH, DEC_BATCH, GDN_CONV - 1, GDN_CONV_CH), 1.0),
        "state_gdn_S": nrm((DEPTH, DEC_BATCH, H_B, DK_B, DV_B), 0.5),
        "cache_band_k": nrm((DEPTH, DEC_BATCH, keep, H_C, HEAD_DIM), 1.0),
        "cache_band_v": nrm((DEPTH, DEC_BATCH, keep, H_C, HEAD_DIM), 1.0),
        "g_pre_mix": 1.0 + nrm((DEPTH, D_MODEL), 0.1),
        "g_post_mix": 1.0 + nrm((DEPTH, D_MODEL), 0.1),
        "g_pre_ffn": 1.0 + nrm((DEPTH, D_MODEL), 0.1),
        "g_post_ffn": 1.0 + nrm((DEPTH, D_MODEL), 0.1),
        "w_in": nrm((DEPTH, D_MODEL, N_IN), D_MODEL ** -0.5),
        "rwkv_mu": uni((DEPTH, RWKV_PROJ), 0.0, 1.0),
        "rwkv_w0": uni((DEPTH, W_A), -5.0, 1.0),
        "rwkv_w2": nrm((DEPTH, DECAY_RANK, W_A), 0.5 * DECAY_RANK ** -0.5),
        "rwkv_a0": nrm((DEPTH, W_A), 0.5),
        "rwkv_a2": nrm((DEPTH, ICLR_RANK, W_A), 0.5 * ICLR_RANK ** -0.5),
        "rwkv_g2": nrm((DEPTH, GATE_RANK, W_A), GATE_RANK ** -0.5),
        "rwkv_kk": 0.85 + nrm((DEPTH, W_A), 0.05),
        "rwkv_ka": 1.0 + nrm((DEPTH, W_A), 0.05),
        "rwkv_rk": nrm((DEPTH, H_A, HEAD_DIM), 0.1),
        "rwkv_ln_w": 1.0 + nrm((DEPTH, W_A), 0.1),
        "rwkv_ln_b": nrm((DEPTH, W_A), 0.01),
        "gdn_conv_w": nrm((DEPTH, GDN_CONV, GDN_CONV_CH), GDN_CONV ** -0.5),
        "gdn_A_log": jnp.log(uni((DEPTH, H_B), 1.0, 16.0)),
        "gdn_dt_bias": dt + jnp.log(-jnp.expm1(-dt)),
        "gdn_norm_w": 1.0 + nrm((DEPTH, DV_B), 0.1),
        "att_rel_bias": nrm((DEPTH, H_C, 2 * MAX_REL + 1), 0.1),
        "w_br_a": nrm((DEPTH, W_A, D_MODEL), W_A ** -0.5),
        "w_br_b": nrm((DEPTH, W_B, D_MODEL), W_B ** -0.5),
        "w_br_c": nrm((DEPTH, W_C, D_MODEL), W_C ** -0.5),
        "w_out": nrm((DEPTH, D_MODEL, D_MODEL), D_MODEL ** -0.5),
        "w_ff_up": nrm((DEPTH, D_MODEL, D_FF), D_MODEL ** -0.5),
        "w_ff_down": nrm((DEPTH, D_FF, D_MODEL), D_FF ** -0.5),
    }


def reference(x_prompt, x_sample, state_rwkv_shift, state_rwkv_wkv, state_gdn_conv,
              state_gdn_S, cache_band_k, cache_band_v, g_pre_mix, g_post_mix, g_pre_ffn,
              g_post_ffn, w_in, rwkv_mu, rwkv_w0, rwkv_w2, rwkv_a0, rwkv_a2, rwkv_g2,
              rwkv_kk, rwkv_ka, rwkv_rk, rwkv_ln_w, rwkv_ln_b, gdn_conv_w, gdn_A_log,
              gdn_dt_bias, gdn_norm_w, att_rel_bias, w_br_a, w_br_b, w_br_c, w_out,
              w_ff_up, w_ff_down):
    f32 = jnp.float32
    Bp = x_prompt.shape[0]
    xp, xs = x_prompt, x_sample
    p_new = [[] for _ in range(6)]
    s_new = [[] for _ in range(6)]
    for l in range(DEPTH):
        p = {
            "g_pre_mix": g_pre_mix[l], "g_post_mix": g_post_mix[l],
            "g_pre_ffn": g_pre_ffn[l], "g_post_ffn": g_post_ffn[l],
            "w_in": w_in[l], "rwkv_mu": rwkv_mu[l], "rwkv_w0": rwkv_w0[l],
            "rwkv_w2": rwkv_w2[l], "rwkv_a0": rwkv_a0[l], "rwkv_a2": rwkv_a2[l],
            "rwkv_g2": rwkv_g2[l], "rwkv_kk": rwkv_kk[l], "rwkv_ka": rwkv_ka[l],
            "rwkv_rk": rwkv_rk[l], "rwkv_ln_w": rwkv_ln_w[l], "rwkv_ln_b": rwkv_ln_b[l],
            "gdn_conv_w": gdn_conv_w[l], "gdn_A_log": gdn_A_log[l],
            "gdn_dt_bias": gdn_dt_bias[l], "gdn_norm_w": gdn_norm_w[l],
            "att_rel_bias": att_rel_bias[l], "w_br_a": w_br_a[l], "w_br_b": w_br_b[l],
            "w_br_c": w_br_c[l], "w_out": w_out[l], "w_ff_up": w_ff_up[l],
            "w_ff_down": w_ff_down[l],
        }
        xp, st_p = trunk_layer(
            xp, p,
            jnp.zeros((Bp, RWKV_PROJ), f32),
            jnp.zeros((Bp, H_A, HEAD_DIM, HEAD_DIM), f32),
            jnp.zeros((Bp, GDN_CONV - 1, GDN_CONV_CH), f32),
            jnp.zeros((Bp, H_B, DK_B, DV_B), f32),
            None, None)
        xs, st_s = trunk_layer(
            xs, p, state_rwkv_shift[l], state_rwkv_wkv[l], state_gdn_conv[l],
            state_gdn_S[l], cache_band_k[l], cache_band_v[l])
        for i in range(6):
            p_new[i].append(st_p[i])
            s_new[i].append(st_s[i])
    stk_p = [jnp.stack(a).astype(x_prompt.dtype) for a in p_new]
    stk_s = [jnp.stack(a).astype(x_sample.dtype) for a in s_new]
    return (xp, xs,
            stk_p[0], stk_p[1], stk_p[2], stk_p[3], stk_p[4], stk_p[5],
            stk_s[0], stk_s[1], stk_s[2], stk_s[3], stk_s[4], stk_s[5])
```

```python
import functools
import math

import numpy as np
import jax
import jax.numpy as jnp
from jax import lax
from jax.experimental import pallas as pl
from jax.experimental.pallas import tpu as pltpu

F32 = jnp.float32
BF16 = jnp.bfloat16

D_MODEL = 1024
HEAD = 64
H_A = 6
W_A = H_A * HEAD
DECAY_RANK = 64
ICLR_RANK = 64
GATE_RANK = 128
RWKV_PROJ = 3 * W_A + DECAY_RANK + ICLR_RANK + GATE_RANK
H_B = 6
W_B = H_B * HEAD
GDN_CONV = 4
GDN_CH = 3 * W_B
H_C = 4
W_C = H_C * HEAD
N_BAND_PREV = 8
MAX_REL = 128
CHUNK = 64
D_FF = 4 * D_MODEL
EPS = 1e-6
GN_EPS = 64e-5

LANE = 128
AB_PAD = LANE
OFF_GATE = 0
OFF_ATT = OFF_GATE + 3 * D_MODEL
OFF_Z = OFF_ATT + 3 * W_C
OFF_RWKV = OFF_Z + W_B
OFF_AB = OFF_RWKV + RWKV_PROJ
OFF_QKV = OFF_AB + AB_PAD
N_PROJ = OFF_QKV + GDN_CH
assert OFF_ATT % W_C == 0 and OFF_Z % W_B == 0 and OFF_RWKV % RWKV_PROJ == 0
assert OFF_AB % AB_PAD == 0 and OFF_QKV % GDN_CH == 0 and N_PROJ % LANE == 0

VMEM_LIMIT = 56 * 1024 * 1024
TM_IN = 256
TN_IN = 768
TM_MERGE = 512
TM_FFN = 512
FF_CHUNK = 1024


def _params(*sem):
    return pltpu.CompilerParams(dimension_semantics=sem, vmem_limit_bytes=VMEM_LIMIT)


def _const_spec(shape):
    nd = len(shape)
    return pl.BlockSpec(shape, lambda *_: (0,) * nd)


def _mm(a, b):
    return jnp.dot(a.astype(BF16), b.astype(BF16), preferred_element_type=F32)


def _mm_nt(a, b):
    return lax.dot_general(a.astype(BF16), b.astype(BF16), (((1,), (1,)), ((), ())),
                           preferred_element_type=F32)


def _mm_tn(a, b):
    return lax.dot_general(a.astype(BF16), b.astype(BF16), (((0,), (0,)), ((), ())),
                           preferred_element_type=F32)


def _split(a):
    hi = a.astype(BF16)
    lo = (a - hi.astype(F32)).astype(BF16)
    return hi, lo


def _mm_lsplit(a, b_exact):
    hi, lo = _split(a)
    return (jnp.dot(hi, b_exact, preferred_element_type=F32)
            + jnp.dot(lo, b_exact, preferred_element_type=F32))


def _mm_rsplit(a_exact, b):
    hi, lo = _split(b)
    return (jnp.dot(a_exact, hi, preferred_element_type=F32)
            + jnp.dot(a_exact, lo, preferred_element_type=F32))


def _sigmoid(x):
    return 1.0 / (1.0 + jnp.exp(-x))


def _softplus(x):
    return jnp.maximum(x, 0.0) + jnp.log(1.0 + jnp.exp(-jnp.abs(x)))


def _tri_inv_minus_eye(a_strict):
    n = a_strict.shape[0]
    neg = -a_strict
    tp = neg
    pw = neg
    for _ in range(int(math.log2(n)) - 1):
        pw = _mm(pw, pw)
        tp = tp + pw + _mm(tp, pw)
    return tp


def _tri_masks(n):
    r = lax.broadcasted_iota(jnp.int32, (n, n), 0)
    c = lax.broadcasted_iota(jnp.int32, (n, n), 1)
    return r > c, r >= c


def _in_proj_kernel(x_ref, g_ref, w_ref, o_ref):
    x = x_ref[...]
    h = (x * lax.rsqrt(jnp.mean(x * x, axis=-1, keepdims=True) + EPS) * g_ref[...]).astype(BF16)
    for n in range(N_PROJ // TN_IN):
        cols = slice(n * TN_IN, (n + 1) * TN_IN)
        o_ref[:, cols] = jnp.dot(h, w_ref[:, cols], preferred_element_type=F32)


def _in_proj(x2d, g, w_perm):
    m = x2d.shape[0]
    tm = min(TM_IN, m)
    return pl.pallas_call(
        _in_proj_kernel,
        grid=(m // tm,),
        in_specs=[pl.BlockSpec((tm, D_MODEL), lambda i: (i, 0)),
                  _const_spec((1, D_MODEL)),
                  _const_spec((D_MODEL, N_PROJ))],
        out_specs=pl.BlockSpec((tm, N_PROJ), lambda i: (i, 0)),
        out_shape=jax.ShapeDtypeStruct((m, N_PROJ), F32),
        compiler_params=_params("parallel"),
    )(x2d, g, w_perm)


def _rwkv_kernel(c_ref, shift0_ref, s0_ref, mu_ref, w0_ref, w2_ref, a0_ref, a2_ref, g2_ref,
                 kk_ref, ka_ref, rk_ref, lnw_ref, lnb_ref, bones_ref, ltri_ref,
                 y_ref, sout_ref, s_scr, prev_scr, y_scr):
    L = c_ref.shape[1]
    t = pl.program_id(1)

    @pl.when(t == 0)
    def _():
        s_scr[...] = s0_ref[0]
        prev_scr[0:1, :] = shift0_ref[0]

    c = c_ref[0]
    row = lax.broadcasted_iota(jnp.int32, c.shape, 0)
    c_prev = jnp.where(row == 0, prev_scr[0:1, :], pltpu.roll(c, 1, 0))
    prev_scr[0:1, :] = c[L - 1:L, :]
    cm = c + (c_prev - c) * mu_ref[...]
    r = cm[:, 0:W_A]
    k = cm[:, W_A:2 * W_A]
    v = cm[:, 2 * W_A:3 * W_A]
    o = 3 * W_A
    wd = cm[:, o:o + DECAY_RANK]
    ad = cm[:, o + DECAY_RANK:o + DECAY_RANK + ICLR_RANK]
    gd = cm[:, o + DECAY_RANK + ICLR_RANK:]

    bones = bones_ref[...]
    seg_sum = lambda x: _mm_lsplit(x, bones)

    w_log = -_softplus(-(w0_ref[...] + _mm(jnp.tanh(wd), w2_ref[...]))) - 0.5
    logw = -jnp.exp(w_log)
    a = _sigmoid(a0_ref[...] + _mm(ad, a2_ref[...]))
    g = _mm(_sigmoid(gd), g2_ref[...])
    kks = k * kk_ref[...]
    kkn = kks * lax.rsqrt(seg_sum(kks * kks) + 1e-6)
    k2 = k * (1.0 + (a - 1.0) * ka_ref[...])
    b = kkn * a

    cum = _mm_rsplit(ltri_ref[...], logw)
    w_inc = jnp.exp(cum)
    w_exc = jnp.exp(cum - logw)
    w_inv = jnp.exp(-cum)
    rt = r * w_inc
    at = kkn * w_exc
    kt = k2 * w_inv
    bt = b * w_inv
    w_last = w_inc[L - 1:L, :]
    kl = kt * w_last
    bl = bt * w_last

    strict, incl = _tri_masks(L)
    for h in range(H_A):
        sl = slice(h * HEAD, (h + 1) * HEAD)
        s_h = s_scr[h]
        rhs = jnp.concatenate([bt[:, sl], kt[:, sl]], axis=0)
        g_a = _mm_nt(at[:, sl], rhs)
        g_r = _mm_nt(rt[:, sl], rhs)
        a_ab = jnp.where(strict, g_a[:, :L], 0.0)
        a_ak = jnp.where(strict, g_a[:, L:], 0.0)
        r_b = jnp.where(incl, g_r[:, :L], 0.0)
        r_k = jnp.where(incl, g_r[:, L:], 0.0)
        v_h = v[:, sl]
        base = _mm_nt(at[:, sl], s_h) + _mm(a_ak, v_h)
        u = base + _mm(_tri_inv_minus_eye(a_ab), base)
        y_scr[:, sl] = _mm_nt(rt[:, sl], s_h) + _mm(r_k, v_h) - _mm(r_b, u)
        s_scr[h] = s_h * w_last[:, sl] + _mm_tn(v_h, kl[:, sl]) - _mm_tn(u, bl[:, sl])

    y = y_scr[...]
    mean = seg_sum(y) * (1.0 / HEAD)
    d = y - mean
    var = seg_sum(d * d) * (1.0 / HEAD)
    yn = d * lax.rsqrt(var + GN_EPS) * lnw_ref[...] + lnb_ref[...]
    bonus = seg_sum(r * k2 * rk_ref[...]) * v
    y_ref[0] = (yn + bonus) * g

    @pl.when(t == pl.num_programs(1) - 1)
    def _():
        sout_ref[0] = s_scr[...]


def _rwkv_mix(proj3, shift0, s0, p, consts):
    bsz, t_len, _ = proj3.shape
    L = min(CHUNK, t_len)
    assert t_len % L == 0
    row = lambda n: _const_spec((1, n))
    return pl.pallas_call(
        _rwkv_kernel,
        grid=(bsz, t_len // L),
        in_specs=[pl.BlockSpec((1, L, RWKV_PROJ), lambda b, t: (b, t, OFF_RWKV // RWKV_PROJ)),
                  pl.BlockSpec((1, 1, RWKV_PROJ), lambda b, t: (b, 0, 0)),
                  pl.BlockSpec((1, H_A, HEAD, HEAD), lambda b, t: (b, 0, 0, 0)),
                  row(RWKV_PROJ), row(W_A), _const_spec((DECAY_RANK, W_A)), row(W_A),
                  _const_spec((ICLR_RANK, W_A)), _const_spec((GATE_RANK, W_A)),
                  row(W_A), row(W_A), row(W_A), row(W_A), row(W_A),
                  _const_spec((W_A, W_A)), _const_spec((L, L))],
        out_specs=[pl.BlockSpec((1, L, W_A), lambda b, t: (b, t, 0)),
                   pl.BlockSpec((1, H_A, HEAD, HEAD), lambda b, t: (b, 0, 0, 0))],
        out_shape=[jax.ShapeDtypeStruct((bsz, t_len, W_A), F32),
                   jax.ShapeDtypeStruct((bsz, H_A, HEAD, HEAD), F32)],
        scratch_shapes=[pltpu.VMEM((H_A, HEAD, HEAD), F32),
                        pltpu.VMEM((8, RWKV_PROJ), F32),
                        pltpu.VMEM((L, W_A), F32)],
        compiler_params=_params("parallel", "arbitrary"),
    )(proj3, shift0, s0, p["rwkv_mu"], p["rwkv_w0"], p["rwkv_w2"], p["rwkv_a0"], p["rwkv_a2"],
      p["rwkv_g2"], p["rwkv_kk"], p["rwkv_ka"], p["rwkv_rk"], p["rwkv_ln_w"], p["rwkv_ln_b"],
      consts["bones"], consts["ltri"][L])


def _gdn_kernel(qkv_ref, ab_ref, z_ref, conv0_ref, s0_ref, cw_ref, alog_ref, dtb_ref, nw_ref,
                bones_ref, ltri_ref, ea_ref, eb_ref, sel_ref,
                y_ref, sout_ref, s_scr, xpad_scr, o_scr):
    L = qkv_ref.shape[1]
    t = pl.program_id(1)
    npre = GDN_CONV - 1
    base_row = 8

    @pl.when(t == 0)
    def _():
        s_scr[...] = s0_ref[0]
        xpad_scr[base_row - npre:base_row, :] = conv0_ref[0]

    x = qkv_ref[0]
    xpad_scr[base_row:base_row + L, :] = x
    conv = x * cw_ref[npre:npre + 1, :]
    for j in range(npre):
        conv = conv + xpad_scr[base_row - npre + j:base_row - npre + j + L, :] * cw_ref[j:j + 1, :]
    xpad_scr[base_row - npre:base_row, :] = xpad_scr[base_row + L - npre:base_row + L, :]
    qkv = conv * _sigmoid(conv)

    bones = bones_ref[...]
    seg_sum = lambda a: _mm_lsplit(a, bones)
    q = qkv[:, 0:W_B]
    k = qkv[:, W_B:2 * W_B]
    v = qkv[:, 2 * W_B:]
    q = q * lax.rsqrt(seg_sum(q * q) + 1e-6) * (HEAD ** -0.5)
    k = k * lax.rsqrt(seg_sum(k * k) + 1e-6)

    ab = ab_ref[0]
    g_row = -jnp.exp(alog_ref[...]) * _softplus(ab + dtb_ref[...])
    g_exp = _mm_lsplit(g_row, ea_ref[...])
    beta = _mm_lsplit(_sigmoid(ab), eb_ref[...])
    gc = _mm_rsplit(ltri_ref[...], g_exp)
    gc_hi, gc_lo = _split(gc)
    sel = sel_ref[...]
    gc_t = (lax.dot_general(sel, gc_hi, (((1,), (1,)), ((), ())), preferred_element_type=F32)
            + lax.dot_general(sel, gc_lo, (((1,), (1,)), ((), ())), preferred_element_type=F32))
    g_last = gc[L - 1:L, :]
    e_gc = jnp.exp(gc)
    e_last = jnp.exp(g_last)
    kbeta = k * beta
    vbeta = v * beta
    kw = kbeta * e_gc
    qe = q * e_gc
    kd = k * jnp.exp(g_last - gc)

    strict, incl = _tri_masks(L)
    for h in range(H_B):
        sl = slice(h * HEAD, (h + 1) * HEAD)
        s_h = s_scr[h]
        diff = gc[:, h * HEAD:h * HEAD + L] - gc_t[h:h + 1, :]
        gamma = jnp.where(incl, jnp.exp(jnp.minimum(diff, 0.0)), 0.0)
        k_h = k[:, sl]
        kk = _mm_nt(jnp.concatenate([kbeta[:, sl], q[:, sl]], axis=0), k_h)
        lower = jnp.where(strict, kk[:L] * gamma, 0.0)
        att = kk[L:] * gamma
        tp = _tri_inv_minus_eye(lower)
        u_pre = vbeta[:, sl] + _mm(tp, vbeta[:, sl])
        w = kw[:, sl] + _mm(tp, kw[:, sl])
        u = u_pre - _mm(w, s_h)
        o_scr[:, sl] = _mm(qe[:, sl], s_h) + _mm(att, u)
        s_scr[h] = s_h * e_last[:, sl] + _mm_tn(kd[:, sl], u)

    o = o_scr[...]
    on = o * lax.rsqrt(seg_sum(o * o) * (1.0 / HEAD) + EPS) * nw_ref[...]
    z = z_ref[0]
    y_ref[0] = on * (z * _sigmoid(z))

    @pl.when(t == pl.num_programs(1) - 1)
    def _():
        sout_ref[0] = s_scr[...]


def _gdn_mix(proj3, conv0, s0, p, consts):
    bsz, t_len, _ = proj3.shape
    L = min(CHUNK, t_len)
    assert t_len % L == 0 and L >= GDN_CONV - 1
    return pl.pallas_call(
        _gdn_kernel,
        grid=(bsz, t_len // L),
        in_specs=[pl.BlockSpec((1, L, GDN_CH), lambda b, t: (b, t, OFF_QKV // GDN_CH)),
                  pl.BlockSpec((1, L, AB_PAD), lambda b, t: (b, t, OFF_AB // AB_PAD)),
                  pl.BlockSpec((1, L, W_B), lambda b, t: (b, t, OFF_Z // W_B)),
                  pl.BlockSpec((1, GDN_CONV - 1, GDN_CH), lambda b, t: (b, 0, 0)),
                  pl.BlockSpec((1, H_B, HEAD, HEAD), lambda b, t: (b, 0, 0, 0)),
                  _const_spec((GDN_CONV, GDN_CH)), _const_spec((1, AB_PAD)), _const_spec((1, AB_PAD)),
                  _const_spec((1, W_B)), _const_spec((W_B, W_B)), _const_spec((L, L)),
                  _const_spec((AB_PAD, W_B)), _const_spec((AB_PAD, W_B)), _const_spec((8, W_B))],
        out_specs=[pl.BlockSpec((1, L, W_B), lambda b, t: (b, t, 0)),
                   pl.BlockSpec((1, H_B, HEAD, HEAD), lambda b, t: (b, 0, 0, 0))],
        out_shape=[jax.ShapeDtypeStruct((bsz, t_len, W_B), F32),
                   jax.ShapeDtypeStruct((bsz, H_B, HEAD, HEAD), F32)],
        scratch_shapes=[pltpu.VMEM((H_B, HEAD, HEAD), F32),
                        pltpu.VMEM((8 + L, GDN_CH), F32),
                        pltpu.VMEM((L, W_B), F32)],
        compiler_params=_params("parallel", "arbitrary"),
    )(proj3, proj3, proj3, conv0, s0, p["gdn_conv_w"], p["gdn_A_log"], p["gdn_dt_bias"],
      p["gdn_norm_w"], consts["bones"], consts["ltri"][L], consts["ea"], consts["eb"], consts["sel"])


def _attend(q, kwin, vwin, bias_ref, valid, write):
    for h in range(H_C):
        sl = slice(h * HEAD, (h + 1) * HEAD)
        s = _mm_nt(q[:, sl], kwin[:, sl]) * (HEAD ** -0.5) + bias_ref[h]
        if valid is not None:
            s = jnp.where(valid, s, -jnp.inf)
        e = jnp.exp(s - jnp.max(s, axis=-1, keepdims=True))
        denom = jnp.sum(e, axis=-1, keepdims=True)
        write(h, _mm(e, vwin[:, sl]) / denom)


def _band_prompt_kernel(q_ref, k_ref, v_ref, bias_ref, o_ref, kpad_scr, vpad_scr):
    ci = pl.program_id(1)
    lead = N_BAND_PREV * CHUNK
    band = lead + CHUNK

    @pl.when(ci == 0)
    def _():
        zeros = jnp.zeros((lead, W_C), BF16)
        kpad_scr[0:lead, :] = zeros
        vpad_scr[0:lead, :] = zeros
        kpad_scr[lead:, :] = k_ref[0].astype(BF16)
        vpad_scr[lead:, :] = v_ref[0].astype(BF16)

    start = pl.multiple_of(ci * CHUNK, CHUNK)
    kwin = kpad_scr[pl.ds(start, band), :]
    vwin = vpad_scr[pl.ds(start, band), :]
    col = lax.broadcasted_iota(jnp.int32, (CHUNK, band), 1)
    valid = col >= (N_BAND_PREV - ci) * CHUNK

    def write(h, val):
        o_ref[0, :, h * HEAD:(h + 1) * HEAD] = val

    _attend(q_ref[0], kwin, vwin, bias_ref, valid, write)


def _band_prompt(proj3, bias):
    bsz, t_len, _ = proj3.shape
    assert t_len % CHUNK == 0
    lead = N_BAND_PREV * CHUNK
    band = lead + CHUNK
    cq = OFF_ATT // W_C
    return pl.pallas_call(
        _band_prompt_kernel,
        grid=(bsz, t_len // CHUNK),
        in_specs=[pl.BlockSpec((1, CHUNK, W_C), lambda b, c: (b, c, cq)),
                  pl.BlockSpec((1, t_len, W_C), lambda b, c: (b, 0, cq + 1)),
                  pl.BlockSpec((1, t_len, W_C), lambda b, c: (b, 0, cq + 2)),
                  _const_spec((H_C, CHUNK, band))],
        out_specs=pl.BlockSpec((1, CHUNK, W_C), lambda b, c: (b, c, 0)),
        out_shape=jax.ShapeDtypeStruct((bsz, t_len, W_C), F32),
        scratch_shapes=[pltpu.VMEM((lead + t_len, W_C), BF16),
                        pltpu.VMEM((lead + t_len, W_C), BF16)],
        compiler_params=_params("parallel", "arbitrary"),
    )(proj3, proj3, proj3, bias)


def _band_step_kernel(q_ref, k_ref, v_ref, bias_ref, o_ref):
    def write(h, val):
        o_ref[0, :, h * HEAD:(h + 1) * HEAD] = val

    _attend(q_ref[0], k_ref[0].astype(BF16), v_ref[0].astype(BF16), bias_ref, None, write)


def _band_step(proj3, k_all, v_all, bias):
    bsz, t_len, _ = proj3.shape
    n_keys = k_all.shape[1]
    return pl.pallas_call(
        _band_step_kernel,
        grid=(bsz,),
        in_specs=[pl.BlockSpec((1, t_len, W_C), lambda b: (b, 0, OFF_ATT // W_C)),
                  pl.BlockSpec((1, n_keys, W_C), lambda b: (b, 0, 0)),
                  pl.BlockSpec((1, n_keys, W_C), lambda b: (b, 0, 0)),
                  _const_spec((H_C, t_len, n_keys))],
        out_specs=pl.BlockSpec((1, t_len, W_C), lambda b: (b, 0, 0)),
        out_shape=jax.ShapeDtypeStruct((bsz, t_len, W_C), F32),
        compiler_params=_params("parallel"),
    )(proj3, k_all, v_all, bias)


def _rel_bias_tile(rel_bias, n_q, n_k, lead):
    d = lead + np.arange(n_q)[:, None] - np.arange(n_k)[None, :]
    return rel_bias[:, np.clip(d, -MAX_REL, MAX_REL) + MAX_REL]


def _merge_kernel(x_ref, ya_ref, yb_ref, yc_ref, g0_ref, g1_ref, g2_ref, wa_ref, wb_ref, wc_ref,
                  wo_ref, gn_ref, o_ref):
    merged = (_sigmoid(g0_ref[...]) * _mm(ya_ref[...], wa_ref[...])
              + _sigmoid(g1_ref[...]) * _mm(yb_ref[...], wb_ref[...])
              + _sigmoid(g2_ref[...]) * _mm(yc_ref[...], wc_ref[...]))
    m2 = _mm(merged, wo_ref[...])
    o_ref[...] = x_ref[...] + m2 * lax.rsqrt(jnp.mean(m2 * m2, axis=-1, keepdims=True) + EPS) * gn_ref[...]


def _merge(x2d, ya, yb, yc, proj2, p):
    m = x2d.shape[0]
    tm = min(TM_MERGE, m)
    tok = lambda n, c=0: pl.BlockSpec((tm, n), lambda i, c=c: (i, c))
    return pl.pallas_call(
        _merge_kernel,
        grid=(m // tm,),
        in_specs=[tok(D_MODEL), tok(W_A), tok(W_B), tok(W_C),
                  tok(D_MODEL, 0), tok(D_MODEL, 1), tok(D_MODEL, 2),
                  _const_spec((W_A, D_MODEL)), _const_spec((W_B, D_MODEL)), _const_spec((W_C, D_MODEL)),
                  _const_spec((D_MODEL, D_MODEL)), _const_spec((1, D_MODEL))],
        out_specs=tok(D_MODEL),
        out_shape=jax.ShapeDtypeStruct((m, D_MODEL), F32),
        compiler_params=_params("parallel"),
    )(x2d, ya, yb, yc, proj2, proj2, proj2, p["w_br_a"], p["w_br_b"], p["w_br_c"], p["w_out"],
      p["g_post_mix"])


def _ffn_kernel(x_ref, gpre_ref, up_ref, down_ref, gpost_ref, o_ref):
    x = x_ref[...]
    h = (x * lax.rsqrt(jnp.mean(x * x, axis=-1, keepdims=True) + EPS) * gpre_ref[...]).astype(BF16)
    f = jnp.zeros(x.shape, F32)
    for n in range(D_FF // FF_CHUNK):
        cols = slice(n * FF_CHUNK, (n + 1) * FF_CHUNK)
        act = jnp.maximum(jnp.dot(h, up_ref[:, cols], preferred_element_type=F32), 0.0)
        f = f + jnp.dot((act * act).astype(BF16), down_ref[cols, :], preferred_element_type=F32)
    o_ref[...] = x + f * lax.rsqrt(jnp.mean(f * f, axis=-1, keepdims=True) + EPS) * gpost_ref[...]


def _ffn(x2d, p):
    m = x2d.shape[0]
    tm = min(TM_FFN, m)
    return pl.pallas_call(
        _ffn_kernel,
        grid=(m // tm,),
        in_specs=[pl.BlockSpec((tm, D_MODEL), lambda i: (i, 0)),
                  _const_spec((1, D_MODEL)), _const_spec((D_MODEL, D_FF)), _const_spec((D_FF, D_MODEL)),
                  _const_spec((1, D_MODEL))],
        out_specs=pl.BlockSpec((tm, D_MODEL), lambda i: (i, 0)),
        out_shape=jax.ShapeDtypeStruct((m, D_MODEL), F32),
        compiler_params=_params("parallel"),
    )(x2d, p["g_pre_ffn"], p["w_ff_up"], p["w_ff_down"], p["g_post_ffn"])


def _constants():
    head_of = np.arange(W_A) // HEAD
    bones = (head_of[:, None] == head_of[None, :]).astype(np.float32)
    ea = np.zeros((AB_PAD, W_B), np.float32)
    eb = np.zeros((AB_PAD, W_B), np.float32)
    sel = np.zeros((8, W_B), np.float32)
    for h in range(H_B):
        ea[h, h * HEAD:(h + 1) * HEAD] = 1.0
        eb[H_B + h, h * HEAD:(h + 1) * HEAD] = 1.0
        sel[h, h * HEAD] = 1.0
    ltri = {n: jnp.asarray(np.tril(np.ones((n, n), np.float32)), BF16) for n in (32, 64)}
    return {"bones": jnp.asarray(bones, BF16), "ea": jnp.asarray(ea, BF16), "eb": jnp.asarray(eb, BF16),
            "sel": jnp.asarray(sel, BF16), "ltri": ltri}


def _stage_layer_params(l, g_pre_mix, g_post_mix, g_pre_ffn, g_post_ffn, w_in, rwkv_mu, rwkv_w0, rwkv_w2,
                        rwkv_a0, rwkv_a2, rwkv_g2, rwkv_kk, rwkv_ka, rwkv_rk, rwkv_ln_w, rwkv_ln_b,
                        gdn_conv_w, gdn_A_log, gdn_dt_bias, gdn_norm_w, att_rel_bias, w_br_a, w_br_b,
                        w_br_c, w_out, w_ff_up, w_ff_down):
    row = lambda a: a[l].reshape(1, -1).astype(F32)
    w = w_in[l]
    o_qkv = RWKV_PROJ
    o_a = o_qkv + GDN_CH
    o_z = o_a + 2 * H_B
    o_att = o_z + W_B
    o_gate = o_att + 3 * W_C
    ab_w = jnp.pad(w[:, o_a:o_z], ((0, 0), (0, AB_PAD - 2 * H_B)))
    w_perm = jnp.concatenate([w[:, o_gate:], w[:, o_att:o_gate], w[:, o_z:o_att], w[:, :RWKV_PROJ],
                              ab_w, w[:, o_qkv:o_a]], axis=1).astype(BF16)
    pad_ab = lambda a, lo: jnp.pad(a[l].astype(F32), (lo, AB_PAD - H_B - lo)).reshape(1, AB_PAD)
    return {
        "g_pre_mix": row(g_pre_mix), "g_post_mix": row(g_post_mix),
        "g_pre_ffn": row(g_pre_ffn), "g_post_ffn": row(g_post_ffn),
        "w_in": w_perm,
        "rwkv_mu": row(rwkv_mu), "rwkv_w0": row(rwkv_w0), "rwkv_w2": rwkv_w2[l].astype(BF16),
        "rwkv_a0": row(rwkv_a0), "rwkv_a2": rwkv_a2[l].astype(BF16), "rwkv_g2": rwkv_g2[l].astype(BF16),
        "rwkv_kk": row(rwkv_kk), "rwkv_ka": row(rwkv_ka), "rwkv_rk": row(rwkv_rk),
        "rwkv_ln_w": row(rwkv_ln_w), "rwkv_ln_b": row(rwkv_ln_b),
        "gdn_conv_w": gdn_conv_w[l].astype(F32), "gdn_A_log": pad_ab(gdn_A_log, 0),
        "gdn_dt_bias": pad_ab(gdn_dt_bias, 0),
        "gdn_norm_w": jnp.tile(gdn_norm_w[l].astype(F32), H_B).reshape(1, W_B),
        "att_rel_bias": att_rel_bias[l].astype(F32),
        "w_br_a": w_br_a[l].astype(BF16), "w_br_b": w_br_b[l].astype(BF16), "w_br_c": w_br_c[l].astype(BF16),
        "w_out": w_out[l].astype(BF16), "w_ff_up": w_ff_up[l].astype(BF16),
        "w_ff_down": w_ff_down[l].astype(BF16),
    }


def _trunk_layer(x, p, consts, rwkv_shift, rwkv_s, gdn_conv, gdn_s, band_k, band_v):
    bsz, t_len, _ = x.shape
    m = bsz * t_len
    x2d = x.reshape(m, D_MODEL)
    proj2 = _in_proj(x2d, p["g_pre_mix"], p["w_in"])
    proj3 = proj2.reshape(bsz, t_len, N_PROJ)
    ya, s_a = _rwkv_mix(proj3, rwkv_shift.reshape(bsz, 1, RWKV_PROJ), rwkv_s, p, consts)
    yb, s_b = _gdn_mix(proj3, gdn_conv, gdn_s, p, consts)
    k_new = proj3[:, :, OFF_ATT + W_C:OFF_ATT + 2 * W_C]
    v_new = proj3[:, :, OFF_ATT + 2 * W_C:OFF_ATT + 3 * W_C]
    if band_k is None:
        lead = N_BAND_PREV * CHUNK
        yc = _band_prompt(proj3, _rel_bias_tile(p["att_rel_bias"], CHUNK, lead + CHUNK, lead))
        keep = min(lead, t_len)
        k_out, v_out = k_new[:, t_len - keep:], v_new[:, t_len - keep:]
    else:
        n_past = band_k.shape[1]
        k_all = jnp.concatenate([band_k.reshape(bsz, n_past, W_C), k_new], axis=1)
        v_all = jnp.concatenate([band_v.reshape(bsz, n_past, W_C), v_new], axis=1)
        yc = _band_step(proj3, k_all, v_all, _rel_bias_tile(p["att_rel_bias"], t_len, n_past + t_len, n_past))
        k_out, v_out = k_new, v_new
    x1 = _merge(x2d, ya.reshape(m, W_A), yb.reshape(m, W_B), yc.reshape(m, W_C), proj2, p)
    x2 = _ffn(x1, p)
    shift_new = proj3[:, t_len - 1, OFF_RWKV:OFF_RWKV + RWKV_PROJ]
    conv_new = proj3[:, t_len - (GDN_CONV - 1):, OFF_QKV:OFF_QKV + GDN_CH]
    new_state = (shift_new, s_a, conv_new, s_b,
                 k_out.reshape(bsz, -1, H_C, HEAD), v_out.reshape(bsz, -1, H_C, HEAD))
    return x2.reshape(bsz, t_len, D_MODEL), new_state


def kernel(x_prompt, x_sample, state_rwkv_shift, state_rwkv_wkv, state_gdn_conv, state_gdn_S, cache_band_k, cache_band_v, g_pre_mix, g_post_mix, g_pre_ffn, g_post_ffn, w_in, rwkv_mu, rwkv_w0, rwkv_w2, rwkv_a0, rwkv_a2, rwkv_g2, rwkv_kk, rwkv_ka, rwkv_rk, rwkv_ln_w, rwkv_ln_b, gdn_conv_w, gdn_A_log, gdn_dt_bias, gdn_norm_w, att_rel_bias, w_br_a, w_br_b, w_br_c, w_out, w_ff_up, w_ff_down):
    depth = w_in.shape[0]
    bp = x_prompt.shape[0]
    consts = _constants()
    xp, xs = x_prompt, x_sample
    p_new = [[] for _ in range(6)]
    s_new = [[] for _ in range(6)]
    for l in range(depth):
        p = _stage_layer_params(l, g_pre_mix, g_post_mix, g_pre_ffn, g_post_ffn, w_in, rwkv_mu, rwkv_w0,
                                rwkv_w2, rwkv_a0, rwkv_a2, rwkv_g2, rwkv_kk, rwkv_ka, rwkv_rk, rwkv_ln_w,
                                rwkv_ln_b, gdn_conv_w, gdn_A_log, gdn_dt_bias, gdn_norm_w, att_rel_bias,
                                w_br_a, w_br_b, w_br_c, w_out, w_ff_up, w_ff_down)
        xp, st_p = _trunk_layer(
            xp, p, consts,
            jnp.zeros((bp, RWKV_PROJ), F32), jnp.zeros((bp, H_A, HEAD, HEAD), F32),
            jnp.zeros((bp, GDN_CONV - 1, GDN_CH), F32), jnp.zeros((bp, H_B, HEAD, HEAD), F32),
            None, None)
        xs, st_s = _trunk_layer(
            xs, p, consts, state_rwkv_shift[l], state_rwkv_wkv[l], state_gdn_conv[l], state_gdn_S[l],
            cache_band_k[l], cache_band_v[l])
        for i in range(6):
            p_new[i].append(st_p[i])
            s_new[i].append(st_s[i])
    stk_p = [jnp.stack(a).astype(x_prompt.dtype) for a in p_new]
    stk_s = [jnp.stack(a).astype(x_sample.dtype) for a in s_new]
    return (xp, xs, *stk_p, *stk_s)
```

```python
import functools
import math

import numpy as np
import jax
import jax.numpy as jnp
from jax import lax
from jax.experimental import pallas as pl
from jax.experimental.pallas import tpu as pltpu

F32 = jnp.float32
BF16 = jnp.bfloat16

D_MODEL = 1024
HEAD = 64
H_A = 6
W_A = H_A * HEAD
DECAY_RANK = 64
ICLR_RANK = 64
GATE_RANK = 128
RWKV_PROJ = 3 * W_A + DECAY_RANK + ICLR_RANK + GATE_RANK
H_B = 6
W_B = H_B * HEAD
GDN_CONV = 4
GDN_CH = 3 * W_B
H_C = 4
W_C = H_C * HEAD
N_BAND_PREV = 8
MAX_REL = 128
CHUNK = 64
D_FF = 4 * D_MODEL
EPS = 1e-6
GN_EPS = 64e-5

LANE = 128
SUBLANE = 8
AB_PAD = LANE
OFF_GATE = 0
OFF_ATT = OFF_GATE + 3 * D_MODEL
OFF_Z = OFF_ATT + 3 * W_C
OFF_RWKV = OFF_Z + W_B
OFF_AB = OFF_RWKV + RWKV_PROJ
OFF_QKV = OFF_AB + AB_PAD
N_PROJ = OFF_QKV + GDN_CH
assert OFF_ATT % W_C == 0 and OFF_Z % W_B == 0 and OFF_RWKV % RWKV_PROJ == 0
assert OFF_AB % AB_PAD == 0 and OFF_QKV % GDN_CH == 0 and N_PROJ % LANE == 0

VMEM_LIMIT = 56 * 1024 * 1024
TM_IN = 256
TN_IN = 768
TM_MERGE = 512
TM_FFN = 512
FF_CHUNK = 1024
MIX_GROUP = 4


def _params(*sem):
    return pltpu.CompilerParams(dimension_semantics=sem, vmem_limit_bytes=VMEM_LIMIT)


def _const_spec(shape):
    nd = len(shape)
    return pl.BlockSpec(shape, lambda *_: (0,) * nd)


def _bdot(a, b):
    return jnp.dot(a, b, preferred_element_type=F32)


def _bdot_nt(a, b):
    return lax.dot_general(a, b, (((1,), (1,)), ((), ())), preferred_element_type=F32)


def _bdot_tn(a, b):
    return lax.dot_general(a, b, (((0,), (0,)), ((), ())), preferred_element_type=F32)


def _mm(a, b):
    return _bdot(a.astype(BF16), b.astype(BF16))


def _split(a):
    hi = a.astype(BF16)
    lo = (a - hi.astype(F32)).astype(BF16)
    return hi, lo


def _mm_lsplit(a, b_exact):
    hi, lo = _split(a)
    return _bdot(hi, b_exact) + _bdot(lo, b_exact)


def _mm_rsplit(a_exact, b):
    hi, lo = _split(b)
    return _bdot(a_exact, hi) + _bdot(a_exact, lo)


def _sigmoid(x):
    return 1.0 / (1.0 + jnp.exp(-x))


def _softplus(x):
    return jnp.maximum(x, 0.0) + jnp.log(1.0 + jnp.exp(-jnp.abs(x)))


def _rows_from(ref_rows, n):
    return jnp.concatenate([jnp.broadcast_to(r, (n, r.shape[-1])) for r in ref_rows], axis=0)


def _in_proj_kernel(x_ref, g_ref, w_ref, o_ref):
    x = x_ref[...]
    h = (x * lax.rsqrt(jnp.mean(x * x, axis=-1, keepdims=True) + EPS) * g_ref[...]).astype(BF16)
    for n in range(N_PROJ // TN_IN):
        cols = slice(n * TN_IN, (n + 1) * TN_IN)
        o_ref[:, cols] = _bdot(h, w_ref[:, cols])


def _in_proj(x2d, g, w_perm):
    m = x2d.shape[0]
    tm = min(TM_IN, m)
    return pl.pallas_call(
        _in_proj_kernel,
        grid=(m // tm,),
        in_specs=[pl.BlockSpec((tm, D_MODEL), lambda i: (i, 0)),
                  _const_spec((1, D_MODEL)),
                  _const_spec((D_MODEL, N_PROJ))],
        out_specs=pl.BlockSpec((tm, N_PROJ), lambda i: (i, 0)),
        out_shape=jax.ShapeDtypeStruct((m, N_PROJ), F32),
        compiler_params=_params("parallel"),
        name="in_proj",
    )(x2d, g, w_perm)


PACK = 4
GROUP_W = PACK * HEAD
GROUPS_PER_PAIR = 2 * H_A // PACK
N_TRI_LEVELS = int(math.log2(HEAD))
assert H_A == H_B and 2 * H_A % PACK == 0 and GROUP_W % LANE == 0 and CHUNK == HEAD


def _group_slots(pair_group):
    if pair_group < 2:
        return [(pair_group, h) for h in range(PACK)]
    return [(0, PACK), (0, PACK + 1), (1, PACK), (1, PACK + 1)]


def _to_groups(x, L):
    out = []
    for a in range(0, x.shape[0] // L, 2):
        ra, rb = slice(a * L, (a + 1) * L), slice((a + 1) * L, (a + 2) * L)
        out += [x[ra, 0:GROUP_W], x[rb, 0:GROUP_W],
                jnp.concatenate([x[ra, GROUP_W:W_A], x[rb, GROUP_W:W_A]], axis=1)]
    return out


def _from_groups(vals):
    rows = []
    half = (W_A - GROUP_W)
    for p in range(len(vals) // GROUPS_PER_PAIR):
        ga, gb, gc = vals[3 * p:3 * p + 3]
        rows.append(jnp.concatenate([ga, gc[:, 0:half]], axis=1))
        rows.append(jnp.concatenate([gb, gc[:, half:2 * half]], axis=1))
    return jnp.concatenate(rows, axis=0)


def _block_diag(x_b, mask_b):
    return jnp.concatenate([x_b] * PACK, axis=0) * mask_b


def _packed_tri_masks(L):
    r = lax.broadcasted_iota(jnp.int32, (L, GROUP_W), 0)
    c = jnp.bitwise_and(lax.broadcasted_iota(jnp.int32, (L, GROUP_W), 1), HEAD - 1)
    return c < r, c <= r, c == r


def _tri_inv_minus_eye_packed(a_list, mask_b, lvl_ref):
    tp = [-(a * lvl_ref[0]) for a in a_list]
    for lvl in range(1, N_TRI_LEVELS):
        m = lvl_ref[lvl]
        off = [a * m for a in a_list]
        off_bd = [_block_diag(x.astype(BF16), mask_b) for x in off]
        tp_b = [t.astype(BF16) for t in tp]
        m1 = [x + _bdot(tb, bd) for x, tb, bd in zip(off, tp_b, off_bd)]
        tp_bd = [_block_diag(tb, mask_b) for tb in tp_b]
        tp = [t - x - _bdot(x.astype(BF16), bd) for t, x, bd in zip(tp, m1, tp_bd)]
    return tp


def _seg_sum(x, ones_b, exact):
    outs = []
    for j in range(W_A // LANE):
        xs = x[:, j * LANE:(j + 1) * LANE]
        outs.append(_mm_lsplit(xs, ones_b) if exact else _bdot(xs.astype(BF16), ones_b))
    return jnp.concatenate(outs, axis=1)


def _load_state(s_scr, s0_ref):
    s_scr[...] = jnp.zeros(s_scr.shape, F32)
    for gi in range(s_scr.shape[0]):
        for j, (q, h) in enumerate(_group_slots(gi % GROUPS_PER_PAIR)):
            seq = 2 * (gi // GROUPS_PER_PAIR) + q
            s_scr[gi, j * HEAD:(j + 1) * HEAD, j * HEAD:(j + 1) * HEAD] = s0_ref[seq, h]


def _store_state(sout_ref, s_scr):
    for gi in range(s_scr.shape[0]):
        for j, (q, h) in enumerate(_group_slots(gi % GROUPS_PER_PAIR)):
            seq = 2 * (gi // GROUPS_PER_PAIR) + q
            sout_ref[seq, h] = s_scr[gi, j * HEAD:(j + 1) * HEAD, j * HEAD:(j + 1) * HEAD]


def _rwkv_kernel(c_ref, shift0_ref, s0_ref, mu_ref, w0_ref, w2_ref, a0_ref, a2_ref, g2_ref,
                 kk_ref, ka_ref, rk_ref, lnw_ref, lnb_ref, ones_ref, ltri_ref, bdb_ref, bdf_ref, lvl_ref,
                 y_ref, sout_ref, s_scr, prev_scr, *, n_valid):
    G, L = c_ref.shape[0], c_ref.shape[1]
    t = pl.program_id(1)

    @pl.when(t == 0)
    def _():
        _load_state(s_scr, s0_ref)
        prev_scr[...] = shift0_ref[...]

    c = c_ref[...].reshape(G * L, RWKV_PROJ)
    row = jnp.bitwise_and(lax.broadcasted_iota(jnp.int32, c.shape, 0), L - 1)
    prev_rows = _rows_from([prev_scr[g] for g in range(G)], L)
    c_prev = jnp.where(row == 0, prev_rows, pltpu.roll(c, 1, 0))
    for g in range(G):
        prev_scr[g] = c[(g + 1) * L - 1:(g + 1) * L, :]
    cm = c + (c_prev - c) * mu_ref[...]
    r = cm[:, 0:W_A]
    k = cm[:, W_A:2 * W_A]
    v = cm[:, 2 * W_A:3 * W_A]
    o = 3 * W_A
    wd = cm[:, o:o + DECAY_RANK]
    ad = cm[:, o + DECAY_RANK:o + DECAY_RANK + ICLR_RANK]
    gd = cm[:, o + DECAY_RANK + ICLR_RANK:]

    ones_b = ones_ref[...]
    w_log = -_softplus(-(w0_ref[...] + _mm(jnp.tanh(wd), w2_ref[...]))) - 0.5
    logw = -jnp.exp(w_log)
    a = _sigmoid(a0_ref[...] + _mm(ad, a2_ref[...]))
    gate = _mm(_sigmoid(gd), g2_ref[...])
    kks = k * kk_ref[...]
    kkn = kks * lax.rsqrt(_seg_sum(kks * kks, ones_b, False) + 1e-6)
    k2 = k * (1.0 + (a - 1.0) * ka_ref[...])
    if n_valid < L:
        live = jnp.bitwise_and(lax.broadcasted_iota(jnp.int32, k.shape, 0), L - 1) < n_valid
        logw = jnp.where(live, logw, 0.0)
        kkn = jnp.where(live, kkn, 0.0)
        k2 = jnp.where(live, k2, 0.0)
    b = kkn * a

    cum = _mm_rsplit(ltri_ref[...], logw)
    w_inc = jnp.exp(cum)
    w_inv = jnp.exp(-cum)
    w_last = _rows_from([w_inc[(g + 1) * L - 1:(g + 1) * L, :] for g in range(G)], L)
    kt = k2 * w_inv
    bt = b * w_inv
    groups = lambda x: _to_groups(x, L)
    rt_g = groups((r * w_inc).astype(BF16))
    at_g = groups((kkn * jnp.exp(cum - logw)).astype(BF16))
    kt_g = groups(kt.astype(BF16))
    bt_g = groups(bt.astype(BF16))
    kl_g = groups((kt * w_last).astype(BF16))
    bl_neg_g = groups((-(bt * w_last)).astype(BF16))
    v_g = groups(v.astype(BF16))
    wl_g = groups(w_last)

    mask_b = bdb_ref[...]
    mask_f = bdf_ref[...]
    strict, incl, _ = _packed_tri_masks(L)
    n_groups = len(v_g)
    s_old = [s_scr[i] for i in range(n_groups)]
    s_old_b = [s.astype(BF16) for s in s_old]
    lhs_ar = [jnp.concatenate([x, y], axis=0) for x, y in zip(at_g, rt_g)]
    bt_bd = [_block_diag(x, mask_b) for x in bt_g]
    kt_bd = [_block_diag(x, mask_b) for x in kt_g]
    v_bd = [_block_diag(x, mask_b) for x in v_g]
    g_b = [_bdot_nt(x, bd) for x, bd in zip(lhs_ar, bt_bd)]
    g_k = [_bdot_nt(x, bd) for x, bd in zip(lhs_ar, kt_bd)]
    a_ab = [jnp.where(strict, x[:L], 0.0) for x in g_b]
    r_b_b = [jnp.where(incl, x[L:], 0.0).astype(BF16) for x in g_b]
    lhs_k = [jnp.concatenate([jnp.where(strict, x[:L], 0.0), jnp.where(incl, x[L:], 0.0)], axis=0).astype(BF16)
             for x in g_k]
    sv = [_bdot_nt(x, s) + _bdot(lk, vbd) for x, s, lk, vbd in zip(lhs_ar, s_old_b, lhs_k, v_bd)]
    tp = _tri_inv_minus_eye_packed(a_ab, mask_b, lvl_ref)
    u = [x[:L] + _bdot(t_.astype(BF16), _block_diag(x[:L].astype(BF16), mask_b))
         for x, t_ in zip(sv, tp)]
    u_b = [x.astype(BF16) for x in u]
    y_g = [x[L:] - _bdot(rb, _block_diag(ub, mask_b)) for x, rb, ub in zip(sv, r_b_b, u_b)]
    for i in range(n_groups):
        upd = _bdot_tn(jnp.concatenate([v_g[i], u_b[i]], axis=0),
                       jnp.concatenate([kl_g[i], bl_neg_g[i]], axis=0))
        s_scr[i] = s_old[i] * wl_g[i][0:1, :] + upd * mask_f

    y = _from_groups(y_g)
    mean = _seg_sum(y, ones_b, True) * (1.0 / HEAD)
    d = y - mean
    var = _seg_sum(d * d, ones_b, False) * (1.0 / HEAD)
    yn = d * lax.rsqrt(var + GN_EPS) * lnw_ref[...] + lnb_ref[...]
    bonus = _seg_sum(r * k2 * rk_ref[...], ones_b, True) * v
    y_ref[...] = ((yn + bonus) * gate).reshape(G, L, W_A)

    @pl.when(t == pl.num_programs(1) - 1)
    def _():
        _store_state(sout_ref, s_scr)


def _mixer_rows(proj3):
    t_len = proj3.shape[1]
    if t_len >= CHUNK:
        assert t_len % CHUNK == 0
        return proj3, CHUNK
    return jnp.pad(proj3, ((0, 0), (0, CHUNK - t_len), (0, 0))), t_len


def _rwkv_mix(proj3, shift0, s0, p, consts):
    bsz, t_len, _ = proj3.shape
    proj3, n_valid = _mixer_rows(proj3)
    t_pad = proj3.shape[1]
    L, G = CHUNK, MIX_GROUP
    assert bsz % G == 0 and G % 2 == 0
    row = lambda n: _const_spec((1, n))
    y, s_new = pl.pallas_call(
        functools.partial(_rwkv_kernel, n_valid=n_valid),
        grid=(bsz // G, t_pad // L),
        in_specs=[pl.BlockSpec((G, L, RWKV_PROJ), lambda b, t: (b, t, OFF_RWKV // RWKV_PROJ)),
                  pl.BlockSpec((G, 1, RWKV_PROJ), lambda b, t: (b, 0, 0)),
                  pl.BlockSpec((G, H_A, HEAD, HEAD), lambda b, t: (b, 0, 0, 0)),
                  row(RWKV_PROJ), row(W_A), _const_spec((DECAY_RANK, W_A)), row(W_A),
                  _const_spec((ICLR_RANK, W_A)), _const_spec((GATE_RANK, W_A)),
                  row(W_A), row(W_A), row(W_A), row(W_A), row(W_A),
                  _const_spec((LANE, LANE)), _const_spec((G * L, G * L)),
                  _const_spec((GROUP_W, GROUP_W)), _const_spec((GROUP_W, GROUP_W)),
                  _const_spec((N_TRI_LEVELS, CHUNK, GROUP_W))],
        out_specs=[pl.BlockSpec((G, L, W_A), lambda b, t: (b, t, 0)),
                   pl.BlockSpec((G, H_A, HEAD, HEAD), lambda b, t: (b, 0, 0, 0))],
        out_shape=[jax.ShapeDtypeStruct((bsz, t_pad, W_A), F32),
                   jax.ShapeDtypeStruct((bsz, H_A, HEAD, HEAD), F32)],
        scratch_shapes=[pltpu.VMEM((G // 2 * GROUPS_PER_PAIR, GROUP_W, GROUP_W), F32),
                        pltpu.VMEM((G, 1, RWKV_PROJ), F32)],
        compiler_params=_params("parallel", "arbitrary"),
        name="rwkv_mix",
    )(proj3, shift0, s0, p["rwkv_mu"], p["rwkv_w0"], p["rwkv_w2"], p["rwkv_a0"], p["rwkv_a2"],
      p["rwkv_g2"], p["rwkv_kk"], p["rwkv_ka"], p["rwkv_rk"], p["rwkv_ln_w"], p["rwkv_ln_b"],
      consts["ones"], _block_tril(G, L), consts["bd_b"], consts["bd_f"], consts["tri_levels"])
    return y[:, :t_len], s_new


def _gdn_kernel(qkv_ref, ab_ref, z_ref, conv0_ref, s0_ref, cw_ref, alog_ref, dtb_ref, nw_ref,
                ones_ref, ltri_ref, ea_ref, eb_ref, bdb_ref, bdf_ref, lvl_ref,
                y_ref, sout_ref, s_scr, xpad_scr, *, n_valid):
    G, L = qkv_ref.shape[0], qkv_ref.shape[1]
    t = pl.program_id(1)
    npre = GDN_CONV - 1
    base_row = SUBLANE

    @pl.when(t == 0)
    def _():
        _load_state(s_scr, s0_ref)
        xpad_scr[:, base_row - npre:base_row, :] = conv0_ref[...]

    convs = []
    for g in range(G):
        x = qkv_ref[g]
        xpad_scr[g, base_row:base_row + L, :] = x
        conv = x * cw_ref[npre:npre + 1, :]
        for j in range(npre):
            lo = base_row - npre + j
            conv = conv + xpad_scr[g, lo:lo + L, :] * cw_ref[j:j + 1, :]
        xpad_scr[g, base_row - npre:base_row, :] = xpad_scr[g, base_row + L - npre:base_row + L, :]
        convs.append(conv)
    conv = jnp.concatenate(convs, axis=0)
    qkv = conv * _sigmoid(conv)

    ones_b = ones_ref[...]
    q = qkv[:, 0:W_B]
    k = qkv[:, W_B:2 * W_B]
    v = qkv[:, 2 * W_B:]
    q = q * lax.rsqrt(_seg_sum(q * q, ones_b, False) + 1e-6) * (HEAD ** -0.5)
    k = k * lax.rsqrt(_seg_sum(k * k, ones_b, False) + 1e-6)

    ab = ab_ref[...].reshape(G * L, AB_PAD)
    g_row = -jnp.exp(alog_ref[...]) * _softplus(ab + dtb_ref[...])
    g_exp = _mm_lsplit(g_row, ea_ref[...])
    beta = _mm_lsplit(_sigmoid(ab), eb_ref[...])
    if n_valid < L:
        live = jnp.bitwise_and(lax.broadcasted_iota(jnp.int32, k.shape, 0), L - 1) < n_valid
        k = jnp.where(live, k, 0.0)
        beta = jnp.where(live, beta, 0.0)
        g_exp = jnp.where(live, g_exp, 0.0)
    gc = _mm_rsplit(ltri_ref[...], g_exp)
    g_last = _rows_from([gc[(g + 1) * L - 1:(g + 1) * L, :] for g in range(G)], L)
    e_gc = jnp.exp(gc)
    kbeta = k * beta
    vbeta = v * beta
    kw = kbeta * e_gc
    groups = lambda x: _to_groups(x, L)
    k_g = groups(k.astype(BF16))
    q_g = groups(q.astype(BF16))
    kbeta_g = groups(kbeta.astype(BF16))
    vbeta_g = groups(vbeta)
    kw_g = groups(kw)
    qe_g = groups((q * e_gc).astype(BF16))
    kd_g = groups((k * jnp.exp(g_last - gc)).astype(BF16))
    gc_g = groups(gc)
    el_g = groups(jnp.exp(g_last))

    mask_b = bdb_ref[...]
    mask_f = bdf_ref[...]
    strict, incl, eye = _packed_tri_masks(L)
    n_groups = len(k_g)
    s_old = [s_scr[i] for i in range(n_groups)]
    s_old_b = [s.astype(BF16) for s in s_old]
    gamma = []
    for x in gc_g:
        g_row_j = jnp.sum(jnp.where(eye, x, 0.0), axis=0, keepdims=True)
        gamma.append(jnp.where(incl, jnp.exp(jnp.minimum(x - g_row_j, 0.0)), 0.0))
    kk = [_bdot_nt(jnp.concatenate([x, y], axis=0), _block_diag(kb, mask_b))
          for x, y, kb in zip(kbeta_g, q_g, k_g)]
    lower = [jnp.where(strict, x[:L] * gm, 0.0) for x, gm in zip(kk, gamma)]
    att_b = [(x[L:] * gm).astype(BF16) for x, gm in zip(kk, gamma)]
    tp_b = [x.astype(BF16) for x in _tri_inv_minus_eye_packed(lower, mask_b, lvl_ref)]
    u_pre = [x + _bdot(t_, _block_diag(x.astype(BF16), mask_b)) for x, t_ in zip(vbeta_g, tp_b)]
    w_b = [(x + _bdot(t_, _block_diag(x.astype(BF16), mask_b))).astype(BF16) for x, t_ in zip(kw_g, tp_b)]
    ws = [_bdot(jnp.concatenate([w_, qe], axis=0), s) for w_, qe, s in zip(w_b, qe_g, s_old_b)]
    u_b = [(up - x[:L]).astype(BF16) for up, x in zip(u_pre, ws)]
    o_g = [x[L:] + _bdot(a_, _block_diag(ub, mask_b)) for x, a_, ub in zip(ws, att_b, u_b)]
    for i in range(n_groups):
        s_scr[i] = s_old[i] * el_g[i][0:1, :] + _bdot_tn(kd_g[i], u_b[i]) * mask_f

    o = _from_groups(o_g)
    on = o * lax.rsqrt(_seg_sum(o * o, ones_b, False) * (1.0 / HEAD) + EPS) * nw_ref[...]
    z = z_ref[...].reshape(G * L, W_B)
    y_ref[...] = (on * (z * _sigmoid(z))).reshape(G, L, W_B)

    @pl.when(t == pl.num_programs(1) - 1)
    def _():
        _store_state(sout_ref, s_scr)


def _gdn_mix(proj3, conv0, s0, p, consts):
    bsz, t_len, _ = proj3.shape
    assert t_len >= GDN_CONV - 1
    proj3, n_valid = _mixer_rows(proj3)
    t_pad = proj3.shape[1]
    L, G = CHUNK, MIX_GROUP
    assert bsz % G == 0 and G % 2 == 0
    y, s_new = pl.pallas_call(
        functools.partial(_gdn_kernel, n_valid=n_valid),
        grid=(bsz // G, t_pad // L),
        in_specs=[pl.BlockSpec((G, L, GDN_CH), lambda b, t: (b, t, OFF_QKV // GDN_CH)),
                  pl.BlockSpec((G, L, AB_PAD), lambda b, t: (b, t, OFF_AB // AB_PAD)),
                  pl.BlockSpec((G, L, W_B), lambda b, t: (b, t, OFF_Z // W_B)),
                  pl.BlockSpec((G, GDN_CONV - 1, GDN_CH), lambda b, t: (b, 0, 0)),
                  pl.BlockSpec((G, H_B, HEAD, HEAD), lambda b, t: (b, 0, 0, 0)),
                  _const_spec((GDN_CONV, GDN_CH)), _const_spec((1, AB_PAD)), _const_spec((1, AB_PAD)),
                  _const_spec((1, W_B)), _const_spec((LANE, LANE)), _const_spec((G * L, G * L)),
                  _const_spec((AB_PAD, W_B)), _const_spec((AB_PAD, W_B)),
                  _const_spec((GROUP_W, GROUP_W)), _const_spec((GROUP_W, GROUP_W)),
                  _const_spec((N_TRI_LEVELS, CHUNK, GROUP_W))],
        out_specs=[pl.BlockSpec((G, L, W_B), lambda b, t: (b, t, 0)),
                   pl.BlockSpec((G, H_B, HEAD, HEAD), lambda b, t: (b, 0, 0, 0))],
        out_shape=[jax.ShapeDtypeStruct((bsz, t_pad, W_B), F32),
                   jax.ShapeDtypeStruct((bsz, H_B, HEAD, HEAD), F32)],
        scratch_shapes=[pltpu.VMEM((G // 2 * GROUPS_PER_PAIR, GROUP_W, GROUP_W), F32),
                        pltpu.VMEM((G, SUBLANE + L, GDN_CH), F32)],
        compiler_params=_params("parallel", "arbitrary"),
        name="gdn_mix",
    )(proj3, proj3, proj3, conv0, s0, p["gdn_conv_w"], p["gdn_A_log"], p["gdn_dt_bias"],
      p["gdn_norm_w"], consts["ones"], _block_tril(G, L), consts["ea"], consts["eb"],
      consts["bd_b"], consts["bd_f"], consts["tri_levels"])
    return y[:, :t_len], s_new


def _attend(q, kwin, vwin, bias_ref, valid, write):
    q_b = (q * (HEAD ** -0.5)).astype(BF16)
    heads = range(H_C)
    sl = lambda x, h: x[:, h * HEAD:(h + 1) * HEAD]
    s = [_bdot_nt(sl(q_b, h), sl(kwin, h)) + bias_ref[h] for h in heads]
    if valid is not None:
        s = [jnp.where(valid, x, -jnp.inf) for x in s]
    e = [jnp.exp(x - jnp.max(x, axis=-1, keepdims=True)) for x in s]
    denom = [jnp.sum(x, axis=-1, keepdims=True) for x in e]
    pv = [_bdot(x.astype(BF16), sl(vwin, h)) for x, h in zip(e, heads)]
    for h in heads:
        write(h, pv[h] / denom[h])


def _band_prompt_kernel(q_ref, k_ref, v_ref, bias_ref, o_ref, kpad_scr, vpad_scr):
    ci = pl.program_id(1)
    lead = N_BAND_PREV * CHUNK
    band = lead + CHUNK

    @pl.when(ci == 0)
    def _():
        zeros = jnp.zeros((lead, W_C), BF16)
        kpad_scr[0:lead, :] = zeros
        vpad_scr[0:lead, :] = zeros
        kpad_scr[lead:, :] = k_ref[0].astype(BF16)
        vpad_scr[lead:, :] = v_ref[0].astype(BF16)

    start = pl.multiple_of(ci * CHUNK, CHUNK)
    kwin = kpad_scr[pl.ds(start, band), :]
    vwin = vpad_scr[pl.ds(start, band), :]
    col = lax.broadcasted_iota(jnp.int32, (CHUNK, band), 1)
    valid = col >= (N_BAND_PREV - ci) * CHUNK

    def write(h, val):
        o_ref[0, :, h * HEAD:(h + 1) * HEAD] = val

    _attend(q_ref[0], kwin, vwin, bias_ref, valid, write)


def _band_prompt(proj3, bias):
    bsz, t_len, _ = proj3.shape
    assert t_len % CHUNK == 0
    lead = N_BAND_PREV * CHUNK
    band = lead + CHUNK
    cq = OFF_ATT // W_C
    return pl.pallas_call(
        _band_prompt_kernel,
        grid=(bsz, t_len // CHUNK),
        in_specs=[pl.BlockSpec((1, CHUNK, W_C), lambda b, c: (b, c, cq)),
                  pl.BlockSpec((1, t_len, W_C), lambda b, c: (b, 0, cq + 1)),
                  pl.BlockSpec((1, t_len, W_C), lambda b, c: (b, 0, cq + 2)),
                  _const_spec((H_C, CHUNK, band))],
        out_specs=pl.BlockSpec((1, CHUNK, W_C), lambda b, c: (b, c, 0)),
        out_shape=jax.ShapeDtypeStruct((bsz, t_len, W_C), F32),
        scratch_shapes=[pltpu.VMEM((lead + t_len, W_C), BF16),
                        pltpu.VMEM((lead + t_len, W_C), BF16)],
        compiler_params=_params("parallel", "arbitrary"),
        name="band_prompt",
    )(proj3, proj3, proj3, bias)


def _band_step_kernel(q_ref, k_ref, v_ref, bias_ref, o_ref):
    def write(h, val):
        o_ref[0, :, h * HEAD:(h + 1) * HEAD] = val

    _attend(q_ref[0], k_ref[0].astype(BF16), v_ref[0].astype(BF16), bias_ref, None, write)


def _band_step(proj3, k_all, v_all, bias):
    bsz, t_len, _ = proj3.shape
    n_keys = k_all.shape[1]
    return pl.pallas_call(
        _band_step_kernel,
        grid=(bsz,),
        in_specs=[pl.BlockSpec((1, t_len, W_C), lambda b: (b, 0, OFF_ATT // W_C)),
                  pl.BlockSpec((1, n_keys, W_C), lambda b: (b, 0, 0)),
                  pl.BlockSpec((1, n_keys, W_C), lambda b: (b, 0, 0)),
                  _const_spec((H_C, t_len, n_keys))],
        out_specs=pl.BlockSpec((1, t_len, W_C), lambda b: (b, 0, 0)),
        out_shape=jax.ShapeDtypeStruct((bsz, t_len, W_C), F32),
        compiler_params=_params("parallel"),
        name="band_step",
    )(proj3, k_all, v_all, bias)


def _rel_bias_tile(rel_bias, n_q, n_k, lead):
    d = lead + np.arange(n_q)[:, None] - np.arange(n_k)[None, :]
    return rel_bias[:, np.clip(d, -MAX_REL, MAX_REL) + MAX_REL]


def _merge_kernel(x_ref, ya_ref, yb_ref, yc_ref, g0_ref, g1_ref, g2_ref, wa_ref, wb_ref, wc_ref,
                  wo_ref, gn_ref, o_ref):
    merged = (_sigmoid(g0_ref[...]) * _mm(ya_ref[...], wa_ref[...])
              + _sigmoid(g1_ref[...]) * _mm(yb_ref[...], wb_ref[...])
              + _sigmoid(g2_ref[...]) * _mm(yc_ref[...], wc_ref[...]))
    m2 = _mm(merged, wo_ref[...])
    o_ref[...] = x_ref[...] + m2 * lax.rsqrt(jnp.mean(m2 * m2, axis=-1, keepdims=True) + EPS) * gn_ref[...]


def _merge(x2d, ya, yb, yc, proj2, p):
    m = x2d.shape[0]
    tm = min(TM_MERGE, m)
    tok = lambda n, c=0: pl.BlockSpec((tm, n), lambda i, c=c: (i, c))
    return pl.pallas_call(
        _merge_kernel,
        grid=(m // tm,),
        in_specs=[tok(D_MODEL), tok(W_A), tok(W_B), tok(W_C),
                  tok(D_MODEL, 0), tok(D_MODEL, 1), tok(D_MODEL, 2),
                  _const_spec((W_A, D_MODEL)), _const_spec((W_B, D_MODEL)), _const_spec((W_C, D_MODEL)),
                  _const_spec((D_MODEL, D_MODEL)), _const_spec((1, D_MODEL))],
        out_specs=tok(D_MODEL),
        out_shape=jax.ShapeDtypeStruct((m, D_MODEL), F32),
        compiler_params=_params("parallel"),
        name="merge",
    )(x2d, ya, yb, yc, proj2, proj2, proj2, p["w_br_a"], p["w_br_b"], p["w_br_c"], p["w_out"],
      p["g_post_mix"])


def _ffn_kernel(x_ref, gpre_ref, up_ref, down_ref, gpost_ref, o_ref):
    x = x_ref[...]
    h = (x * lax.rsqrt(jnp.mean(x * x, axis=-1, keepdims=True) + EPS) * gpre_ref[...]).astype(BF16)
    f = jnp.zeros(x.shape, F32)
    for n in range(D_FF // FF_CHUNK):
        cols = slice(n * FF_CHUNK, (n + 1) * FF_CHUNK)
        act = jnp.maximum(_bdot(h, up_ref[:, cols]), 0.0)
        f = f + _bdot((act * act).astype(BF16), down_ref[cols, :])
    o_ref[...] = x + f * lax.rsqrt(jnp.mean(f * f, axis=-1, keepdims=True) + EPS) * gpost_ref[...]


def _ffn(x2d, p):
    m = x2d.shape[0]
    tm = min(TM_FFN, m)
    return pl.pallas_call(
        _ffn_kernel,
        grid=(m // tm,),
        in_specs=[pl.BlockSpec((tm, D_MODEL), lambda i: (i, 0)),
                  _const_spec((1, D_MODEL)), _const_spec((D_MODEL, D_FF)), _const_spec((D_FF, D_MODEL)),
                  _const_spec((1, D_MODEL))],
        out_specs=pl.BlockSpec((tm, D_MODEL), lambda i: (i, 0)),
        out_shape=jax.ShapeDtypeStruct((m, D_MODEL), F32),
        compiler_params=_params("parallel"),
        name="ffn",
    )(x2d, p["g_pre_ffn"], p["w_ff_up"], p["w_ff_down"], p["g_post_ffn"])


def _block_tril(groups, n):
    idx = np.arange(groups * n)
    same = (idx[:, None] // n) == (idx[None, :] // n)
    return jnp.asarray((same & (idx[:, None] >= idx[None, :])).astype(np.float32), BF16)


def _constants():
    lane_head = np.arange(LANE) // HEAD
    ones = (lane_head[:, None] == lane_head[None, :]).astype(np.float32)
    slot = np.arange(GROUP_W) // HEAD
    bd = (slot[:, None] == slot[None, :]).astype(np.float32)
    ea = np.zeros((AB_PAD, W_B), np.float32)
    eb = np.zeros((AB_PAD, W_B), np.float32)
    for h in range(H_B):
        ea[h, h * HEAD:(h + 1) * HEAD] = 1.0
        eb[H_B + h, h * HEAD:(h + 1) * HEAD] = 1.0
    i = np.arange(CHUNK)[:, None]
    j = (np.arange(GROUP_W) % HEAD)[None, :]
    same = lambda size: (i // size) == (j // size)
    levels = [same(2)] + [same(4 << l) & ~same(2 << l) for l in range(N_TRI_LEVELS - 1)]
    tri_levels = np.stack(levels).astype(np.float32)
    return {"tri_levels": jnp.asarray(tri_levels), "ones": jnp.asarray(ones, BF16), "ea": jnp.asarray(ea, BF16), "eb": jnp.asarray(eb, BF16),
            "bd_b": jnp.asarray(bd, BF16), "bd_f": jnp.asarray(bd, F32)}


def _stage_layer_params(l, g_pre_mix, g_post_mix, g_pre_ffn, g_post_ffn, w_in, rwkv_mu, rwkv_w0, rwkv_w2,
                        rwkv_a0, rwkv_a2, rwkv_g2, rwkv_kk, rwkv_ka, rwkv_rk, rwkv_ln_w, rwkv_ln_b,
                        gdn_conv_w, gdn_A_log, gdn_dt_bias, gdn_norm_w, att_rel_bias, w_br_a, w_br_b,
                        w_br_c, w_out, w_ff_up, w_ff_down):
    row = lambda a: a[l].reshape(1, -1).astype(F32)
    w = w_in[l]
    o_qkv = RWKV_PROJ
    o_a = o_qkv + GDN_CH
    o_z = o_a + 2 * H_B
    o_att = o_z + W_B
    o_gate = o_att + 3 * W_C
    ab_w = jnp.pad(w[:, o_a:o_z], ((0, 0), (0, AB_PAD - 2 * H_B)))
    w_perm = jnp.concatenate([w[:, o_gate:], w[:, o_att:o_gate], w[:, o_z:o_att], w[:, :RWKV_PROJ],
                              ab_w, w[:, o_qkv:o_a]], axis=1).astype(BF16)
    pad_ab = lambda a: jnp.pad(a[l].astype(F32), (0, AB_PAD - H_B)).reshape(1, AB_PAD)
    return {
        "g_pre_mix": row(g_pre_mix), "g_post_mix": row(g_post_mix),
        "g_pre_ffn": row(g_pre_ffn), "g_post_ffn": row(g_post_ffn),
        "w_in": w_perm,
        "rwkv_mu": row(rwkv_mu), "rwkv_w0": row(rwkv_w0), "rwkv_w2": rwkv_w2[l].astype(BF16),
        "rwkv_a0": row(rwkv_a0), "rwkv_a2": rwkv_a2[l].astype(BF16), "rwkv_g2": rwkv_g2[l].astype(BF16),
        "rwkv_kk": row(rwkv_kk), "rwkv_ka": row(rwkv_ka), "rwkv_rk": row(rwkv_rk),
        "rwkv_ln_w": row(rwkv_ln_w), "rwkv_ln_b": row(rwkv_ln_b),
        "gdn_conv_w": gdn_conv_w[l].astype(F32), "gdn_A_log": pad_ab(gdn_A_log),
        "gdn_dt_bias": pad_ab(gdn_dt_bias),
        "gdn_norm_w": jnp.tile(gdn_norm_w[l].astype(F32), H_B).reshape(1, W_B),
        "att_rel_bias": att_rel_bias[l].astype(F32),
        "w_br_a": w_br_a[l].astype(BF16), "w_br_b": w_br_b[l].astype(BF16), "w_br_c": w_br_c[l].astype(BF16),
        "w_out": w_out[l].astype(BF16), "w_ff_up": w_ff_up[l].astype(BF16),
        "w_ff_down": w_ff_down[l].astype(BF16),
    }


def _trunk_layer(x, p, consts, rwkv_shift, rwkv_s, gdn_conv, gdn_s, band_k, band_v):
    bsz, t_len, _ = x.shape
    m = bsz * t_len
    x2d = x.reshape(m, D_MODEL)
    proj2 = _in_proj(x2d, p["g_pre_mix"], p["w_in"])
    proj3 = proj2.reshape(bsz, t_len, N_PROJ)
    ya, s_a = _rwkv_mix(proj3, rwkv_shift.reshape(bsz, 1, RWKV_PROJ), rwkv_s, p, consts)
    yb, s_b = _gdn_mix(proj3, gdn_conv, gdn_s, p, consts)
    k_new = proj3[:, :, OFF_ATT + W_C:OFF_ATT + 2 * W_C]
    v_new = proj3[:, :, OFF_ATT + 2 * W_C:OFF_ATT + 3 * W_C]
    if band_k is None:
        lead = N_BAND_PREV * CHUNK
        yc = _band_prompt(proj3, _rel_bias_tile(p["att_rel_bias"], CHUNK, lead + CHUNK, lead))
        keep = min(lead, t_len)
        k_out, v_out = k_new[:, t_len - keep:], v_new[:, t_len - keep:]
    else:
        n_past = band_k.shape[1]
        k_all = jnp.concatenate([band_k.reshape(bsz, n_past, W_C), k_new], axis=1)
        v_all = jnp.concatenate([band_v.reshape(bsz, n_past, W_C), v_new], axis=1)
        yc = _band_step(proj3, k_all, v_all, _rel_bias_tile(p["att_rel_bias"], t_len, n_past + t_len, n_past))
        k_out, v_out = k_new, v_new
    x1 = _merge(x2d, ya.reshape(m, W_A), yb.reshape(m, W_B), yc.reshape(m, W_C), proj2, p)
    x2 = _ffn(x1, p)
    shift_new = proj3[:, t_len - 1, OFF_RWKV:OFF_RWKV + RWKV_PROJ]
    conv_new = proj3[:, t_len - (GDN_CONV - 1):, OFF_QKV:OFF_QKV + GDN_CH]
    new_state = (shift_new, s_a, conv_new, s_b,
                 k_out.reshape(bsz, -1, H_C, HEAD), v_out.reshape(bsz, -1, H_C, HEAD))
    return x2.reshape(bsz, t_len, D_MODEL), new_state


def kernel(x_prompt, x_sample, state_rwkv_shift, state_rwkv_wkv, state_gdn_conv, state_gdn_S, cache_band_k, cache_band_v, g_pre_mix, g_post_mix, g_pre_ffn, g_post_ffn, w_in, rwkv_mu, rwkv_w0, rwkv_w2, rwkv_a0, rwkv_a2, rwkv_g2, rwkv_kk, rwkv_ka, rwkv_rk, rwkv_ln_w, rwkv_ln_b, gdn_conv_w, gdn_A_log, gdn_dt_bias, gdn_norm_w, att_rel_bias, w_br_a, w_br_b, w_br_c, w_out, w_ff_up, w_ff_down):
    depth = w_in.shape[0]
    bp = x_prompt.shape[0]
    consts = _constants()
    xp, xs = x_prompt, x_sample
    p_new = [[] for _ in range(6)]
    s_new = [[] for _ in range(6)]
    for l in range(depth):
        p = _stage_layer_params(l, g_pre_mix, g_post_mix, g_pre_ffn, g_post_ffn, w_in, rwkv_mu, rwkv_w0,
                                rwkv_w2, rwkv_a0, rwkv_a2, rwkv_g2, rwkv_kk, rwkv_ka, rwkv_rk, rwkv_ln_w,
                                rwkv_ln_b, gdn_conv_w, gdn_A_log, gdn_dt_bias, gdn_norm_w, att_rel_bias,
                                w_br_a, w_br_b, w_br_c, w_out, w_ff_up, w_ff_down)
        xp, st_p = _trunk_layer(
            xp, p, consts,
            jnp.zeros((bp, RWKV_PROJ), F32), jnp.zeros((bp, H_A, HEAD, HEAD), F32),
            jnp.zeros((bp, GDN_CONV - 1, GDN_CH), F32), jnp.zeros((bp, H_B, HEAD, HEAD), F32),
            None, None)
        xs, st_s = _trunk_layer(
            xs, p, consts, state_rwkv_shift[l], state_rwkv_wkv[l], state_gdn_conv[l], state_gdn_S[l],
            cache_band_k[l], cache_band_v[l])
        for i in range(6):
            p_new[i].append(st_p[i])
            s_new[i].append(st_s[i])
    stk_p = [jnp.stack(a).astype(x_prompt.dtype) for a in p_new]
    stk_s = [jnp.stack(a).astype(x_sample.dtype) for a in s_new]
    return (xp, xs, *stk_p, *stk_s)
```

```python
import functools
import math

import numpy as np
import jax
import jax.numpy as jnp
from jax import lax
from jax.experimental import pallas as pl
from jax.experimental.pallas import tpu as pltpu

F32 = jnp.float32
BF16 = jnp.bfloat16

D_MODEL = 1024
HEAD = 64
H_A = 6
W_A = H_A * HEAD
DECAY_RANK = 64
ICLR_RANK = 64
GATE_RANK = 128
RWKV_PROJ = 3 * W_A + DECAY_RANK + ICLR_RANK + GATE_RANK
H_B = 6
W_B = H_B * HEAD
GDN_CONV = 4
GDN_CH = 3 * W_B
H_C = 4
W_C = H_C * HEAD
N_BAND_PREV = 8
MAX_REL = 128
CHUNK = 64
D_FF = 4 * D_MODEL
EPS = 1e-6
GN_EPS = 64e-5

LANE = 128
SUBLANE = 8
AB_PAD = LANE
OFF_GATE = 0
OFF_ATT = OFF_GATE + 3 * D_MODEL
OFF_Z = OFF_ATT + 3 * W_C
OFF_RWKV = OFF_Z + W_B
OFF_AB = OFF_RWKV + RWKV_PROJ
OFF_QKV = OFF_AB + AB_PAD
N_PROJ = OFF_QKV + GDN_CH
assert OFF_ATT % W_C == 0 and OFF_Z % W_B == 0 and OFF_RWKV % RWKV_PROJ == 0
assert OFF_AB % AB_PAD == 0 and OFF_QKV % GDN_CH == 0 and N_PROJ % LANE == 0

VMEM_LIMIT = 56 * 1024 * 1024
TM_IN = 256
TN_IN = 768
TM_MERGE = 512
TM_FFN = 512
FF_CHUNK = 1024
MIX_GROUP = 4
BAND_Q_CHUNKS = 4


def _params(*sem):
    return pltpu.CompilerParams(dimension_semantics=sem, vmem_limit_bytes=VMEM_LIMIT)


def _const_spec(shape):
    nd = len(shape)
    return pl.BlockSpec(shape, lambda *_: (0,) * nd)


def _bdot(a, b):
    return jnp.dot(a, b, preferred_element_type=F32)


def _bdot_nt(a, b):
    return lax.dot_general(a, b, (((1,), (1,)), ((), ())), preferred_element_type=F32)


def _bdot_tn(a, b):
    return lax.dot_general(a, b, (((0,), (0,)), ((), ())), preferred_element_type=F32)


def _mm(a, b):
    return _bdot(a.astype(BF16), b.astype(BF16))


def _split(a):
    hi = a.astype(BF16)
    lo = (a - hi.astype(F32)).astype(BF16)
    return hi, lo


def _mm_lsplit(a, b_exact):
    hi, lo = _split(a)
    return _bdot(hi, b_exact) + _bdot(lo, b_exact)


def _mm_rsplit(a_exact, b):
    hi, lo = _split(b)
    return _bdot(a_exact, hi) + _bdot(a_exact, lo)


def _sigmoid(x):
    return 1.0 / (1.0 + jnp.exp(-x))


def _softplus(x):
    return jnp.maximum(x, 0.0) + jnp.log(1.0 + jnp.exp(-jnp.abs(x)))


def _rows_from(ref_rows, n):
    return jnp.concatenate([jnp.broadcast_to(r, (n, r.shape[-1])) for r in ref_rows], axis=0)


def _in_proj_kernel(x_ref, g_ref, w_ref, o_ref):
    x = x_ref[...]
    h = (x * lax.rsqrt(jnp.mean(x * x, axis=-1, keepdims=True) + EPS) * g_ref[...]).astype(BF16)
    for n in range(N_PROJ // TN_IN):
        cols = slice(n * TN_IN, (n + 1) * TN_IN)
        o_ref[:, cols] = _bdot(h, w_ref[:, cols])


def _in_proj(x2d, g, w_perm):
    m = x2d.shape[0]
    tm = min(TM_IN, m)
    return pl.pallas_call(
        _in_proj_kernel,
        grid=(m // tm,),
        in_specs=[pl.BlockSpec((tm, D_MODEL), lambda i: (i, 0)),
                  _const_spec((1, D_MODEL)),
                  _const_spec((D_MODEL, N_PROJ))],
        out_specs=pl.BlockSpec((tm, N_PROJ), lambda i: (i, 0)),
        out_shape=jax.ShapeDtypeStruct((m, N_PROJ), F32),
        compiler_params=_params("parallel"),
        name="in_proj",
    )(x2d, g, w_perm)


PACK = 4
GROUP_W = PACK * HEAD
GROUPS_PER_PAIR = 2 * H_A // PACK
N_TRI_LEVELS = int(math.log2(HEAD))
assert H_A == H_B and 2 * H_A % PACK == 0 and GROUP_W % LANE == 0 and CHUNK == HEAD


def _group_slots(pair_group):
    if pair_group < 2:
        return [(pair_group, h) for h in range(PACK)]
    return [(0, PACK), (0, PACK + 1), (1, PACK), (1, PACK + 1)]


def _to_groups(x, L):
    out = []
    for a in range(0, x.shape[0] // L, 2):
        ra, rb = slice(a * L, (a + 1) * L), slice((a + 1) * L, (a + 2) * L)
        out += [x[ra, 0:GROUP_W], x[rb, 0:GROUP_W],
                jnp.concatenate([x[ra, GROUP_W:W_A], x[rb, GROUP_W:W_A]], axis=1)]
    return out


def _from_groups(vals):
    rows = []
    half = (W_A - GROUP_W)
    for p in range(len(vals) // GROUPS_PER_PAIR):
        ga, gb, gc = vals[3 * p:3 * p + 3]
        rows.append(jnp.concatenate([ga, gc[:, 0:half]], axis=1))
        rows.append(jnp.concatenate([gb, gc[:, half:2 * half]], axis=1))
    return jnp.concatenate(rows, axis=0)


def _block_diag(x_b, mask_b):
    return jnp.concatenate([x_b] * PACK, axis=0) * mask_b


def _packed_tri_masks(L):
    r = lax.broadcasted_iota(jnp.int32, (L, GROUP_W), 0)
    c = jnp.bitwise_and(lax.broadcasted_iota(jnp.int32, (L, GROUP_W), 1), HEAD - 1)
    return c < r, c <= r, c == r


def _tri_inv_minus_eye_packed(a_list, mask_b, lvl_ref):
    tp = [-(a * lvl_ref[0]) for a in a_list]
    for lvl in range(1, N_TRI_LEVELS):
        m = lvl_ref[lvl]
        off = [a * m for a in a_list]
        off_bd = [_block_diag(x.astype(BF16), mask_b) for x in off]
        tp_b = [t.astype(BF16) for t in tp]
        m1 = [x + _bdot(tb, bd) for x, tb, bd in zip(off, tp_b, off_bd)]
        tp_bd = [_block_diag(tb, mask_b) for tb in tp_b]
        tp = [t - x - _bdot(x.astype(BF16), bd) for t, x, bd in zip(tp, m1, tp_bd)]
    return tp


def _seg_sum(x, ones_b, exact):
    outs = []
    for j in range(W_A // LANE):
        xs = x[:, j * LANE:(j + 1) * LANE]
        outs.append(_mm_lsplit(xs, ones_b) if exact else _bdot(xs.astype(BF16), ones_b))
    return jnp.concatenate(outs, axis=1)


def _load_state(s_scr, s0_ref):
    s_scr[...] = jnp.zeros(s_scr.shape, F32)
    for gi in range(s_scr.shape[0]):
        for j, (q, h) in enumerate(_group_slots(gi % GROUPS_PER_PAIR)):
            seq = 2 * (gi // GROUPS_PER_PAIR) + q
            s_scr[gi, j * HEAD:(j + 1) * HEAD, j * HEAD:(j + 1) * HEAD] = s0_ref[seq, h]


def _store_state(sout_ref, s_scr):
    for gi in range(s_scr.shape[0]):
        for j, (q, h) in enumerate(_group_slots(gi % GROUPS_PER_PAIR)):
            seq = 2 * (gi // GROUPS_PER_PAIR) + q
            sout_ref[seq, h] = s_scr[gi, j * HEAD:(j + 1) * HEAD, j * HEAD:(j + 1) * HEAD]


def _rwkv_kernel(c_ref, shift0_ref, s0_ref, mu_ref, w0_ref, w2_ref, a0_ref, a2_ref, g2_ref,
                 kk_ref, ka_ref, rk_ref, lnw_ref, lnb_ref, ones_ref, ltri_ref, bdb_ref, bdf_ref, lvl_ref,
                 y_ref, sout_ref, s_scr, prev_scr, *, n_valid):
    G, L = c_ref.shape[0], c_ref.shape[1]
    t = pl.program_id(1)

    @pl.when(t == 0)
    def _():
        _load_state(s_scr, s0_ref)
        prev_scr[...] = shift0_ref[...]

    c = c_ref[...].reshape(G * L, RWKV_PROJ)
    row = jnp.bitwise_and(lax.broadcasted_iota(jnp.int32, c.shape, 0), L - 1)
    prev_rows = _rows_from([prev_scr[g] for g in range(G)], L)
    c_prev = jnp.where(row == 0, prev_rows, pltpu.roll(c, 1, 0))
    for g in range(G):
        prev_scr[g] = c[(g + 1) * L - 1:(g + 1) * L, :]
    cm = c + (c_prev - c) * mu_ref[...]
    r = cm[:, 0:W_A]
    k = cm[:, W_A:2 * W_A]
    v = cm[:, 2 * W_A:3 * W_A]
    o = 3 * W_A
    wd = cm[:, o:o + DECAY_RANK]
    ad = cm[:, o + DECAY_RANK:o + DECAY_RANK + ICLR_RANK]
    gd = cm[:, o + DECAY_RANK + ICLR_RANK:]

    ones_b = ones_ref[...]
    w_log = -_softplus(-(w0_ref[...] + _mm(jnp.tanh(wd), w2_ref[...]))) - 0.5
    logw = -jnp.exp(w_log)
    a = _sigmoid(a0_ref[...] + _mm(ad, a2_ref[...]))
    gate = _mm(_sigmoid(gd), g2_ref[...])
    kks = k * kk_ref[...]
    kkn = kks * lax.rsqrt(_seg_sum(kks * kks, ones_b, False) + 1e-6)
    k2 = k * (1.0 + (a - 1.0) * ka_ref[...])
    if n_valid < L:
        live = jnp.bitwise_and(lax.broadcasted_iota(jnp.int32, k.shape, 0), L - 1) < n_valid
        logw = jnp.where(live, logw, 0.0)
        kkn = jnp.where(live, kkn, 0.0)
        k2 = jnp.where(live, k2, 0.0)
    b = kkn * a

    cum = _mm_rsplit(ltri_ref[...], logw)
    w_inc = jnp.exp(cum)
    w_inv = jnp.exp(-cum)
    w_last = _rows_from([w_inc[(g + 1) * L - 1:(g + 1) * L, :] for g in range(G)], L)
    kt = k2 * w_inv
    bt = b * w_inv
    groups = lambda x: _to_groups(x, L)
    rt_g = groups((r * w_inc).astype(BF16))
    at_g = groups((kkn * jnp.exp(cum - logw)).astype(BF16))
    kt_g = groups(kt.astype(BF16))
    bt_g = groups(bt.astype(BF16))
    kl_g = groups((kt * w_last).astype(BF16))
    bl_neg_g = groups((-(bt * w_last)).astype(BF16))
    v_g = groups(v.astype(BF16))
    wl_g = groups(w_last)

    mask_b = bdb_ref[...]
    mask_f = bdf_ref[...]
    strict, incl, _ = _packed_tri_masks(L)
    n_groups = len(v_g)
    s_old = [s_scr[i] for i in range(n_groups)]
    s_old_b = [s.astype(BF16) for s in s_old]
    lhs_ar = [jnp.concatenate([x, y], axis=0) for x, y in zip(at_g, rt_g)]
    bt_bd = [_block_diag(x, mask_b) for x in bt_g]
    kt_bd = [_block_diag(x, mask_b) for x in kt_g]
    v_bd = [_block_diag(x, mask_b) for x in v_g]
    g_b = [_bdot_nt(x, bd) for x, bd in zip(lhs_ar, bt_bd)]
    g_k = [_bdot_nt(x, bd) for x, bd in zip(lhs_ar, kt_bd)]
    a_ab = [jnp.where(strict, x[:L], 0.0) for x in g_b]
    r_b_b = [jnp.where(incl, x[L:], 0.0).astype(BF16) for x in g_b]
    lhs_k = [jnp.concatenate([jnp.where(strict, x[:L], 0.0), jnp.where(incl, x[L:], 0.0)], axis=0).astype(BF16)
             for x in g_k]
    sv = [_bdot_nt(x, s) + _bdot(lk, vbd) for x, s, lk, vbd in zip(lhs_ar, s_old_b, lhs_k, v_bd)]
    tp = _tri_inv_minus_eye_packed(a_ab, mask_b, lvl_ref)
    u = [x[:L] + _bdot(t_.astype(BF16), _block_diag(x[:L].astype(BF16), mask_b))
         for x, t_ in zip(sv, tp)]
    u_b = [x.astype(BF16) for x in u]
    y_g = [x[L:] - _bdot(rb, _block_diag(ub, mask_b)) for x, rb, ub in zip(sv, r_b_b, u_b)]
    for i in range(n_groups):
        upd = _bdot_tn(jnp.concatenate([v_g[i], u_b[i]], axis=0),
                       jnp.concatenate([kl_g[i], bl_neg_g[i]], axis=0))
        s_scr[i] = s_old[i] * wl_g[i][0:1, :] + upd * mask_f

    y = _from_groups(y_g)
    mean = _seg_sum(y, ones_b, True) * (1.0 / HEAD)
    d = y - mean
    var = _seg_sum(d * d, ones_b, False) * (1.0 / HEAD)
    yn = d * lax.rsqrt(var + GN_EPS) * lnw_ref[...] + lnb_ref[...]
    bonus = _seg_sum(r * k2 * rk_ref[...], ones_b, True) * v
    y_ref[...] = ((yn + bonus) * gate).reshape(G, L, W_A)

    @pl.when(t == pl.num_programs(1) - 1)
    def _():
        _store_state(sout_ref, s_scr)


def _mixer_rows(proj3):
    t_len = proj3.shape[1]
    if t_len >= CHUNK:
        assert t_len % CHUNK == 0
        return proj3, CHUNK
    return jnp.pad(proj3, ((0, 0), (0, CHUNK - t_len), (0, 0))), t_len


def _rwkv_mix(proj3, shift0, s0, p, consts):
    bsz, t_len, _ = proj3.shape
    proj3, n_valid = _mixer_rows(proj3)
    t_pad = proj3.shape[1]
    L, G = CHUNK, MIX_GROUP
    assert bsz % G == 0 and G % 2 == 0
    row = lambda n: _const_spec((1, n))
    y, s_new = pl.pallas_call(
        functools.partial(_rwkv_kernel, n_valid=n_valid),
        grid=(bsz // G, t_pad // L),
        in_specs=[pl.BlockSpec((G, L, RWKV_PROJ), lambda b, t: (b, t, OFF_RWKV // RWKV_PROJ)),
                  pl.BlockSpec((G, 1, RWKV_PROJ), lambda b, t: (b, 0, 0)),
                  pl.BlockSpec((G, H_A, HEAD, HEAD), lambda b, t: (b, 0, 0, 0)),
                  row(RWKV_PROJ), row(W_A), _const_spec((DECAY_RANK, W_A)), row(W_A),
                  _const_spec((ICLR_RANK, W_A)), _const_spec((GATE_RANK, W_A)),
                  row(W_A), row(W_A), row(W_A), row(W_A), row(W_A),
                  _const_spec((LANE, LANE)), _const_spec((G * L, G * L)),
                  _const_spec((GROUP_W, GROUP_W)), _const_spec((GROUP_W, GROUP_W)),
                  _const_spec((N_TRI_LEVELS, CHUNK, GROUP_W))],
        out_specs=[pl.BlockSpec((G, L, W_A), lambda b, t: (b, t, 0)),
                   pl.BlockSpec((G, H_A, HEAD, HEAD), lambda b, t: (b, 0, 0, 0))],
        out_shape=[jax.ShapeDtypeStruct((bsz, t_pad, W_A), F32),
                   jax.ShapeDtypeStruct((bsz, H_A, HEAD, HEAD), F32)],
        scratch_shapes=[pltpu.VMEM((G // 2 * GROUPS_PER_PAIR, GROUP_W, GROUP_W), F32),
                        pltpu.VMEM((G, 1, RWKV_PROJ), F32)],
        compiler_params=_params("parallel", "arbitrary"),
        name="rwkv_mix",
    )(proj3, shift0, s0, p["rwkv_mu"], p["rwkv_w0"], p["rwkv_w2"], p["rwkv_a0"], p["rwkv_a2"],
      p["rwkv_g2"], p["rwkv_kk"], p["rwkv_ka"], p["rwkv_rk"], p["rwkv_ln_w"], p["rwkv_ln_b"],
      consts["ones"], _block_tril(G, L), consts["bd_b"], consts["bd_f"], consts["tri_levels"])
    return y[:, :t_len], s_new


def _gdn_kernel(qkv_ref, ab_ref, z_ref, conv0_ref, s0_ref, cw_ref, alog_ref, dtb_ref, nw_ref,
                ones_ref, ltri_ref, ea_ref, eb_ref, bdb_ref, bdf_ref, lvl_ref,
                y_ref, sout_ref, s_scr, xpad_scr, *, n_valid):
    G, L = qkv_ref.shape[0], qkv_ref.shape[1]
    t = pl.program_id(1)
    npre = GDN_CONV - 1
    base_row = SUBLANE

    @pl.when(t == 0)
    def _():
        _load_state(s_scr, s0_ref)
        xpad_scr[:, base_row - npre:base_row, :] = conv0_ref[...]

    convs = []
    for g in range(G):
        x = qkv_ref[g]
        xpad_scr[g, base_row:base_row + L, :] = x
        conv = x * cw_ref[npre:npre + 1, :]
        for j in range(npre):
            lo = base_row - npre + j
            conv = conv + xpad_scr[g, lo:lo + L, :] * cw_ref[j:j + 1, :]
        xpad_scr[g, base_row - npre:base_row, :] = xpad_scr[g, base_row + L - npre:base_row + L, :]
        convs.append(conv)
    conv = jnp.concatenate(convs, axis=0)
    qkv = conv * _sigmoid(conv)

    ones_b = ones_ref[...]
    q = qkv[:, 0:W_B]
    k = qkv[:, W_B:2 * W_B]
    v = qkv[:, 2 * W_B:]
    q = q * lax.rsqrt(_seg_sum(q * q, ones_b, False) + 1e-6) * (HEAD ** -0.5)
    k = k * lax.rsqrt(_seg_sum(k * k, ones_b, False) + 1e-6)

    ab = ab_ref[...].reshape(G * L, AB_PAD)
    g_row = -jnp.exp(alog_ref[...]) * _softplus(ab + dtb_ref[...])
    g_exp = _mm_lsplit(g_row, ea_ref[...])
    beta = _mm_lsplit(_sigmoid(ab), eb_ref[...])
    if n_valid < L:
        live = jnp.bitwise_and(lax.broadcasted_iota(jnp.int32, k.shape, 0), L - 1) < n_valid
        k = jnp.where(live, k, 0.0)
        beta = jnp.where(live, beta, 0.0)
        g_exp = jnp.where(live, g_exp, 0.0)
    gc = _mm_rsplit(ltri_ref[...], g_exp)
    g_last = _rows_from([gc[(g + 1) * L - 1:(g + 1) * L, :] for g in range(G)], L)
    e_gc = jnp.exp(gc)
    kbeta = k * beta
    vbeta = v * beta
    kw = kbeta * e_gc
    groups = lambda x: _to_groups(x, L)
    k_g = groups(k.astype(BF16))
    q_g = groups(q.astype(BF16))
    kbeta_g = groups(kbeta.astype(BF16))
    vbeta_g = groups(vbeta)
    kw_g = groups(kw)
    qe_g = groups((q * e_gc).astype(BF16))
    kd_g = groups((k * jnp.exp(g_last - gc)).astype(BF16))
    gc_g = groups(gc)
    el_g = groups(jnp.exp(g_last))

    mask_b = bdb_ref[...]
    mask_f = bdf_ref[...]
    strict, incl, eye = _packed_tri_masks(L)
    n_groups = len(k_g)
    s_old = [s_scr[i] for i in range(n_groups)]
    s_old_b = [s.astype(BF16) for s in s_old]
    gamma = []
    for x in gc_g:
        g_row_j = jnp.sum(jnp.where(eye, x, 0.0), axis=0, keepdims=True)
        gamma.append(jnp.where(incl, jnp.exp(jnp.minimum(x - g_row_j, 0.0)), 0.0))
    kk = [_bdot_nt(jnp.concatenate([x, y], axis=0), _block_diag(kb, mask_b))
          for x, y, kb in zip(kbeta_g, q_g, k_g)]
    lower = [jnp.where(strict, x[:L] * gm, 0.0) for x, gm in zip(kk, gamma)]
    att_b = [(x[L:] * gm).astype(BF16) for x, gm in zip(kk, gamma)]
    tp_b = [x.astype(BF16) for x in _tri_inv_minus_eye_packed(lower, mask_b, lvl_ref)]
    u_pre = [x + _bdot(t_, _block_diag(x.astype(BF16), mask_b)) for x, t_ in zip(vbeta_g, tp_b)]
    w_b = [(x + _bdot(t_, _block_diag(x.astype(BF16), mask_b))).astype(BF16) for x, t_ in zip(kw_g, tp_b)]
    ws = [_bdot(jnp.concatenate([w_, qe], axis=0), s) for w_, qe, s in zip(w_b, qe_g, s_old_b)]
    u_b = [(up - x[:L]).astype(BF16) for up, x in zip(u_pre, ws)]
    o_g = [x[L:] + _bdot(a_, _block_diag(ub, mask_b)) for x, a_, ub in zip(ws, att_b, u_b)]
    for i in range(n_groups):
        s_scr[i] = s_old[i] * el_g[i][0:1, :] + _bdot_tn(kd_g[i], u_b[i]) * mask_f

    o = _from_groups(o_g)
    on = o * lax.rsqrt(_seg_sum(o * o, ones_b, False) * (1.0 / HEAD) + EPS) * nw_ref[...]
    z = z_ref[...].reshape(G * L, W_B)
    y_ref[...] = (on * (z * _sigmoid(z))).reshape(G, L, W_B)

    @pl.when(t == pl.num_programs(1) - 1)
    def _():
        _store_state(sout_ref, s_scr)


def _gdn_mix(proj3, conv0, s0, p, consts):
    bsz, t_len, _ = proj3.shape
    assert t_len >= GDN_CONV - 1
    proj3, n_valid = _mixer_rows(proj3)
    t_pad = proj3.shape[1]
    L, G = CHUNK, MIX_GROUP
    assert bsz % G == 0 and G % 2 == 0
    y, s_new = pl.pallas_call(
        functools.partial(_gdn_kernel, n_valid=n_valid),
        grid=(bsz // G, t_pad // L),
        in_specs=[pl.BlockSpec((G, L, GDN_CH), lambda b, t: (b, t, OFF_QKV // GDN_CH)),
                  pl.BlockSpec((G, L, AB_PAD), lambda b, t: (b, t, OFF_AB // AB_PAD)),
                  pl.BlockSpec((G, L, W_B), lambda b, t: (b, t, OFF_Z // W_B)),
                  pl.BlockSpec((G, GDN_CONV - 1, GDN_CH), lambda b, t: (b, 0, 0)),
                  pl.BlockSpec((G, H_B, HEAD, HEAD), lambda b, t: (b, 0, 0, 0)),
                  _const_spec((GDN_CONV, GDN_CH)), _const_spec((1, AB_PAD)), _const_spec((1, AB_PAD)),
                  _const_spec((1, W_B)), _const_spec((LANE, LANE)), _const_spec((G * L, G * L)),
                  _const_spec((AB_PAD, W_B)), _const_spec((AB_PAD, W_B)),
                  _const_spec((GROUP_W, GROUP_W)), _const_spec((GROUP_W, GROUP_W)),
                  _const_spec((N_TRI_LEVELS, CHUNK, GROUP_W))],
        out_specs=[pl.BlockSpec((G, L, W_B), lambda b, t: (b, t, 0)),
                   pl.BlockSpec((G, H_B, HEAD, HEAD), lambda b, t: (b, 0, 0, 0))],
        out_shape=[jax.ShapeDtypeStruct((bsz, t_pad, W_B), F32),
                   jax.ShapeDtypeStruct((bsz, H_B, HEAD, HEAD), F32)],
        scratch_shapes=[pltpu.VMEM((G // 2 * GROUPS_PER_PAIR, GROUP_W, GROUP_W), F32),
                        pltpu.VMEM((G, SUBLANE + L, GDN_CH), F32)],
        compiler_params=_params("parallel", "arbitrary"),
        name="gdn_mix",
    )(proj3, proj3, proj3, conv0, s0, p["gdn_conv_w"], p["gdn_A_log"], p["gdn_dt_bias"],
      p["gdn_norm_w"], consts["ones"], _block_tril(G, L), consts["ea"], consts["eb"],
      consts["bd_b"], consts["bd_f"], consts["tri_levels"])
    return y[:, :t_len], s_new


def _attend(q, kwin, vwin, bias_ref, min_col, o_ref):
    q_b = (q * (HEAD ** -0.5)).astype(BF16)
    heads = range(H_C)
    sl = lambda x, h: x[:, h * HEAD:(h + 1) * HEAD]
    s = [_bdot_nt(sl(q_b, h), sl(kwin, h)) + bias_ref[h] for h in heads]
    if min_col is not None:
        col = lax.broadcasted_iota(jnp.int32, s[0].shape, 1)
        s = [jnp.where(col >= min_col, x, -jnp.inf) for x in s]
    e = [jnp.exp(x - jnp.max(x, axis=-1, keepdims=True)) for x in s]
    denom = [jnp.sum(x, axis=-1, keepdims=True) for x in e]
    pv = [_bdot(x.astype(BF16), sl(vwin, h)) for x, h in zip(e, heads)]
    for h in heads:
        o_ref[0, :, h * HEAD:(h + 1) * HEAD] = pv[h] / denom[h]


def _band_prompt_kernel(q_ref, k_ref, v_ref, bias_ref, o_ref, kpad_scr, vpad_scr):
    step = pl.program_id(1)
    n_q = q_ref.shape[1]
    lead = N_BAND_PREV * CHUNK

    @pl.when(step == 0)
    def _():
        zeros = jnp.zeros((lead, W_C), BF16)
        kpad_scr[0:lead, :] = zeros
        vpad_scr[0:lead, :] = zeros
        kpad_scr[lead:, :] = k_ref[0].astype(BF16)
        vpad_scr[lead:, :] = v_ref[0].astype(BF16)

    start = pl.multiple_of(step * n_q, CHUNK)
    kwin = kpad_scr[pl.ds(start, lead + n_q), :]
    vwin = vpad_scr[pl.ds(start, lead + n_q), :]
    _attend(q_ref[0], kwin, vwin, bias_ref, lead - step * n_q, o_ref)


def _band_prompt(proj3, rel_bias):
    bsz, t_len, _ = proj3.shape
    lead = N_BAND_PREV * CHUNK
    n_q = min(BAND_Q_CHUNKS * CHUNK, t_len)
    assert t_len % n_q == 0 and n_q % CHUNK == 0
    n_k = lead + n_q
    first = (np.arange(n_q)[:, None] // CHUNK) * CHUNK
    cols = np.arange(n_k)[None, :]
    in_band = (cols >= first) & (cols < first + lead + CHUNK)
    bias = jnp.where(in_band[None], _rel_bias_tile(rel_bias, n_q, n_k, lead), -jnp.inf)
    cq = OFF_ATT // W_C
    return pl.pallas_call(
        _band_prompt_kernel,
        grid=(bsz, t_len // n_q),
        in_specs=[pl.BlockSpec((1, n_q, W_C), lambda b, c: (b, c, cq)),
                  pl.BlockSpec((1, t_len, W_C), lambda b, c: (b, 0, cq + 1)),
                  pl.BlockSpec((1, t_len, W_C), lambda b, c: (b, 0, cq + 2)),
                  _const_spec((H_C, n_q, n_k))],
        out_specs=pl.BlockSpec((1, n_q, W_C), lambda b, c: (b, c, 0)),
        out_shape=jax.ShapeDtypeStruct((bsz, t_len, W_C), F32),
        scratch_shapes=[pltpu.VMEM((lead + t_len, W_C), BF16),
                        pltpu.VMEM((lead + t_len, W_C), BF16)],
        compiler_params=_params("parallel", "arbitrary"),
        name="band_prompt",
    )(proj3, proj3, proj3, bias)


def _band_step_kernel(q_ref, k_ref, v_ref, bias_ref, o_ref):
    _attend(q_ref[0], k_ref[0].astype(BF16), v_ref[0].astype(BF16), bias_ref, None, o_ref)


def _band_step(proj3, k_all, v_all, bias):
    bsz, t_len, _ = proj3.shape
    n_keys = k_all.shape[1]
    return pl.pallas_call(
        _band_step_kernel,
        grid=(bsz,),
        in_specs=[pl.BlockSpec((1, t_len, W_C), lambda b: (b, 0, OFF_ATT // W_C)),
                  pl.BlockSpec((1, n_keys, W_C), lambda b: (b, 0, 0)),
                  pl.BlockSpec((1, n_keys, W_C), lambda b: (b, 0, 0)),
                  _const_spec((H_C, t_len, n_keys))],
        out_specs=pl.BlockSpec((1, t_len, W_C), lambda b: (b, 0, 0)),
        out_shape=jax.ShapeDtypeStruct((bsz, t_len, W_C), F32),
        compiler_params=_params("parallel"),
        name="band_step",
    )(proj3, k_all, v_all, bias)


def _rel_bias_tile(rel_bias, n_q, n_k, lead):
    n_d = n_q + n_k - 1
    d = np.clip(lead + n_q - 1 - np.arange(n_d), -MAX_REL, MAX_REL) + MAX_REL
    n_hi = int(np.sum(d == 2 * MAX_REL)) - 1 if d[0] == 2 * MAX_REL else 0
    n_lo = int(np.sum(d == 0)) - 1 if d[-1] == 0 else 0
    core = rel_bias[:, int(d[-1 - n_lo]):int(d[n_hi]) + 1][:, ::-1]
    rev = jnp.concatenate([jnp.repeat(rel_bias[:, -1:], n_hi, axis=1), core,
                           jnp.repeat(rel_bias[:, :1], n_lo, axis=1)], axis=1)
    heads = rel_bias.shape[0]
    flat = jnp.broadcast_to(rev[:, None, :], (heads, n_q, n_d)).reshape(heads, n_q * n_d)
    skew = flat[:, n_q - 1:n_q - 1 + n_q * (n_d - 1)].reshape(heads, n_q, n_d - 1)
    return skew[:, :, :n_k]


def _merge_kernel(x_ref, ya_ref, yb_ref, yc_ref, g0_ref, g1_ref, g2_ref, wa_ref, wb_ref, wc_ref,
                  wo_ref, gn_ref, o_ref):
    merged = (_sigmoid(g0_ref[...]) * _mm(ya_ref[...], wa_ref[...])
              + _sigmoid(g1_ref[...]) * _mm(yb_ref[...], wb_ref[...])
              + _sigmoid(g2_ref[...]) * _mm(yc_ref[...], wc_ref[...]))
    m2 = _mm(merged, wo_ref[...])
    o_ref[...] = x_ref[...] + m2 * lax.rsqrt(jnp.mean(m2 * m2, axis=-1, keepdims=True) + EPS) * gn_ref[...]


def _merge(x2d, ya, yb, yc, proj2, p):
    m = x2d.shape[0]
    tm = min(TM_MERGE, m)
    tok = lambda n, c=0: pl.BlockSpec((tm, n), lambda i, c=c: (i, c))
    return pl.pallas_call(
        _merge_kernel,
        grid=(m // tm,),
        in_specs=[tok(D_MODEL), tok(W_A), tok(W_B), tok(W_C),
                  tok(D_MODEL, 0), tok(D_MODEL, 1), tok(D_MODEL, 2),
                  _const_spec((W_A, D_MODEL)), _const_spec((W_B, D_MODEL)), _const_spec((W_C, D_MODEL)),
                  _const_spec((D_MODEL, D_MODEL)), _const_spec((1, D_MODEL))],
        out_specs=tok(D_MODEL),
        out_shape=jax.ShapeDtypeStruct((m, D_MODEL), F32),
        compiler_params=_params("parallel"),
        name="merge",
    )(x2d, ya, yb, yc, proj2, proj2, proj2, p["w_br_a"], p["w_br_b"], p["w_br_c"], p["w_out"],
      p["g_post_mix"])


def _ffn_kernel(x_ref, gpre_ref, up_ref, down_ref, gpost_ref, o_ref):
    x = x_ref[...]
    h = (x * lax.rsqrt(jnp.mean(x * x, axis=-1, keepdims=True) + EPS) * gpre_ref[...]).astype(BF16)
    f = jnp.zeros(x.shape, F32)
    for n in range(D_FF // FF_CHUNK):
        cols = slice(n * FF_CHUNK, (n + 1) * FF_CHUNK)
        act = jnp.maximum(_bdot(h, up_ref[:, cols]), 0.0)
        f = f + _bdot((act * act).astype(BF16), down_ref[cols, :])
    o_ref[...] = x + f * lax.rsqrt(jnp.mean(f * f, axis=-1, keepdims=True) + EPS) * gpost_ref[...]


def _ffn(x2d, p):
    m = x2d.shape[0]
    tm = min(TM_FFN, m)
    return pl.pallas_call(
        _ffn_kernel,
        grid=(m // tm,),
        in_specs=[pl.BlockSpec((tm, D_MODEL), lambda i: (i, 0)),
                  _const_spec((1, D_MODEL)), _const_spec((D_MODEL, D_FF)), _const_spec((D_FF, D_MODEL)),
                  _const_spec((1, D_MODEL))],
        out_specs=pl.BlockSpec((tm, D_MODEL), lambda i: (i, 0)),
        out_shape=jax.ShapeDtypeStruct((m, D_MODEL), F32),
        compiler_params=_params("parallel"),
        name="ffn",
    )(x2d, p["g_pre_ffn"], p["w_ff_up"], p["w_ff_down"], p["g_post_ffn"])


def _block_tril(groups, n):
    idx = np.arange(groups * n)
    same = (idx[:, None] // n) == (idx[None, :] // n)
    return jnp.asarray((same & (idx[:, None] >= idx[None, :])).astype(np.float32), BF16)


def _constants():
    lane_head = np.arange(LANE) // HEAD
    ones = (lane_head[:, None] == lane_head[None, :]).astype(np.float32)
    slot = np.arange(GROUP_W) // HEAD
    bd = (slot[:, None] == slot[None, :]).astype(np.float32)
    ea = np.zeros((AB_PAD, W_B), np.float32)
    eb = np.zeros((AB_PAD, W_B), np.float32)
    for h in range(H_B):
        ea[h, h * HEAD:(h + 1) * HEAD] = 1.0
        eb[H_B + h, h * HEAD:(h + 1) * HEAD] = 1.0
    i = np.arange(CHUNK)[:, None]
    j = (np.arange(GROUP_W) % HEAD)[None, :]
    same = lambda size: (i // size) == (j // size)
    levels = [same(2)] + [same(4 << l) & ~same(2 << l) for l in range(N_TRI_LEVELS - 1)]
    tri_levels = np.stack(levels).astype(np.float32)
    return {"tri_levels": jnp.asarray(tri_levels), "ones": jnp.asarray(ones, BF16), "ea": jnp.asarray(ea, BF16), "eb": jnp.asarray(eb, BF16),
            "bd_b": jnp.asarray(bd, BF16), "bd_f": jnp.asarray(bd, F32)}


def _stage_layer_params(l, g_pre_mix, g_post_mix, g_pre_ffn, g_post_ffn, w_in, rwkv_mu, rwkv_w0, rwkv_w2,
                        rwkv_a0, rwkv_a2, rwkv_g2, rwkv_kk, rwkv_ka, rwkv_rk, rwkv_ln_w, rwkv_ln_b,
                        gdn_conv_w, gdn_A_log, gdn_dt_bias, gdn_norm_w, att_rel_bias, w_br_a, w_br_b,
                        w_br_c, w_out, w_ff_up, w_ff_down):
    row = lambda a: a[l].reshape(1, -1).astype(F32)
    w = w_in[l]
    o_qkv = RWKV_PROJ
    o_a = o_qkv + GDN_CH
    o_z = o_a + 2 * H_B
    o_att = o_z + W_B
    o_gate = o_att + 3 * W_C
    ab_w = jnp.pad(w[:, o_a:o_z], ((0, 0), (0, AB_PAD - 2 * H_B)))
    w_perm = jnp.concatenate([w[:, o_gate:], w[:, o_att:o_gate], w[:, o_z:o_att], w[:, :RWKV_PROJ],
                              ab_w, w[:, o_qkv:o_a]], axis=1).astype(BF16)
    pad_ab = lambda a: jnp.pad(a[l].astype(F32), (0, AB_PAD - H_B)).reshape(1, AB_PAD)
    return {
        "g_pre_mix": row(g_pre_mix), "g_post_mix": row(g_post_mix),
        "g_pre_ffn": row(g_pre_ffn), "g_post_ffn": row(g_post_ffn),
        "w_in": w_perm,
        "rwkv_mu": row(rwkv_mu), "rwkv_w0": row(rwkv_w0), "rwkv_w2": rwkv_w2[l].astype(BF16),
        "rwkv_a0": row(rwkv_a0), "rwkv_a2": rwkv_a2[l].astype(BF16), "rwkv_g2": rwkv_g2[l].astype(BF16),
        "rwkv_kk": row(rwkv_kk), "rwkv_ka": row(rwkv_ka), "rwkv_rk": row(rwkv_rk),
        "rwkv_ln_w": row(rwkv_ln_w), "rwkv_ln_b": row(rwkv_ln_b),
        "gdn_conv_w": gdn_conv_w[l].astype(F32), "gdn_A_log": pad_ab(gdn_A_log),
        "gdn_dt_bias": pad_ab(gdn_dt_bias),
        "gdn_norm_w": jnp.tile(gdn_norm_w[l].astype(F32), H_B).reshape(1, W_B),
        "att_rel_bias": att_rel_bias[l].astype(F32),
        "w_br_a": w_br_a[l].astype(BF16), "w_br_b": w_br_b[l].astype(BF16), "w_br_c": w_br_c[l].astype(BF16),
        "w_out": w_out[l].astype(BF16), "w_ff_up": w_ff_up[l].astype(BF16),
        "w_ff_down": w_ff_down[l].astype(BF16),
    }


def _trunk_layer(x, p, consts, rwkv_shift, rwkv_s, gdn_conv, gdn_s, band_k, band_v):
    bsz, t_len, _ = x.shape
    m = bsz * t_len
    x2d = x.reshape(m, D_MODEL)
    proj2 = _in_proj(x2d, p["g_pre_mix"], p["w_in"])
    proj3 = proj2.reshape(bsz, t_len, N_PROJ)
    ya, s_a = _rwkv_mix(proj3, rwkv_shift.reshape(bsz, 1, RWKV_PROJ), rwkv_s, p, consts)
    yb, s_b = _gdn_mix(proj3, gdn_conv, gdn_s, p, consts)
    k_new = proj3[:, :, OFF_ATT + W_C:OFF_ATT + 2 * W_C]
    v_new = proj3[:, :, OFF_ATT + 2 * W_C:OFF_ATT + 3 * W_C]
    if band_k is None:
        lead = N_BAND_PREV * CHUNK
        yc = _band_prompt(proj3, p["att_rel_bias"])
        keep = min(lead, t_len)
        k_out, v_out = k_new[:, t_len - keep:], v_new[:, t_len - keep:]
    else:
        n_past = band_k.shape[1]
        k_all = jnp.concatenate([band_k.reshape(bsz, n_past, W_C), k_new], axis=1)
        v_all = jnp.concatenate([band_v.reshape(bsz, n_past, W_C), v_new], axis=1)
        yc = _band_step(proj3, k_all, v_all, _rel_bias_tile(p["att_rel_bias"], t_len, n_past + t_len, n_past))
        k_out, v_out = k_new, v_new
    x1 = _merge(x2d, ya.reshape(m, W_A), yb.reshape(m, W_B), yc.reshape(m, W_C), proj2, p)
    x2 = _ffn(x1, p)
    shift_new = proj3[:, t_len - 1, OFF_RWKV:OFF_RWKV + RWKV_PROJ]
    conv_new = proj3[:, t_len - (GDN_CONV - 1):, OFF_QKV:OFF_QKV + GDN_CH]
    new_state = (shift_new, s_a, conv_new, s_b,
                 k_out.reshape(bsz, -1, H_C, HEAD), v_out.reshape(bsz, -1, H_C, HEAD))
    return x2.reshape(bsz, t_len, D_MODEL), new_state


def kernel(x_prompt, x_sample, state_rwkv_shift, state_rwkv_wkv, state_gdn_conv, state_gdn_S, cache_band_k, cache_band_v, g_pre_mix, g_post_mix, g_pre_ffn, g_post_ffn, w_in, rwkv_mu, rwkv_w0, rwkv_w2, rwkv_a0, rwkv_a2, rwkv_g2, rwkv_kk, rwkv_ka, rwkv_rk, rwkv_ln_w, rwkv_ln_b, gdn_conv_w, gdn_A_log, gdn_dt_bias, gdn_norm_w, att_rel_bias, w_br_a, w_br_b, w_br_c, w_out, w_ff_up, w_ff_down):
    depth = w_in.shape[0]
    bp = x_prompt.shape[0]
    consts = _constants()
    xp, xs = x_prompt, x_sample
    p_new = [[] for _ in range(6)]
    s_new = [[] for _ in range(6)]
    for l in range(depth):
        p = _stage_layer_params(l, g_pre_mix, g_post_mix, g_pre_ffn, g_post_ffn, w_in, rwkv_mu, rwkv_w0,
                                rwkv_w2, rwkv_a0, rwkv_a2, rwkv_g2, rwkv_kk, rwkv_ka, rwkv_rk, rwkv_ln_w,
                                rwkv_ln_b, gdn_conv_w, gdn_A_log, gdn_dt_bias, gdn_norm_w, att_rel_bias,
                                w_br_a, w_br_b, w_br_c, w_out, w_ff_up, w_ff_down)
        xp, st_p = _trunk_layer(
            xp, p, consts,
            jnp.zeros((bp, RWKV_PROJ), F32), jnp.zeros((bp, H_A, HEAD, HEAD), F32),
            jnp.zeros((bp, GDN_CONV - 1, GDN_CH), F32), jnp.zeros((bp, H_B, HEAD, HEAD), F32),
            None, None)
        xs, st_s = _trunk_layer(
            xs, p, consts, state_rwkv_shift[l], state_rwkv_wkv[l], state_gdn_conv[l], state_gdn_S[l],
            cache_band_k[l], cache_band_v[l])
        for i in range(6):
            p_new[i].append(st_p[i])
            s_new[i].append(st_s[i])
    stk_p = [jnp.stack(a).astype(x_prompt.dtype) for a in p_new]
    stk_s = [jnp.stack(a).astype(x_sample.dtype) for a in s_new]
    return (xp, xs, *stk_p, *stk_s)
```

```python
import functools
import math

import numpy as np
import jax
import jax.numpy as jnp
from jax import lax
from jax.experimental import pallas as pl
from jax.experimental.pallas import tpu as pltpu

F32 = jnp.float32
BF16 = jnp.bfloat16

D_MODEL = 1024
HEAD = 64
H_A = 6
W_A = H_A * HEAD
DECAY_RANK = 64
ICLR_RANK = 64
GATE_RANK = 128
RWKV_PROJ = 3 * W_A + DECAY_RANK + ICLR_RANK + GATE_RANK
H_B = 6
W_B = H_B * HEAD
GDN_CONV = 4
GDN_CH = 3 * W_B
H_C = 4
W_C = H_C * HEAD
N_BAND_PREV = 8
MAX_REL = 128
CHUNK = 64
D_FF = 4 * D_MODEL
EPS = 1e-6
GN_EPS = 64e-5

LANE = 128
SUBLANE = 8
AB_PAD = LANE
OFF_GATE = 0
OFF_ATT = OFF_GATE + 3 * D_MODEL
OFF_Z = OFF_ATT + 3 * W_C
OFF_RWKV = OFF_Z + W_B
OFF_AB = OFF_RWKV + RWKV_PROJ
OFF_QKV = OFF_AB + AB_PAD
N_PROJ = OFF_QKV + GDN_CH
assert OFF_ATT % W_C == 0 and OFF_Z % W_B == 0 and OFF_RWKV % RWKV_PROJ == 0
assert OFF_AB % AB_PAD == 0 and OFF_QKV % GDN_CH == 0 and N_PROJ % LANE == 0

VMEM_LIMIT = 56 * 1024 * 1024
TM_IN = 256
TN_IN = 768
TM_FFN = 512
FF_CHUNK = 1024
MIX_GROUP = 4
BAND_Q_CHUNKS = 4


def _params(*sem):
    return pltpu.CompilerParams(dimension_semantics=sem, vmem_limit_bytes=VMEM_LIMIT)


def _const_spec(shape):
    nd = len(shape)
    return pl.BlockSpec(shape, lambda *_: (0,) * nd)


def _bdot(a, b):
    return jnp.dot(a, b, preferred_element_type=F32)


def _bdot_nt(a, b):
    return lax.dot_general(a, b, (((1,), (1,)), ((), ())), preferred_element_type=F32)


def _bdot_tn(a, b):
    return lax.dot_general(a, b, (((0,), (0,)), ((), ())), preferred_element_type=F32)


def _mm(a, b):
    return _bdot(a.astype(BF16), b.astype(BF16))


def _split(a):
    hi = a.astype(BF16)
    lo = (a - hi.astype(F32)).astype(BF16)
    return hi, lo


def _mm_lsplit(a, b_exact):
    hi, lo = _split(a)
    return _bdot(hi, b_exact) + _bdot(lo, b_exact)


def _mm_rsplit(a_exact, b):
    hi, lo = _split(b)
    return _bdot(a_exact, hi) + _bdot(a_exact, lo)


def _sigmoid(x):
    return 1.0 / (1.0 + jnp.exp(-x))


def _softplus(x):
    return jnp.maximum(x, 0.0) + jnp.log(1.0 + jnp.exp(-jnp.abs(x)))


def _rows_from(ref_rows, n):
    return jnp.concatenate([jnp.broadcast_to(r, (n, r.shape[-1])) for r in ref_rows], axis=0)


N_IN = RWKV_PROJ + GDN_CH + 2 * H_B + W_B + 3 * W_C + 3 * D_MODEL
SRC_QKV = RWKV_PROJ
SRC_AB = SRC_QKV + GDN_CH
SRC_Z = SRC_AB + 2 * H_B
TR_STAGE = 128
assert SRC_QKV % LANE == 0 and SRC_AB % LANE == 0


def _stage_w_in_kernel(w_ref, o_ref):
    w = w_ref[0]
    tail = w[:, SRC_Z:N_IN].astype(BF16)
    o_ref[0, :, OFF_GATE:OFF_GATE + 3 * D_MODEL] = tail[:, W_B + 3 * W_C:]
    o_ref[0, :, OFF_ATT:OFF_ATT + 3 * W_C] = tail[:, W_B:W_B + 3 * W_C]
    o_ref[0, :, OFF_Z:OFF_Z + W_B] = tail[:, 0:W_B]
    o_ref[0, :, OFF_RWKV:OFF_RWKV + RWKV_PROJ] = w[:, 0:RWKV_PROJ].astype(BF16)
    o_ref[0, :, OFF_AB:OFF_AB + AB_PAD] = jnp.zeros((w.shape[0], AB_PAD), BF16)
    o_ref[0, :, OFF_AB:OFF_AB + 2 * H_B] = w[:, SRC_AB:SRC_AB + 2 * H_B].astype(BF16)
    o_ref[0, :, OFF_QKV:OFF_QKV + GDN_CH] = w[:, SRC_QKV:SRC_QKV + GDN_CH].astype(BF16)


def _stage_w_in(w_in):
    depth, d_model, n_in = w_in.shape
    assert d_model == D_MODEL and n_in == N_IN
    return pl.pallas_call(
        _stage_w_in_kernel,
        grid=(depth, D_MODEL // TR_STAGE),
        in_specs=[pl.BlockSpec((1, TR_STAGE, N_IN), lambda l, i: (l, i, 0))],
        out_specs=pl.BlockSpec((1, TR_STAGE, N_PROJ), lambda l, i: (l, i, 0)),
        out_shape=jax.ShapeDtypeStruct((depth, D_MODEL, N_PROJ), BF16),
        compiler_params=_params("parallel", "parallel"),
        name="stage_w_in",
    )(w_in)


def _in_proj_kernel(x_ref, g_ref, w_ref, o_ref):
    x = x_ref[...]
    h = (x * lax.rsqrt(jnp.mean(x * x, axis=-1, keepdims=True) + EPS) * g_ref[...]).astype(BF16)
    for n in range(N_PROJ // TN_IN):
        cols = slice(n * TN_IN, (n + 1) * TN_IN)
        o_ref[:, cols] = _bdot(h, w_ref[:, cols])


def _in_proj(x2d, g, w_staged, layer):
    m = x2d.shape[0]
    tm = min(TM_IN, m)
    return pl.pallas_call(
        _in_proj_kernel,
        grid=(m // tm,),
        in_specs=[pl.BlockSpec((tm, D_MODEL), lambda i: (i, 0)),
                  _const_spec((1, D_MODEL)),
                  pl.BlockSpec((None, D_MODEL, N_PROJ), lambda i: (layer, 0, 0))],
        out_specs=pl.BlockSpec((tm, N_PROJ), lambda i: (i, 0)),
        out_shape=jax.ShapeDtypeStruct((m, N_PROJ), F32),
        compiler_params=_params("parallel"),
        name="in_proj",
    )(x2d, g, w_staged)


PACK = 4
GROUP_W = PACK * HEAD
GROUPS_PER_PAIR = 2 * H_A // PACK
N_TRI_LEVELS = int(math.log2(HEAD))
assert H_A == H_B and 2 * H_A % PACK == 0 and GROUP_W % LANE == 0 and CHUNK == HEAD


def _group_slots(pair_group):
    if pair_group < 2:
        return [(pair_group, h) for h in range(PACK)]
    return [(0, PACK), (0, PACK + 1), (1, PACK), (1, PACK + 1)]


def _to_groups(x, L):
    out = []
    for a in range(0, x.shape[0] // L, 2):
        ra, rb = slice(a * L, (a + 1) * L), slice((a + 1) * L, (a + 2) * L)
        out += [x[ra, 0:GROUP_W], x[rb, 0:GROUP_W],
                jnp.concatenate([x[ra, GROUP_W:W_A], x[rb, GROUP_W:W_A]], axis=1)]
    return out


def _from_groups(vals):
    rows = []
    half = (W_A - GROUP_W)
    for p in range(len(vals) // GROUPS_PER_PAIR):
        ga, gb, gc = vals[3 * p:3 * p + 3]
        rows.append(jnp.concatenate([ga, gc[:, 0:half]], axis=1))
        rows.append(jnp.concatenate([gb, gc[:, half:2 * half]], axis=1))
    return jnp.concatenate(rows, axis=0)


def _block_diag(x_b, mask_b):
    return jnp.concatenate([x_b] * PACK, axis=0) * mask_b


def _packed_tri_masks(L):
    r = lax.broadcasted_iota(jnp.int32, (L, GROUP_W), 0)
    c = jnp.bitwise_and(lax.broadcasted_iota(jnp.int32, (L, GROUP_W), 1), HEAD - 1)
    return c < r, c <= r, c == r


def _tri_inv_minus_eye_packed(a_list, mask_b, lvl_ref):
    tp = [-(a * lvl_ref[0]) for a in a_list]
    for lvl in range(1, N_TRI_LEVELS):
        m = lvl_ref[lvl]
        off = [a * m for a in a_list]
        off_bd = [_block_diag(x.astype(BF16), mask_b) for x in off]
        tp_b = [t.astype(BF16) for t in tp]
        m1 = [x + _bdot(tb, bd) for x, tb, bd in zip(off, tp_b, off_bd)]
        tp_bd = [_block_diag(tb, mask_b) for tb in tp_b]
        tp = [t - x - _bdot(x.astype(BF16), bd) for t, x, bd in zip(tp, m1, tp_bd)]
    return tp


def _seg_sum(x, ones_b, exact):
    outs = []
    for j in range(W_A // LANE):
        xs = x[:, j * LANE:(j + 1) * LANE]
        outs.append(_mm_lsplit(xs, ones_b) if exact else _bdot(xs.astype(BF16), ones_b))
    return jnp.concatenate(outs, axis=1)


def _load_state(s_scr, s0_ref):
    s_scr[...] = jnp.zeros(s_scr.shape, F32)
    for gi in range(s_scr.shape[0]):
        for j, (q, h) in enumerate(_group_slots(gi % GROUPS_PER_PAIR)):
            seq = 2 * (gi // GROUPS_PER_PAIR) + q
            s_scr[gi, j * HEAD:(j + 1) * HEAD, j * HEAD:(j + 1) * HEAD] = s0_ref[seq, h]


def _store_state(sout_ref, s_scr):
    for gi in range(s_scr.shape[0]):
        for j, (q, h) in enumerate(_group_slots(gi % GROUPS_PER_PAIR)):
            seq = 2 * (gi // GROUPS_PER_PAIR) + q
            sout_ref[seq, h] = s_scr[gi, j * HEAD:(j + 1) * HEAD, j * HEAD:(j + 1) * HEAD]


def _rwkv_kernel(c_ref, shift0_ref, s0_ref, mu_ref, w0_ref, w2_ref, a0_ref, a2_ref, g2_ref,
                 kk_ref, ka_ref, rk_ref, lnw_ref, lnb_ref, ones_ref, ltri_ref, bdb_ref, bdf_ref, lvl_ref,
                 y_ref, sout_ref, s_scr, prev_scr, *, n_valid):
    G, L = c_ref.shape[0], c_ref.shape[1]
    t = pl.program_id(1)

    @pl.when(t == 0)
    def _():
        _load_state(s_scr, s0_ref)
        prev_scr[...] = shift0_ref[...]

    c = c_ref[...].reshape(G * L, RWKV_PROJ)
    row = jnp.bitwise_and(lax.broadcasted_iota(jnp.int32, c.shape, 0), L - 1)
    prev_rows = _rows_from([prev_scr[g] for g in range(G)], L)
    c_prev = jnp.where(row == 0, prev_rows, pltpu.roll(c, 1, 0))
    for g in range(G):
        prev_scr[g] = c[(g + 1) * L - 1:(g + 1) * L, :]
    cm = c + (c_prev - c) * mu_ref[...]
    r = cm[:, 0:W_A]
    k = cm[:, W_A:2 * W_A]
    v = cm[:, 2 * W_A:3 * W_A]
    o = 3 * W_A
    wd = cm[:, o:o + DECAY_RANK]
    ad = cm[:, o + DECAY_RANK:o + DECAY_RANK + ICLR_RANK]
    gd = cm[:, o + DECAY_RANK + ICLR_RANK:]

    ones_b = ones_ref[...]
    logw = -math.exp(-0.5) * _sigmoid(w0_ref[...] + _mm(jnp.tanh(wd), w2_ref[...]))
    a = _sigmoid(a0_ref[...] + _mm(ad, a2_ref[...]))
    gate = _mm(_sigmoid(gd), g2_ref[...])
    kks = k * kk_ref[...]
    kkn = kks * lax.rsqrt(_seg_sum(kks * kks, ones_b, False) + 1e-6)
    k2 = k * (1.0 + (a - 1.0) * ka_ref[...])
    if n_valid < L:
        live = jnp.bitwise_and(lax.broadcasted_iota(jnp.int32, k.shape, 0), L - 1) < n_valid
        logw = jnp.where(live, logw, 0.0)
        kkn = jnp.where(live, kkn, 0.0)
        k2 = jnp.where(live, k2, 0.0)
    b = kkn * a

    cum = _mm_rsplit(ltri_ref[...], logw)
    w_inc = jnp.exp(cum)
    w_inv = jnp.exp(-cum)
    w_last = _rows_from([w_inc[(g + 1) * L - 1:(g + 1) * L, :] for g in range(G)], L)
    kt = k2 * w_inv
    bt = b * w_inv
    groups = lambda x: _to_groups(x, L)
    rt_g = groups((r * w_inc).astype(BF16))
    at_g = groups((kkn * jnp.exp(cum - logw)).astype(BF16))
    kt_g = groups(kt.astype(BF16))
    bt_g = groups(bt.astype(BF16))
    kl_g = groups((kt * w_last).astype(BF16))
    bl_neg_g = groups((-(bt * w_last)).astype(BF16))
    v_g = groups(v.astype(BF16))
    wl_g = groups(w_last)

    mask_b = bdb_ref[...]
    mask_f = bdf_ref[...]
    strict, incl, _ = _packed_tri_masks(L)
    n_groups = len(v_g)
    s_old = [s_scr[i] for i in range(n_groups)]
    s_old_b = [s.astype(BF16) for s in s_old]
    lhs_ar = [jnp.concatenate([x, y], axis=0) for x, y in zip(at_g, rt_g)]
    bt_bd = [_block_diag(x, mask_b) for x in bt_g]
    kt_bd = [_block_diag(x, mask_b) for x in kt_g]
    v_bd = [_block_diag(x, mask_b) for x in v_g]
    g_b = [_bdot_nt(x, bd) for x, bd in zip(lhs_ar, bt_bd)]
    g_k = [_bdot_nt(x, bd) for x, bd in zip(lhs_ar, kt_bd)]
    a_ab = [jnp.where(strict, x[:L], 0.0) for x in g_b]
    r_b_b = [jnp.where(incl, x[L:], 0.0).astype(BF16) for x in g_b]
    lhs_k = [jnp.concatenate([jnp.where(strict, x[:L], 0.0), jnp.where(incl, x[L:], 0.0)], axis=0).astype(BF16)
             for x in g_k]
    sv = [_bdot_nt(x, s) + _bdot(lk, vbd) for x, s, lk, vbd in zip(lhs_ar, s_old_b, lhs_k, v_bd)]
    tp = _tri_inv_minus_eye_packed(a_ab, mask_b, lvl_ref)
    u = [x[:L] + _bdot(t_.astype(BF16), _block_diag(x[:L].astype(BF16), mask_b))
         for x, t_ in zip(sv, tp)]
    u_b = [x.astype(BF16) for x in u]
    y_g = [x[L:] - _bdot(rb, _block_diag(ub, mask_b)) for x, rb, ub in zip(sv, r_b_b, u_b)]
    for i in range(n_groups):
        upd = _bdot_tn(jnp.concatenate([v_g[i], u_b[i]], axis=0),
                       jnp.concatenate([kl_g[i], bl_neg_g[i]], axis=0))
        s_scr[i] = s_old[i] * wl_g[i][0:1, :] + upd * mask_f

    y = _from_groups(y_g)
    mean = _seg_sum(y, ones_b, True) * (1.0 / HEAD)
    d = y - mean
    var = _seg_sum(d * d, ones_b, False) * (1.0 / HEAD)
    yn = d * lax.rsqrt(var + GN_EPS) * lnw_ref[...] + lnb_ref[...]
    bonus = _seg_sum(r * k2 * rk_ref[...], ones_b, True) * v
    y_ref[...] = ((yn + bonus) * gate).reshape(G, L, W_A)

    @pl.when(t == pl.num_programs(1) - 1)
    def _():
        _store_state(sout_ref, s_scr)


def _mixer_rows(proj3):
    t_len = proj3.shape[1]
    if t_len >= CHUNK:
        assert t_len % CHUNK == 0
        return proj3, CHUNK
    return jnp.pad(proj3, ((0, 0), (0, CHUNK - t_len), (0, 0))), t_len


def _rwkv_mix(proj3, shift0, s0, p, consts):
    bsz, t_len, _ = proj3.shape
    proj3, n_valid = _mixer_rows(proj3)
    t_pad = proj3.shape[1]
    L, G = CHUNK, MIX_GROUP
    assert bsz % G == 0 and G % 2 == 0
    row = lambda n: _const_spec((1, n))
    y, s_new = pl.pallas_call(
        functools.partial(_rwkv_kernel, n_valid=n_valid),
        grid=(bsz // G, t_pad // L),
        in_specs=[pl.BlockSpec((G, L, RWKV_PROJ), lambda b, t: (b, t, OFF_RWKV // RWKV_PROJ)),
                  pl.BlockSpec((G, 1, RWKV_PROJ), lambda b, t: (b, 0, 0)),
                  pl.BlockSpec((G, H_A, HEAD, HEAD), lambda b, t: (b, 0, 0, 0)),
                  row(RWKV_PROJ), row(W_A), _const_spec((DECAY_RANK, W_A)), row(W_A),
                  _const_spec((ICLR_RANK, W_A)), _const_spec((GATE_RANK, W_A)),
                  row(W_A), row(W_A), row(W_A), row(W_A), row(W_A),
                  _const_spec((LANE, LANE)), _const_spec((G * L, G * L)),
                  _const_spec((GROUP_W, GROUP_W)), _const_spec((GROUP_W, GROUP_W)),
                  _const_spec((N_TRI_LEVELS, CHUNK, GROUP_W))],
        out_specs=[pl.BlockSpec((G, L, W_A), lambda b, t: (b, t, 0)),
                   pl.BlockSpec((G, H_A, HEAD, HEAD), lambda b, t: (b, 0, 0, 0))],
        out_shape=[jax.ShapeDtypeStruct((bsz, t_pad, W_A), F32),
                   jax.ShapeDtypeStruct((bsz, H_A, HEAD, HEAD), F32)],
        scratch_shapes=[pltpu.VMEM((G // 2 * GROUPS_PER_PAIR, GROUP_W, GROUP_W), F32),
                        pltpu.VMEM((G, 1, RWKV_PROJ), F32)],
        compiler_params=_params("parallel", "arbitrary"),
        name="rwkv_mix",
    )(proj3, shift0, s0, p["rwkv_mu"], p["rwkv_w0"], p["rwkv_w2"], p["rwkv_a0"], p["rwkv_a2"],
      p["rwkv_g2"], p["rwkv_kk"], p["rwkv_ka"], p["rwkv_rk"], p["rwkv_ln_w"], p["rwkv_ln_b"],
      consts["ones"], _block_tril(G, L), consts["bd_b"], consts["bd_f"], consts["tri_levels"])
    return y[:, :t_len], s_new


def _gdn_kernel(qkv_ref, ab_ref, z_ref, conv0_ref, s0_ref, cw_ref, alog_ref, dtb_ref, nw_ref,
                ones_ref, ltri_ref, ea_ref, eb_ref, bdb_ref, bdf_ref, lvl_ref,
                y_ref, sout_ref, s_scr, xpad_scr, *, n_valid):
    G, L = qkv_ref.shape[0], qkv_ref.shape[1]
    t = pl.program_id(1)
    npre = GDN_CONV - 1
    base_row = SUBLANE

    @pl.when(t == 0)
    def _():
        _load_state(s_scr, s0_ref)
        xpad_scr[:, base_row - npre:base_row, :] = conv0_ref[...]

    convs = []
    for g in range(G):
        x = qkv_ref[g]
        xpad_scr[g, base_row:base_row + L, :] = x
        conv = x * cw_ref[npre:npre + 1, :]
        for j in range(npre):
            lo = base_row - npre + j
            conv = conv + xpad_scr[g, lo:lo + L, :] * cw_ref[j:j + 1, :]
        xpad_scr[g, base_row - npre:base_row, :] = xpad_scr[g, base_row + L - npre:base_row + L, :]
        convs.append(conv)
    conv = jnp.concatenate(convs, axis=0)
    qkv = conv * _sigmoid(conv)

    ones_b = ones_ref[...]
    q = qkv[:, 0:W_B]
    k = qkv[:, W_B:2 * W_B]
    v = qkv[:, 2 * W_B:]
    q = q * lax.rsqrt(_seg_sum(q * q, ones_b, False) + 1e-6) * (HEAD ** -0.5)
    k = k * lax.rsqrt(_seg_sum(k * k, ones_b, False) + 1e-6)

    ab = ab_ref[...].reshape(G * L, AB_PAD)
    g_row = -jnp.exp(alog_ref[...]) * _softplus(ab + dtb_ref[...])
    g_exp = _mm_lsplit(g_row, ea_ref[...])
    beta = _mm_lsplit(_sigmoid(ab), eb_ref[...])
    if n_valid < L:
        live = jnp.bitwise_and(lax.broadcasted_iota(jnp.int32, k.shape, 0), L - 1) < n_valid
        k = jnp.where(live, k, 0.0)
        beta = jnp.where(live, beta, 0.0)
        g_exp = jnp.where(live, g_exp, 0.0)
    gc = _mm_rsplit(ltri_ref[...], g_exp)
    g_last = _rows_from([gc[(g + 1) * L - 1:(g + 1) * L, :] for g in range(G)], L)
    e_gc = jnp.exp(gc)
    kbeta = k * beta
    vbeta = v * beta
    kw = kbeta * e_gc
    groups = lambda x: _to_groups(x, L)
    k_g = groups(k.astype(BF16))
    q_g = groups(q.astype(BF16))
    kbeta_g = groups(kbeta.astype(BF16))
    vbeta_g = groups(vbeta)
    kw_g = groups(kw)
    qe_g = groups((q * e_gc).astype(BF16))
    kd_g = groups((k * jnp.exp(g_last - gc)).astype(BF16))
    gc_g = groups(gc)
    el_g = groups(jnp.exp(g_last))

    mask_b = bdb_ref[...]
    mask_f = bdf_ref[...]
    strict, incl, eye = _packed_tri_masks(L)
    n_groups = len(k_g)
    s_old = [s_scr[i] for i in range(n_groups)]
    s_old_b = [s.astype(BF16) for s in s_old]
    gamma = []
    for x in gc_g:
        g_row_j = jnp.sum(jnp.where(eye, x, 0.0), axis=0, keepdims=True)
        gamma.append(jnp.where(incl, jnp.exp(jnp.minimum(x - g_row_j, 0.0)), 0.0))
    kk = [_bdot_nt(jnp.concatenate([x, y], axis=0), _block_diag(kb, mask_b))
          for x, y, kb in zip(kbeta_g, q_g, k_g)]
    lower = [jnp.where(strict, x[:L] * gm, 0.0) for x, gm in zip(kk, gamma)]
    att_b = [(x[L:] * gm).astype(BF16) for x, gm in zip(kk, gamma)]
    tp_b = [x.astype(BF16) for x in _tri_inv_minus_eye_packed(lower, mask_b, lvl_ref)]
    u_pre = [x + _bdot(t_, _block_diag(x.astype(BF16), mask_b)) for x, t_ in zip(vbeta_g, tp_b)]
    w_b = [(x + _bdot(t_, _block_diag(x.astype(BF16), mask_b))).astype(BF16) for x, t_ in zip(kw_g, tp_b)]
    ws = [_bdot(jnp.concatenate([w_, qe], axis=0), s) for w_, qe, s in zip(w_b, qe_g, s_old_b)]
    u_b = [(up - x[:L]).astype(BF16) for up, x in zip(u_pre, ws)]
    o_g = [x[L:] + _bdot(a_, _block_diag(ub, mask_b)) for x, a_, ub in zip(ws, att_b, u_b)]
    for i in range(n_groups):
        s_scr[i] = s_old[i] * el_g[i][0:1, :] + _bdot_tn(kd_g[i], u_b[i]) * mask_f

    o = _from_groups(o_g)
    on = o * lax.rsqrt(_seg_sum(o * o, ones_b, False) * (1.0 / HEAD) + EPS) * nw_ref[...]
    z = z_ref[...].reshape(G * L, W_B)
    y_ref[...] = (on * (z * _sigmoid(z))).reshape(G, L, W_B)

    @pl.when(t == pl.num_programs(1) - 1)
    def _():
        _store_state(sout_ref, s_scr)


def _gdn_mix(proj3, conv0, s0, p, consts):
    bsz, t_len, _ = proj3.shape
    assert t_len >= GDN_CONV - 1
    proj3, n_valid = _mixer_rows(proj3)
    t_pad = proj3.shape[1]
    L, G = CHUNK, MIX_GROUP
    assert bsz % G == 0 and G % 2 == 0
    y, s_new = pl.pallas_call(
        functools.partial(_gdn_kernel, n_valid=n_valid),
        grid=(bsz // G, t_pad // L),
        in_specs=[pl.BlockSpec((G, L, GDN_CH), lambda b, t: (b, t, OFF_QKV // GDN_CH)),
                  pl.BlockSpec((G, L, AB_PAD), lambda b, t: (b, t, OFF_AB // AB_PAD)),
                  pl.BlockSpec((G, L, W_B), lambda b, t: (b, t, OFF_Z // W_B)),
                  pl.BlockSpec((G, GDN_CONV - 1, GDN_CH), lambda b, t: (b, 0, 0)),
                  pl.BlockSpec((G, H_B, HEAD, HEAD), lambda b, t: (b, 0, 0, 0)),
                  _const_spec((GDN_CONV, GDN_CH)), _const_spec((1, AB_PAD)), _const_spec((1, AB_PAD)),
                  _const_spec((1, W_B)), _const_spec((LANE, LANE)), _const_spec((G * L, G * L)),
                  _const_spec((AB_PAD, W_B)), _const_spec((AB_PAD, W_B)),
                  _const_spec((GROUP_W, GROUP_W)), _const_spec((GROUP_W, GROUP_W)),
                  _const_spec((N_TRI_LEVELS, CHUNK, GROUP_W))],
        out_specs=[pl.BlockSpec((G, L, W_B), lambda b, t: (b, t, 0)),
                   pl.BlockSpec((G, H_B, HEAD, HEAD), lambda b, t: (b, 0, 0, 0))],
        out_shape=[jax.ShapeDtypeStruct((bsz, t_pad, W_B), F32),
                   jax.ShapeDtypeStruct((bsz, H_B, HEAD, HEAD), F32)],
        scratch_shapes=[pltpu.VMEM((G // 2 * GROUPS_PER_PAIR, GROUP_W, GROUP_W), F32),
                        pltpu.VMEM((G, SUBLANE + L, GDN_CH), F32)],
        compiler_params=_params("parallel", "arbitrary"),
        name="gdn_mix",
    )(proj3, proj3, proj3, conv0, s0, p["gdn_conv_w"], p["gdn_A_log"], p["gdn_dt_bias"],
      p["gdn_norm_w"], consts["ones"], _block_tril(G, L), consts["ea"], consts["eb"],
      consts["bd_b"], consts["bd_f"], consts["tri_levels"])
    return y[:, :t_len], s_new


def _attend(q, kwin, vwin, bias_ref, min_col, o_ref):
    q_b = (q * (HEAD ** -0.5)).astype(BF16)
    heads = range(H_C)
    sl = lambda x, h: x[:, h * HEAD:(h + 1) * HEAD]
    s = [_bdot_nt(sl(q_b, h), sl(kwin, h)) + bias_ref[h] for h in heads]
    if min_col is not None:
        col = lax.broadcasted_iota(jnp.int32, s[0].shape, 1)
        s = [jnp.where(col >= min_col, x, -jnp.inf) for x in s]
    e = [jnp.exp(x - jnp.max(x, axis=-1, keepdims=True)) for x in s]
    denom = [jnp.sum(x, axis=-1, keepdims=True) for x in e]
    pv = [_bdot(x.astype(BF16), sl(vwin, h)) for x, h in zip(e, heads)]
    for h in heads:
        o_ref[0, :, h * HEAD:(h + 1) * HEAD] = pv[h] / denom[h]


def _band_prompt_kernel(q_ref, k_ref, v_ref, bias_ref, o_ref, kpad_scr, vpad_scr):
    step = pl.program_id(1)
    n_q = q_ref.shape[1]
    lead = N_BAND_PREV * CHUNK

    @pl.when(step == 0)
    def _():
        zeros = jnp.zeros((lead, W_C), BF16)
        kpad_scr[0:lead, :] = zeros
        vpad_scr[0:lead, :] = zeros
        kpad_scr[lead:, :] = k_ref[0].astype(BF16)
        vpad_scr[lead:, :] = v_ref[0].astype(BF16)

    start = pl.multiple_of(step * n_q, CHUNK)
    kwin = kpad_scr[pl.ds(start, lead + n_q), :]
    vwin = vpad_scr[pl.ds(start, lead + n_q), :]
    _attend(q_ref[0], kwin, vwin, bias_ref, lead - step * n_q, o_ref)


def _band_prompt(proj3, rel_bias):
    bsz, t_len, _ = proj3.shape
    lead = N_BAND_PREV * CHUNK
    n_q = min(BAND_Q_CHUNKS * CHUNK, t_len)
    assert t_len % n_q == 0 and n_q % CHUNK == 0
    n_k = lead + n_q
    first = (np.arange(n_q)[:, None] // CHUNK) * CHUNK
    cols = np.arange(n_k)[None, :]
    in_band = (cols >= first) & (cols < first + lead + CHUNK)
    bias = jnp.where(in_band[None], _rel_bias_tile(rel_bias, n_q, n_k, lead), -jnp.inf)
    cq = OFF_ATT // W_C
    return pl.pallas_call(
        _band_prompt_kernel,
        grid=(bsz, t_len // n_q),
        in_specs=[pl.BlockSpec((1, n_q, W_C), lambda b, c: (b, c, cq)),
                  pl.BlockSpec((1, t_len, W_C), lambda b, c: (b, 0, cq + 1)),
                  pl.BlockSpec((1, t_len, W_C), lambda b, c: (b, 0, cq + 2)),
                  _const_spec((H_C, n_q, n_k))],
        out_specs=pl.BlockSpec((1, n_q, W_C), lambda b, c: (b, c, 0)),
        out_shape=jax.ShapeDtypeStruct((bsz, t_len, W_C), F32),
        scratch_shapes=[pltpu.VMEM((lead + t_len, W_C), BF16),
                        pltpu.VMEM((lead + t_len, W_C), BF16)],
        compiler_params=_params("parallel", "arbitrary"),
        name="band_prompt",
    )(proj3, proj3, proj3, bias)


def _band_step_kernel(q_ref, k_ref, v_ref, bias_ref, o_ref):
    _attend(q_ref[0], k_ref[0].astype(BF16), v_ref[0].astype(BF16), bias_ref, None, o_ref)


def _band_step(proj3, k_all, v_all, bias):
    bsz, t_len, _ = proj3.shape
    n_keys = k_all.shape[1]
    return pl.pallas_call(
        _band_step_kernel,
        grid=(bsz,),
        in_specs=[pl.BlockSpec((1, t_len, W_C), lambda b: (b, 0, OFF_ATT // W_C)),
                  pl.BlockSpec((1, n_keys, W_C), lambda b: (b, 0, 0)),
                  pl.BlockSpec((1, n_keys, W_C), lambda b: (b, 0, 0)),
                  _const_spec((H_C, t_len, n_keys))],
        out_specs=pl.BlockSpec((1, t_len, W_C), lambda b: (b, 0, 0)),
        out_shape=jax.ShapeDtypeStruct((bsz, t_len, W_C), F32),
        compiler_params=_params("parallel"),
        name="band_step",
    )(proj3, k_all, v_all, bias)


def _rel_bias_tile(rel_bias, n_q, n_k, lead):
    n_d = n_q + n_k - 1
    d = np.clip(lead + n_q - 1 - np.arange(n_d), -MAX_REL, MAX_REL) + MAX_REL
    n_hi = int(np.sum(d == 2 * MAX_REL)) - 1 if d[0] == 2 * MAX_REL else 0
    n_lo = int(np.sum(d == 0)) - 1 if d[-1] == 0 else 0
    core = rel_bias[:, int(d[-1 - n_lo]):int(d[n_hi]) + 1][:, ::-1]
    rev = jnp.concatenate([jnp.repeat(rel_bias[:, -1:], n_hi, axis=1), core,
                           jnp.repeat(rel_bias[:, :1], n_lo, axis=1)], axis=1)
    heads = rel_bias.shape[0]
    flat = jnp.broadcast_to(rev[:, None, :], (heads, n_q, n_d)).reshape(heads, n_q * n_d)
    skew = flat[:, n_q - 1:n_q - 1 + n_q * (n_d - 1)].reshape(heads, n_q, n_d - 1)
    return skew[:, :, :n_k]


def _rms(x):
    return x * lax.rsqrt(jnp.mean(x * x, axis=-1, keepdims=True) + EPS)


def _merge_ffn_kernel(x_ref, ya_ref, yb_ref, yc_ref, g0_ref, g1_ref, g2_ref, wa_ref, wb_ref, wc_ref,
                      wo_ref, gmix_ref, gpre_ref, up_ref, down_ref, gpost_ref, o_ref):
    merged = (_sigmoid(g0_ref[...]) * _mm(ya_ref[...], wa_ref[...])
              + _sigmoid(g1_ref[...]) * _mm(yb_ref[...], wb_ref[...])
              + _sigmoid(g2_ref[...]) * _mm(yc_ref[...], wc_ref[...]))
    x1 = x_ref[...] + _rms(_mm(merged, wo_ref[...])) * gmix_ref[...]
    h = (_rms(x1) * gpre_ref[...]).astype(BF16)
    f = jnp.zeros(x1.shape, F32)
    for n in range(D_FF // FF_CHUNK):
        cols = slice(n * FF_CHUNK, (n + 1) * FF_CHUNK)
        act = jnp.maximum(_bdot(h, up_ref[:, cols]), 0.0)
        f = f + _bdot((act * act).astype(BF16), down_ref[cols, :])
    o_ref[...] = x1 + _rms(f) * gpost_ref[...]


def _merge_ffn(x2d, ya, yb, yc, proj2, p):
    m = x2d.shape[0]
    tm = min(TM_FFN, m)
    tok = lambda n, c=0: pl.BlockSpec((tm, n), lambda i, c=c: (i, c))
    held = lambda *shape: pl.BlockSpec(shape, lambda i: (0,) * len(shape), pipeline_mode=pl.Buffered(1))
    return pl.pallas_call(
        _merge_ffn_kernel,
        grid=(m // tm,),
        in_specs=[tok(D_MODEL), tok(W_A), tok(W_B), tok(W_C),
                  tok(D_MODEL, 0), tok(D_MODEL, 1), tok(D_MODEL, 2),
                  held(W_A, D_MODEL), held(W_B, D_MODEL), held(W_C, D_MODEL), held(D_MODEL, D_MODEL),
                  held(1, D_MODEL), held(1, D_MODEL), held(D_MODEL, D_FF), held(D_FF, D_MODEL),
                  held(1, D_MODEL)],
        out_specs=tok(D_MODEL),
        out_shape=jax.ShapeDtypeStruct((m, D_MODEL), F32),
        compiler_params=_params("parallel"),
        name="merge_ffn",
    )(x2d, ya, yb, yc, proj2, proj2, proj2, p["w_br_a"], p["w_br_b"], p["w_br_c"], p["w_out"],
      p["g_post_mix"], p["g_pre_ffn"], p["w_ff_up"], p["w_ff_down"], p["g_post_ffn"])


def _block_tril(groups, n):
    idx = np.arange(groups * n)
    same = (idx[:, None] // n) == (idx[None, :] // n)
    return jnp.asarray((same & (idx[:, None] >= idx[None, :])).astype(np.float32), BF16)


def _constants():
    lane_head = np.arange(LANE) // HEAD
    ones = (lane_head[:, None] == lane_head[None, :]).astype(np.float32)
    slot = np.arange(GROUP_W) // HEAD
    bd = (slot[:, None] == slot[None, :]).astype(np.float32)
    ea = np.zeros((AB_PAD, W_B), np.float32)
    eb = np.zeros((AB_PAD, W_B), np.float32)
    for h in range(H_B):
        ea[h, h * HEAD:(h + 1) * HEAD] = 1.0
        eb[H_B + h, h * HEAD:(h + 1) * HEAD] = 1.0
    i = np.arange(CHUNK)[:, None]
    j = (np.arange(GROUP_W) % HEAD)[None, :]
    same = lambda size: (i // size) == (j // size)
    levels = [same(2)] + [same(4 << l) & ~same(2 << l) for l in range(N_TRI_LEVELS - 1)]
    tri_levels = np.stack(levels).astype(np.float32)
    return {"tri_levels": jnp.asarray(tri_levels), "ones": jnp.asarray(ones, BF16), "ea": jnp.asarray(ea, BF16), "eb": jnp.asarray(eb, BF16),
            "bd_b": jnp.asarray(bd, BF16), "bd_f": jnp.asarray(bd, F32)}


def _stage_layer_params(l, w_staged, g_pre_mix, g_post_mix, g_pre_ffn, g_post_ffn, rwkv_mu, rwkv_w0, rwkv_w2,
                        rwkv_a0, rwkv_a2, rwkv_g2, rwkv_kk, rwkv_ka, rwkv_rk, rwkv_ln_w, rwkv_ln_b,
                        gdn_conv_w, gdn_A_log, gdn_dt_bias, gdn_norm_w, att_rel_bias, w_br_a, w_br_b,
                        w_br_c, w_out, w_ff_up, w_ff_down):
    row = lambda a: a[l].reshape(1, -1).astype(F32)
    pad_ab = lambda a: jnp.pad(a[l].astype(F32), (0, AB_PAD - H_B)).reshape(1, AB_PAD)
    return {
        "g_pre_mix": row(g_pre_mix), "g_post_mix": row(g_post_mix),
        "g_pre_ffn": row(g_pre_ffn), "g_post_ffn": row(g_post_ffn),
        "w_in": w_staged, "layer": l,
        "rwkv_mu": row(rwkv_mu), "rwkv_w0": row(rwkv_w0), "rwkv_w2": rwkv_w2[l].astype(BF16),
        "rwkv_a0": row(rwkv_a0), "rwkv_a2": rwkv_a2[l].astype(BF16), "rwkv_g2": rwkv_g2[l].astype(BF16),
        "rwkv_kk": row(rwkv_kk), "rwkv_ka": row(rwkv_ka), "rwkv_rk": row(rwkv_rk),
        "rwkv_ln_w": row(rwkv_ln_w), "rwkv_ln_b": row(rwkv_ln_b),
        "gdn_conv_w": gdn_conv_w[l].astype(F32), "gdn_A_log": pad_ab(gdn_A_log),
        "gdn_dt_bias": pad_ab(gdn_dt_bias),
        "gdn_norm_w": jnp.tile(gdn_norm_w[l].astype(F32), H_B).reshape(1, W_B),
        "att_rel_bias": att_rel_bias[l].astype(F32),
        "w_br_a": w_br_a[l].astype(BF16), "w_br_b": w_br_b[l].astype(BF16), "w_br_c": w_br_c[l].astype(BF16),
        "w_out": w_out[l].astype(BF16), "w_ff_up": w_ff_up[l].astype(BF16),
        "w_ff_down": w_ff_down[l].astype(BF16),
    }


def _trunk_layer(x, p, consts, rwkv_shift, rwkv_s, gdn_conv, gdn_s, band_k, band_v):
    bsz, t_len, _ = x.shape
    m = bsz * t_len
    x2d = x.reshape(m, D_MODEL)
    proj2 = _in_proj(x2d, p["g_pre_mix"], p["w_in"], p["layer"])
    proj3 = proj2.reshape(bsz, t_len, N_PROJ)
    ya, s_a = _rwkv_mix(proj3, rwkv_shift.reshape(bsz, 1, RWKV_PROJ), rwkv_s, p, consts)
    yb, s_b = _gdn_mix(proj3, gdn_conv, gdn_s, p, consts)
    k_new = proj3[:, :, OFF_ATT + W_C:OFF_ATT + 2 * W_C]
    v_new = proj3[:, :, OFF_ATT + 2 * W_C:OFF_ATT + 3 * W_C]
    if band_k is None:
        lead = N_BAND_PREV * CHUNK
        yc = _band_prompt(proj3, p["att_rel_bias"])
        keep = min(lead, t_len)
        k_out, v_out = k_new[:, t_len - keep:], v_new[:, t_len - keep:]
    else:
        n_past = band_k.shape[1]
        k_all = jnp.concatenate([band_k.reshape(bsz, n_past, W_C), k_new], axis=1)
        v_all = jnp.concatenate([band_v.reshape(bsz, n_past, W_C), v_new], axis=1)
        yc = _band_step(proj3, k_all, v_all, _rel_bias_tile(p["att_rel_bias"], t_len, n_past + t_len, n_past))
        k_out, v_out = k_new, v_new
    x2 = _merge_ffn(x2d, ya.reshape(m, W_A), yb.reshape(m, W_B), yc.reshape(m, W_C), proj2, p)
    shift_new = proj3[:, t_len - 1, OFF_RWKV:OFF_RWKV + RWKV_PROJ]
    conv_new = proj3[:, t_len - (GDN_CONV - 1):, OFF_QKV:OFF_QKV + GDN_CH]
    new_state = (shift_new, s_a, conv_new, s_b,
                 k_out.reshape(bsz, -1, H_C, HEAD), v_out.reshape(bsz, -1, H_C, HEAD))
    return x2.reshape(bsz, t_len, D_MODEL), new_state


def kernel(x_prompt, x_sample, state_rwkv_shift, state_rwkv_wkv, state_gdn_conv, state_gdn_S, cache_band_k, cache_band_v, g_pre_mix, g_post_mix, g_pre_ffn, g_post_ffn, w_in, rwkv_mu, rwkv_w0, rwkv_w2, rwkv_a0, rwkv_a2, rwkv_g2, rwkv_kk, rwkv_ka, rwkv_rk, rwkv_ln_w, rwkv_ln_b, gdn_conv_w, gdn_A_log, gdn_dt_bias, gdn_norm_w, att_rel_bias, w_br_a, w_br_b, w_br_c, w_out, w_ff_up, w_ff_down):
    depth = w_in.shape[0]
    bp = x_prompt.shape[0]
    consts = _constants()
    w_staged = _stage_w_in(w_in)
    xp, xs = x_prompt, x_sample
    p_new = [[] for _ in range(6)]
    s_new = [[] for _ in range(6)]
    for l in range(depth):
        p = _stage_layer_params(l, w_staged, g_pre_mix, g_post_mix, g_pre_ffn, g_post_ffn, rwkv_mu, rwkv_w0,
                                rwkv_w2, rwkv_a0, rwkv_a2, rwkv_g2, rwkv_kk, rwkv_ka, rwkv_rk, rwkv_ln_w,
                                rwkv_ln_b, gdn_conv_w, gdn_A_log, gdn_dt_bias, gdn_norm_w, att_rel_bias,
                                w_br_a, w_br_b, w_br_c, w_out, w_ff_up, w_ff_down)
        xp, st_p = _trunk_layer(
            xp, p, consts,
            jnp.zeros((bp, RWKV_PROJ), F32), jnp.zeros((bp, H_A, HEAD, HEAD), F32),
            jnp.zeros((bp, GDN_CONV - 1, GDN_CH), F32), jnp.zeros((bp, H_B, HEAD, HEAD), F32),
            None, None)
        xs, st_s = _trunk_layer(
            xs, p, consts, state_rwkv_shift[l], state_rwkv_wkv[l], state_gdn_conv[l], state_gdn_S[l],
            cache_band_k[l], cache_band_v[l])
        for i in range(6):
            p_new[i].append(st_p[i])
            s_new[i].append(st_s[i])
    stk_p = [jnp.stack(a).astype(x_prompt.dtype) for a in p_new]
    stk_s = [jnp.stack(a).astype(x_sample.dtype) for a in s_new]
    return (xp, xs, *stk_p, *stk_s)
```

```python
import functools
import math

import numpy as np
import jax
import jax.numpy as jnp
from jax import lax
from jax.experimental import pallas as pl
from jax.experimental.pallas import tpu as pltpu

F32 = jnp.float32
BF16 = jnp.bfloat16

D_MODEL = 1024
HEAD = 64
H_A = 6
W_A = H_A * HEAD
DECAY_RANK = 64
ICLR_RANK = 64
GATE_RANK = 128
RWKV_PROJ = 3 * W_A + DECAY_RANK + ICLR_RANK + GATE_RANK
H_B = 6
W_B = H_B * HEAD
GDN_CONV = 4
GDN_CH = 3 * W_B
H_C = 4
W_C = H_C * HEAD
N_BAND_PREV = 8
MAX_REL = 128
CHUNK = 64
D_FF = 4 * D_MODEL
EPS = 1e-6
GN_EPS = 64e-5

LANE = 128
SUBLANE = 8
AB_PAD = LANE
OFF_GATE = 0
OFF_ATT = OFF_GATE + 3 * D_MODEL
OFF_Z = OFF_ATT + 3 * W_C
OFF_RWKV = OFF_Z + W_B
OFF_AB = OFF_RWKV + RWKV_PROJ
OFF_QKV = OFF_AB + AB_PAD
N_PROJ = OFF_QKV + GDN_CH
assert OFF_ATT % W_C == 0 and OFF_Z % W_B == 0 and OFF_RWKV % RWKV_PROJ == 0
assert OFF_AB % AB_PAD == 0 and OFF_QKV % GDN_CH == 0 and N_PROJ % LANE == 0

VMEM_LIMIT = 56 * 1024 * 1024
TM_IN = 256
TN_IN = 768
TM_FFN = 512
FF_CHUNK = 1024
MIX_GROUP = 8
BAND_Q_CHUNKS = 4


def _params(*sem):
    return pltpu.CompilerParams(dimension_semantics=sem, vmem_limit_bytes=VMEM_LIMIT)


def _const_spec(shape):
    nd = len(shape)
    return pl.BlockSpec(shape, lambda *_: (0,) * nd)


def _bdot(a, b):
    return jnp.dot(a, b, preferred_element_type=F32)


def _bdot_nt(a, b):
    return lax.dot_general(a, b, (((1,), (1,)), ((), ())), preferred_element_type=F32)


def _bdot_tn(a, b):
    return lax.dot_general(a, b, (((0,), (0,)), ((), ())), preferred_element_type=F32)


def _mm(a, b):
    return _bdot(a.astype(BF16), b.astype(BF16))


def _split(a):
    hi = a.astype(BF16)
    lo = (a - hi.astype(F32)).astype(BF16)
    return hi, lo


def _mm_lsplit(a, b_exact):
    hi, lo = _split(a)
    return _bdot(hi, b_exact) + _bdot(lo, b_exact)


def _mm_rsplit(a_exact, b):
    hi, lo = _split(b)
    return _bdot(a_exact, hi) + _bdot(a_exact, lo)


def _sigmoid(x):
    return 0.5 * jnp.tanh(0.5 * x) + 0.5


def _softplus(x):
    return jnp.maximum(x, 0.0) + jnp.log(1.0 + jnp.exp(-jnp.abs(x)))


def _rows_from(ref_rows, n):
    return jnp.concatenate([jnp.broadcast_to(r, (n, r.shape[-1])) for r in ref_rows], axis=0)


N_IN = RWKV_PROJ + GDN_CH + 2 * H_B + W_B + 3 * W_C + 3 * D_MODEL
SRC_QKV = RWKV_PROJ
SRC_AB = SRC_QKV + GDN_CH
SRC_Z = SRC_AB + 2 * H_B
TR_STAGE = 128
assert SRC_QKV % LANE == 0 and SRC_AB % LANE == 0


def _stage_w_in_kernel(w_ref, o_ref):
    w = w_ref[0]
    tail = w[:, SRC_Z:N_IN].astype(BF16)
    o_ref[0, :, OFF_GATE:OFF_GATE + 3 * D_MODEL] = tail[:, W_B + 3 * W_C:]
    o_ref[0, :, OFF_ATT:OFF_ATT + 3 * W_C] = tail[:, W_B:W_B + 3 * W_C]
    o_ref[0, :, OFF_Z:OFF_Z + W_B] = tail[:, 0:W_B]
    o_ref[0, :, OFF_RWKV:OFF_RWKV + RWKV_PROJ] = w[:, 0:RWKV_PROJ].astype(BF16)
    o_ref[0, :, OFF_AB:OFF_AB + AB_PAD] = jnp.zeros((w.shape[0], AB_PAD), BF16)
    o_ref[0, :, OFF_AB:OFF_AB + 2 * H_B] = w[:, SRC_AB:SRC_AB + 2 * H_B].astype(BF16)
    o_ref[0, :, OFF_QKV:OFF_QKV + GDN_CH] = w[:, SRC_QKV:SRC_QKV + GDN_CH].astype(BF16)


def _stage_w_in(w_in):
    depth, d_model, n_in = w_in.shape
    assert d_model == D_MODEL and n_in == N_IN
    return pl.pallas_call(
        _stage_w_in_kernel,
        grid=(depth, D_MODEL // TR_STAGE),
        in_specs=[pl.BlockSpec((1, TR_STAGE, N_IN), lambda l, i: (l, i, 0))],
        out_specs=pl.BlockSpec((1, TR_STAGE, N_PROJ), lambda l, i: (l, i, 0)),
        out_shape=jax.ShapeDtypeStruct((depth, D_MODEL, N_PROJ), BF16),
        compiler_params=_params("parallel", "parallel"),
        name="stage_w_in",
    )(w_in)


def _in_proj_kernel(x_ref, g_ref, w_ref, o_ref):
    x = x_ref[...]
    h = (x * lax.rsqrt(jnp.mean(x * x, axis=-1, keepdims=True) + EPS) * g_ref[...]).astype(BF16)
    for n in range(N_PROJ // TN_IN):
        cols = slice(n * TN_IN, (n + 1) * TN_IN)
        o_ref[:, cols] = _bdot(h, w_ref[:, cols])


def _in_proj(x2d, g, w_staged, layer):
    m = x2d.shape[0]
    tm = min(TM_IN, m)
    return pl.pallas_call(
        _in_proj_kernel,
        grid=(m // tm,),
        in_specs=[pl.BlockSpec((tm, D_MODEL), lambda i: (i, 0)),
                  _const_spec((1, D_MODEL)),
                  pl.BlockSpec((None, D_MODEL, N_PROJ), lambda i: (layer, 0, 0))],
        out_specs=pl.BlockSpec((tm, N_PROJ), lambda i: (i, 0)),
        out_shape=jax.ShapeDtypeStruct((m, N_PROJ), F32),
        compiler_params=_params("parallel"),
        name="in_proj",
    )(x2d, g, w_staged)


PACK = 4
GROUP_W = PACK * HEAD
GROUPS_PER_PAIR = 2 * H_A // PACK
N_TRI_LEVELS = int(math.log2(HEAD))
assert H_A == H_B and 2 * H_A % PACK == 0 and GROUP_W % LANE == 0 and CHUNK == HEAD


def _group_slots(pair_group):
    if pair_group < 2:
        return [(pair_group, h) for h in range(PACK)]
    return [(0, PACK), (0, PACK + 1), (1, PACK), (1, PACK + 1)]


def _to_groups(x, L):
    out = []
    for a in range(0, x.shape[0] // L, 2):
        ra, rb = slice(a * L, (a + 1) * L), slice((a + 1) * L, (a + 2) * L)
        out += [x[ra, 0:GROUP_W], x[rb, 0:GROUP_W],
                jnp.concatenate([x[ra, GROUP_W:W_A], x[rb, GROUP_W:W_A]], axis=1)]
    return out


def _from_groups(vals):
    rows = []
    half = (W_A - GROUP_W)
    for p in range(len(vals) // GROUPS_PER_PAIR):
        ga, gb, gc = vals[3 * p:3 * p + 3]
        rows.append(jnp.concatenate([ga, gc[:, 0:half]], axis=1))
        rows.append(jnp.concatenate([gb, gc[:, half:2 * half]], axis=1))
    return jnp.concatenate(rows, axis=0)


def _keep_diag(m_b, half_ref):
    per_tile = LANE // HEAD
    zero = jnp.zeros((HEAD, LANE), BF16)
    rows = []
    for h in range(PACK):
        t = h // per_tile
        tile = m_b[h * HEAD:(h + 1) * HEAD, t * LANE:(t + 1) * LANE] * half_ref[h % per_tile]
        rows.append(jnp.concatenate([tile if j == t else zero for j in range(GROUP_W // LANE)], axis=1))
    return jnp.concatenate(rows, axis=0)


def _block_diag(x_b, half_ref):
    return _keep_diag(jnp.concatenate([x_b] * PACK, axis=0), half_ref)


def _packed_tri_masks(L):
    r = lax.broadcasted_iota(jnp.int32, (L, GROUP_W), 0)
    c = jnp.bitwise_and(lax.broadcasted_iota(jnp.int32, (L, GROUP_W), 1), HEAD - 1)
    return c < r, c <= r, c == r


def _tri_inv_minus_eye_packed(a_list, mask_b, lvl_ref):
    tp = [-(a * lvl_ref[0]) for a in a_list]
    for lvl in range(1, N_TRI_LEVELS):
        m = lvl_ref[lvl]
        off = [a * m for a in a_list]
        off_bd = [_block_diag(x.astype(BF16), mask_b) for x in off]
        tp_b = [t.astype(BF16) for t in tp]
        m1 = [x + _bdot(tb, bd) for x, tb, bd in zip(off, tp_b, off_bd)]
        tp_bd = [_block_diag(tb, mask_b) for tb in tp_b]
        tp = [t - x - _bdot(x.astype(BF16), bd) for t, x, bd in zip(tp, m1, tp_bd)]
    return tp


def _seg_sum(x, ones_b, exact):
    outs = []
    for j in range(W_A // LANE):
        xs = x[:, j * LANE:(j + 1) * LANE]
        outs.append(_mm_lsplit(xs, ones_b) if exact else _bdot(xs.astype(BF16), ones_b))
    return jnp.concatenate(outs, axis=1)


def _load_state(s_scr, s0_ref):
    s_scr[...] = jnp.zeros(s_scr.shape, F32)
    for gi in range(s_scr.shape[0]):
        for j, (q, h) in enumerate(_group_slots(gi % GROUPS_PER_PAIR)):
            seq = 2 * (gi // GROUPS_PER_PAIR) + q
            s_scr[gi, j * HEAD:(j + 1) * HEAD, j * HEAD:(j + 1) * HEAD] = s0_ref[seq, h]


def _store_state(sout_ref, s_scr):
    for gi in range(s_scr.shape[0]):
        for j, (q, h) in enumerate(_group_slots(gi % GROUPS_PER_PAIR)):
            seq = 2 * (gi // GROUPS_PER_PAIR) + q
            sout_ref[seq, h] = s_scr[gi, j * HEAD:(j + 1) * HEAD, j * HEAD:(j + 1) * HEAD]


def _rwkv_kernel(c_ref, shift0_ref, s0_ref, mu_ref, w0_ref, w2_ref, a0_ref, a2_ref, g2_ref,
                 kk_ref, ka_ref, rk_ref, lnw_ref, lnb_ref, ones_ref, ltri_ref, half_ref, lvl_ref,
                 y_ref, sout_ref, s_scr, prev_scr, *, n_valid):
    G, L = c_ref.shape[0], c_ref.shape[1]
    t = pl.program_id(1)

    @pl.when(t == 0)
    def _():
        _load_state(s_scr, s0_ref)
        prev_scr[...] = shift0_ref[...]

    c = c_ref[...].reshape(G * L, RWKV_PROJ)
    row = jnp.bitwise_and(lax.broadcasted_iota(jnp.int32, c.shape, 0), L - 1)
    prev_rows = _rows_from([prev_scr[g] for g in range(G)], L)
    c_prev = jnp.where(row == 0, prev_rows, pltpu.roll(c, 1, 0))
    for g in range(G):
        prev_scr[g] = c[(g + 1) * L - 1:(g + 1) * L, :]
    cm = c + (c_prev - c) * mu_ref[...]
    r = cm[:, 0:W_A]
    k = cm[:, W_A:2 * W_A]
    v = cm[:, 2 * W_A:3 * W_A]
    o = 3 * W_A
    wd = cm[:, o:o + DECAY_RANK]
    ad = cm[:, o + DECAY_RANK:o + DECAY_RANK + ICLR_RANK]
    gd = cm[:, o + DECAY_RANK + ICLR_RANK:]

    ones_b = ones_ref[...]
    logw = -math.exp(-0.5) * _sigmoid(w0_ref[...] + _mm(jnp.tanh(wd), w2_ref[...]))
    a = _sigmoid(a0_ref[...] + _mm(ad, a2_ref[...]))
    gate = _mm(_sigmoid(gd), g2_ref[...])
    kks = k * kk_ref[...]
    kkn = kks * lax.rsqrt(_seg_sum(kks * kks, ones_b, False) + 1e-6)
    k2 = k * (1.0 + (a - 1.0) * ka_ref[...])
    if n_valid < L:
        live = jnp.bitwise_and(lax.broadcasted_iota(jnp.int32, k.shape, 0), L - 1) < n_valid
        logw = jnp.where(live, logw, 0.0)
        kkn = jnp.where(live, kkn, 0.0)
        k2 = jnp.where(live, k2, 0.0)
    b = kkn * a

    cum = _mm_rsplit(ltri_ref[...], logw)
    w_inc = jnp.exp(cum)
    w_inv = jnp.exp(-cum)
    w_last = _rows_from([w_inc[(g + 1) * L - 1:(g + 1) * L, :] for g in range(G)], L)
    kt = k2 * w_inv
    bt = b * w_inv
    groups = lambda x: _to_groups(x, L)
    rt_g = groups((r * w_inc).astype(BF16))
    at_g = groups((kkn * jnp.exp(cum - logw)).astype(BF16))
    kt_g = groups(kt.astype(BF16))
    bt_g = groups(bt.astype(BF16))
    kl_g = groups((kt * w_last).astype(BF16))
    bl_neg_g = groups((-(bt * w_last)).astype(BF16))
    v_g = groups(v.astype(BF16))
    wl_g = groups(w_last)

    mask_b = half_ref
    strict, incl, _ = _packed_tri_masks(L)
    n_groups = len(v_g)
    s_old = [s_scr[i] for i in range(n_groups)]
    s_old_b = [_keep_diag(s.astype(BF16), mask_b) for s in s_old]
    lhs_ar = [jnp.concatenate([x, y], axis=0) for x, y in zip(at_g, rt_g)]
    bt_bd = [_block_diag(x, mask_b) for x in bt_g]
    kt_bd = [_block_diag(x, mask_b) for x in kt_g]
    v_bd = [_block_diag(x, mask_b) for x in v_g]
    g_b = [_bdot_nt(x, bd) for x, bd in zip(lhs_ar, bt_bd)]
    g_k = [_bdot_nt(x, bd) for x, bd in zip(lhs_ar, kt_bd)]
    a_ab = [jnp.where(strict, x[:L], 0.0) for x in g_b]
    r_b_b = [jnp.where(incl, x[L:], 0.0).astype(BF16) for x in g_b]
    lhs_k = [jnp.concatenate([jnp.where(strict, x[:L], 0.0), jnp.where(incl, x[L:], 0.0)], axis=0).astype(BF16)
             for x in g_k]
    sv = [_bdot_nt(x, s) + _bdot(lk, vbd) for x, s, lk, vbd in zip(lhs_ar, s_old_b, lhs_k, v_bd)]
    tp = _tri_inv_minus_eye_packed(a_ab, mask_b, lvl_ref)
    u = [x[:L] + _bdot(t_.astype(BF16), _block_diag(x[:L].astype(BF16), mask_b))
         for x, t_ in zip(sv, tp)]
    u_b = [x.astype(BF16) for x in u]
    y_g = [x[L:] - _bdot(rb, _block_diag(ub, mask_b)) for x, rb, ub in zip(sv, r_b_b, u_b)]
    for i in range(n_groups):
        upd = _bdot_tn(jnp.concatenate([v_g[i], u_b[i]], axis=0),
                       jnp.concatenate([kl_g[i], bl_neg_g[i]], axis=0))
        s_scr[i] = s_old[i] * wl_g[i][0:1, :] + upd

    y = _from_groups(y_g)
    mean = _seg_sum(y, ones_b, True) * (1.0 / HEAD)
    d = y - mean
    var = _seg_sum(d * d, ones_b, False) * (1.0 / HEAD)
    yn = d * lax.rsqrt(var + GN_EPS) * lnw_ref[...] + lnb_ref[...]
    bonus = _seg_sum(r * k2 * rk_ref[...], ones_b, True) * v
    y_ref[...] = ((yn + bonus) * gate).reshape(G, L, W_A)

    @pl.when(t == pl.num_programs(1) - 1)
    def _():
        _store_state(sout_ref, s_scr)


def _mixer_rows(proj3):
    t_len = proj3.shape[1]
    if t_len >= CHUNK:
        assert t_len % CHUNK == 0
        return proj3, CHUNK
    return jnp.pad(proj3, ((0, 0), (0, CHUNK - t_len), (0, 0))), t_len


def _rwkv_mix(proj3, shift0, s0, p, consts):
    bsz, t_len, _ = proj3.shape
    proj3, n_valid = _mixer_rows(proj3)
    t_pad = proj3.shape[1]
    L, G = CHUNK, MIX_GROUP
    assert bsz % G == 0 and G % 2 == 0
    row = lambda n: _const_spec((1, n))
    y, s_new = pl.pallas_call(
        functools.partial(_rwkv_kernel, n_valid=n_valid),
        grid=(bsz // G, t_pad // L),
        in_specs=[pl.BlockSpec((G, L, RWKV_PROJ), lambda b, t: (b, t, OFF_RWKV // RWKV_PROJ)),
                  pl.BlockSpec((G, 1, RWKV_PROJ), lambda b, t: (b, 0, 0)),
                  pl.BlockSpec((G, H_A, HEAD, HEAD), lambda b, t: (b, 0, 0, 0)),
                  row(RWKV_PROJ), row(W_A), _const_spec((DECAY_RANK, W_A)), row(W_A),
                  _const_spec((ICLR_RANK, W_A)), _const_spec((GATE_RANK, W_A)),
                  row(W_A), row(W_A), row(W_A), row(W_A), row(W_A),
                  _const_spec((LANE, LANE)), _const_spec((G * L, G * L)),
                  _const_spec((LANE // HEAD, HEAD, LANE)), _const_spec((N_TRI_LEVELS, CHUNK, GROUP_W))],
        out_specs=[pl.BlockSpec((G, L, W_A), lambda b, t: (b, t, 0)),
                   pl.BlockSpec((G, H_A, HEAD, HEAD), lambda b, t: (b, 0, 0, 0))],
        out_shape=[jax.ShapeDtypeStruct((bsz, t_pad, W_A), F32),
                   jax.ShapeDtypeStruct((bsz, H_A, HEAD, HEAD), F32)],
        scratch_shapes=[pltpu.VMEM((G // 2 * GROUPS_PER_PAIR, GROUP_W, GROUP_W), F32),
                        pltpu.VMEM((G, 1, RWKV_PROJ), F32)],
        compiler_params=_params("parallel", "arbitrary"),
        name="rwkv_mix",
    )(proj3, shift0, s0, p["rwkv_mu"], p["rwkv_w0"], p["rwkv_w2"], p["rwkv_a0"], p["rwkv_a2"],
      p["rwkv_g2"], p["rwkv_kk"], p["rwkv_ka"], p["rwkv_rk"], p["rwkv_ln_w"], p["rwkv_ln_b"],
      consts["ones"], _block_tril(G, L), consts["half"], consts["tri_levels"])
    return y[:, :t_len], s_new


def _gdn_kernel(qkv_ref, ab_ref, z_ref, conv0_ref, s0_ref, cw_ref, alog_ref, dtb_ref, nw_ref,
                ones_ref, ltri_ref, ea_ref, eb_ref, half_ref, lvl_ref,
                y_ref, sout_ref, s_scr, xpad_scr, *, n_valid):
    G, L = qkv_ref.shape[0], qkv_ref.shape[1]
    t = pl.program_id(1)
    npre = GDN_CONV - 1
    base_row = SUBLANE

    @pl.when(t == 0)
    def _():
        _load_state(s_scr, s0_ref)
        xpad_scr[:, base_row - npre:base_row, :] = conv0_ref[...]

    convs = []
    for g in range(G):
        x = qkv_ref[g]
        xpad_scr[g, base_row:base_row + L, :] = x
        conv = x * cw_ref[npre:npre + 1, :]
        for j in range(npre):
            lo = base_row - npre + j
            conv = conv + xpad_scr[g, lo:lo + L, :] * cw_ref[j:j + 1, :]
        xpad_scr[g, base_row - npre:base_row, :] = xpad_scr[g, base_row + L - npre:base_row + L, :]
        convs.append(conv)
    conv = jnp.concatenate(convs, axis=0)
    qkv = conv * _sigmoid(conv)

    ones_b = ones_ref[...]
    q = qkv[:, 0:W_B]
    k = qkv[:, W_B:2 * W_B]
    v = qkv[:, 2 * W_B:]
    q = q * lax.rsqrt(_seg_sum(q * q, ones_b, False) + 1e-6) * (HEAD ** -0.5)
    k = k * lax.rsqrt(_seg_sum(k * k, ones_b, False) + 1e-6)

    ab = ab_ref[...].reshape(G * L, AB_PAD)
    g_row = -jnp.exp(alog_ref[...]) * _softplus(ab + dtb_ref[...])
    g_exp = _mm_lsplit(g_row, ea_ref[...])
    beta = _mm_lsplit(_sigmoid(ab), eb_ref[...])
    if n_valid < L:
        live = jnp.bitwise_and(lax.broadcasted_iota(jnp.int32, k.shape, 0), L - 1) < n_valid
        k = jnp.where(live, k, 0.0)
        beta = jnp.where(live, beta, 0.0)
        g_exp = jnp.where(live, g_exp, 0.0)
    gc = _mm_rsplit(ltri_ref[...], g_exp)
    g_last = _rows_from([gc[(g + 1) * L - 1:(g + 1) * L, :] for g in range(G)], L)
    e_gc = jnp.exp(gc)
    kbeta = k * beta
    vbeta = v * beta
    kw = kbeta * e_gc
    groups = lambda x: _to_groups(x, L)
    k_g = groups(k.astype(BF16))
    q_g = groups(q.astype(BF16))
    kbeta_g = groups(kbeta.astype(BF16))
    vbeta_g = groups(vbeta)
    kw_g = groups(kw)
    qe_g = groups((q * e_gc).astype(BF16))
    kd_g = groups((k * jnp.exp(g_last - gc)).astype(BF16))
    gc_g = groups(gc)
    el_g = groups(jnp.exp(g_last))

    mask_b = half_ref
    strict, incl, eye = _packed_tri_masks(L)
    n_groups = len(k_g)
    s_old = [s_scr[i] for i in range(n_groups)]
    s_old_b = [_keep_diag(s.astype(BF16), mask_b) for s in s_old]
    gamma = []
    for x in gc_g:
        g_row_j = jnp.sum(jnp.where(eye, x, 0.0), axis=0, keepdims=True)
        gamma.append(jnp.where(incl, jnp.exp(jnp.minimum(x - g_row_j, 0.0)), 0.0))
    kk = [_bdot_nt(jnp.concatenate([x, y], axis=0), _block_diag(kb, mask_b))
          for x, y, kb in zip(kbeta_g, q_g, k_g)]
    lower = [jnp.where(strict, x[:L] * gm, 0.0) for x, gm in zip(kk, gamma)]
    att_b = [(x[L:] * gm).astype(BF16) for x, gm in zip(kk, gamma)]
    tp_b = [x.astype(BF16) for x in _tri_inv_minus_eye_packed(lower, mask_b, lvl_ref)]
    u_pre = [x + _bdot(t_, _block_diag(x.astype(BF16), mask_b)) for x, t_ in zip(vbeta_g, tp_b)]
    w_b = [(x + _bdot(t_, _block_diag(x.astype(BF16), mask_b))).astype(BF16) for x, t_ in zip(kw_g, tp_b)]
    ws = [_bdot(jnp.concatenate([w_, qe], axis=0), s) for w_, qe, s in zip(w_b, qe_g, s_old_b)]
    u_b = [(up - x[:L]).astype(BF16) for up, x in zip(u_pre, ws)]
    o_g = [x[L:] + _bdot(a_, _block_diag(ub, mask_b)) for x, a_, ub in zip(ws, att_b, u_b)]
    for i in range(n_groups):
        s_scr[i] = s_old[i] * el_g[i][0:1, :] + _bdot_tn(kd_g[i], u_b[i])

    o = _from_groups(o_g)
    on = o * lax.rsqrt(_seg_sum(o * o, ones_b, False) * (1.0 / HEAD) + EPS) * nw_ref[...]
    z = z_ref[...].reshape(G * L, W_B)
    y_ref[...] = (on * (z * _sigmoid(z))).reshape(G, L, W_B)

    @pl.when(t == pl.num_programs(1) - 1)
    def _():
        _store_state(sout_ref, s_scr)


def _gdn_mix(proj3, conv0, s0, p, consts):
    bsz, t_len, _ = proj3.shape
    assert t_len >= GDN_CONV - 1
    proj3, n_valid = _mixer_rows(proj3)
    t_pad = proj3.shape[1]
    L, G = CHUNK, MIX_GROUP
    assert bsz % G == 0 and G % 2 == 0
    y, s_new = pl.pallas_call(
        functools.partial(_gdn_kernel, n_valid=n_valid),
        grid=(bsz // G, t_pad // L),
        in_specs=[pl.BlockSpec((G, L, GDN_CH), lambda b, t: (b, t, OFF_QKV // GDN_CH)),
                  pl.BlockSpec((G, L, AB_PAD), lambda b, t: (b, t, OFF_AB // AB_PAD)),
                  pl.BlockSpec((G, L, W_B), lambda b, t: (b, t, OFF_Z // W_B)),
                  pl.BlockSpec((G, GDN_CONV - 1, GDN_CH), lambda b, t: (b, 0, 0)),
                  pl.BlockSpec((G, H_B, HEAD, HEAD), lambda b, t: (b, 0, 0, 0)),
                  _const_spec((GDN_CONV, GDN_CH)), _const_spec((1, AB_PAD)), _const_spec((1, AB_PAD)),
                  _const_spec((1, W_B)), _const_spec((LANE, LANE)), _const_spec((G * L, G * L)),
                  _const_spec((AB_PAD, W_B)), _const_spec((AB_PAD, W_B)),
                  _const_spec((LANE // HEAD, HEAD, LANE)), _const_spec((N_TRI_LEVELS, CHUNK, GROUP_W))],
        out_specs=[pl.BlockSpec((G, L, W_B), lambda b, t: (b, t, 0)),
                   pl.BlockSpec((G, H_B, HEAD, HEAD), lambda b, t: (b, 0, 0, 0))],
        out_shape=[jax.ShapeDtypeStruct((bsz, t_pad, W_B), F32),
                   jax.ShapeDtypeStruct((bsz, H_B, HEAD, HEAD), F32)],
        scratch_shapes=[pltpu.VMEM((G // 2 * GROUPS_PER_PAIR, GROUP_W, GROUP_W), F32),
                        pltpu.VMEM((G, SUBLANE + L, GDN_CH), F32)],
        compiler_params=_params("parallel", "arbitrary"),
        name="gdn_mix",
    )(proj3, proj3, proj3, conv0, s0, p["gdn_conv_w"], p["gdn_A_log"], p["gdn_dt_bias"],
      p["gdn_norm_w"], consts["ones"], _block_tril(G, L), consts["ea"], consts["eb"],
      consts["half"], consts["tri_levels"])
    return y[:, :t_len], s_new


def _attend(q, kwin, vwin, bias_ref, min_col, o_ref):
    q_b = (q * (HEAD ** -0.5)).astype(BF16)
    heads = range(H_C)
    sl = lambda x, h: x[:, h * HEAD:(h + 1) * HEAD]
    s = [_bdot_nt(sl(q_b, h), sl(kwin, h)) + bias_ref[h] for h in heads]
    if min_col is not None:
        col = lax.broadcasted_iota(jnp.int32, s[0].shape, 1)
        s = [jnp.where(col >= min_col, x, -jnp.inf) for x in s]
    e = [jnp.exp(x - jnp.max(x, axis=-1, keepdims=True)) for x in s]
    denom = [jnp.sum(x, axis=-1, keepdims=True) for x in e]
    pv = [_bdot(x.astype(BF16), sl(vwin, h)) for x, h in zip(e, heads)]
    for h in heads:
        o_ref[0, :, h * HEAD:(h + 1) * HEAD] = pv[h] / denom[h]


def _band_prompt_kernel(q_ref, k_ref, v_ref, bias_ref, o_ref, kpad_scr, vpad_scr):
    step = pl.program_id(1)
    n_q = q_ref.shape[1]
    lead = N_BAND_PREV * CHUNK

    @pl.when(step == 0)
    def _():
        zeros = jnp.zeros((lead, W_C), BF16)
        kpad_scr[0:lead, :] = zeros
        vpad_scr[0:lead, :] = zeros
        kpad_scr[lead:, :] = k_ref[0].astype(BF16)
        vpad_scr[lead:, :] = v_ref[0].astype(BF16)

    start = pl.multiple_of(step * n_q, CHUNK)
    kwin = kpad_scr[pl.ds(start, lead + n_q), :]
    vwin = vpad_scr[pl.ds(start, lead + n_q), :]
    _attend(q_ref[0], kwin, vwin, bias_ref, lead - step * n_q, o_ref)


def _band_prompt(proj3, rel_bias):
    bsz, t_len, _ = proj3.shape
    lead = N_BAND_PREV * CHUNK
    n_q = min(BAND_Q_CHUNKS * CHUNK, t_len)
    assert t_len % n_q == 0 and n_q % CHUNK == 0
    n_k = lead + n_q
    first = (np.arange(n_q)[:, None] // CHUNK) * CHUNK
    cols = np.arange(n_k)[None, :]
    in_band = (cols >= first) & (cols < first + lead + CHUNK)
    bias = jnp.where(in_band[None], _rel_bias_tile(rel_bias, n_q, n_k, lead), -jnp.inf)
    cq = OFF_ATT // W_C
    return pl.pallas_call(
        _band_prompt_kernel,
        grid=(bsz, t_len // n_q),
        in_specs=[pl.BlockSpec((1, n_q, W_C), lambda b, c: (b, c, cq)),
                  pl.BlockSpec((1, t_len, W_C), lambda b, c: (b, 0, cq + 1)),
                  pl.BlockSpec((1, t_len, W_C), lambda b, c: (b, 0, cq + 2)),
                  _const_spec((H_C, n_q, n_k))],
        out_specs=pl.BlockSpec((1, n_q, W_C), lambda b, c: (b, c, 0)),
        out_shape=jax.ShapeDtypeStruct((bsz, t_len, W_C), F32),
        scratch_shapes=[pltpu.VMEM((lead + t_len, W_C), BF16),
                        pltpu.VMEM((lead + t_len, W_C), BF16)],
        compiler_params=_params("parallel", "arbitrary"),
        name="band_prompt",
    )(proj3, proj3, proj3, bias)


def _band_step_kernel(q_ref, k_ref, v_ref, bias_ref, o_ref):
    _attend(q_ref[0], k_ref[0].astype(BF16), v_ref[0].astype(BF16), bias_ref, None, o_ref)


def _band_step(proj3, k_all, v_all, bias):
    bsz, t_len, _ = proj3.shape
    n_keys = k_all.shape[1]
    return pl.pallas_call(
        _band_step_kernel,
        grid=(bsz,),
        in_specs=[pl.BlockSpec((1, t_len, W_C), lambda b: (b, 0, OFF_ATT // W_C)),
                  pl.BlockSpec((1, n_keys, W_C), lambda b: (b, 0, 0)),
                  pl.BlockSpec((1, n_keys, W_C), lambda b: (b, 0, 0)),
                  _const_spec((H_C, t_len, n_keys))],
        out_specs=pl.BlockSpec((1, t_len, W_C), lambda b: (b, 0, 0)),
        out_shape=jax.ShapeDtypeStruct((bsz, t_len, W_C), F32),
        compiler_params=_params("parallel"),
        name="band_step",
    )(proj3, k_all, v_all, bias)


def _rel_bias_tile(rel_bias, n_q, n_k, lead):
    n_d = n_q + n_k - 1
    d = np.clip(lead + n_q - 1 - np.arange(n_d), -MAX_REL, MAX_REL) + MAX_REL
    n_hi = int(np.sum(d == 2 * MAX_REL)) - 1 if d[0] == 2 * MAX_REL else 0
    n_lo = int(np.sum(d == 0)) - 1 if d[-1] == 0 else 0
    core = rel_bias[:, int(d[-1 - n_lo]):int(d[n_hi]) + 1][:, ::-1]
    rev = jnp.concatenate([jnp.repeat(rel_bias[:, -1:], n_hi, axis=1), core,
                           jnp.repeat(rel_bias[:, :1], n_lo, axis=1)], axis=1)
    heads = rel_bias.shape[0]
    flat = jnp.broadcast_to(rev[:, None, :], (heads, n_q, n_d)).reshape(heads, n_q * n_d)
    skew = flat[:, n_q - 1:n_q - 1 + n_q * (n_d - 1)].reshape(heads, n_q, n_d - 1)
    return skew[:, :, :n_k]


def _rms(x):
    return x * lax.rsqrt(jnp.mean(x * x, axis=-1, keepdims=True) + EPS)


def _merge_ffn_kernel(x_ref, ya_ref, yb_ref, yc_ref, g0_ref, g1_ref, g2_ref, wa_ref, wb_ref, wc_ref,
                      wo_ref, gmix_ref, gpre_ref, up_ref, down_ref, gpost_ref, o_ref):
    merged = (_sigmoid(g0_ref[...]) * _mm(ya_ref[...], wa_ref[...])
              + _sigmoid(g1_ref[...]) * _mm(yb_ref[...], wb_ref[...])
              + _sigmoid(g2_ref[...]) * _mm(yc_ref[...], wc_ref[...]))
    x1 = x_ref[...] + _rms(_mm(merged, wo_ref[...])) * gmix_ref[...]
    h = (_rms(x1) * gpre_ref[...]).astype(BF16)
    f = jnp.zeros(x1.shape, F32)
    for n in range(D_FF // FF_CHUNK):
        cols = slice(n * FF_CHUNK, (n + 1) * FF_CHUNK)
        act = jnp.maximum(_bdot(h, up_ref[:, cols]), 0.0)
        f = f + _bdot((act * act).astype(BF16), down_ref[cols, :])
    o_ref[...] = x1 + _rms(f) * gpost_ref[...]


def _merge_ffn(x2d, ya, yb, yc, proj2, p):
    m = x2d.shape[0]
    tm = min(TM_FFN, m)
    tok = lambda n, c=0: pl.BlockSpec((tm, n), lambda i, c=c: (i, c))
    held = lambda *shape: pl.BlockSpec(shape, lambda i: (0,) * len(shape), pipeline_mode=pl.Buffered(1))
    return pl.pallas_call(
        _merge_ffn_kernel,
        grid=(m // tm,),
        in_specs=[tok(D_MODEL), tok(W_A), tok(W_B), tok(W_C),
                  tok(D_MODEL, 0), tok(D_MODEL, 1), tok(D_MODEL, 2),
                  held(W_A, D_MODEL), held(W_B, D_MODEL), held(W_C, D_MODEL), held(D_MODEL, D_MODEL),
                  held(1, D_MODEL), held(1, D_MODEL), held(D_MODEL, D_FF), held(D_FF, D_MODEL),
                  held(1, D_MODEL)],
        out_specs=tok(D_MODEL),
        out_shape=jax.ShapeDtypeStruct((m, D_MODEL), F32),
        compiler_params=_params("parallel"),
        name="merge_ffn",
    )(x2d, ya, yb, yc, proj2, proj2, proj2, p["w_br_a"], p["w_br_b"], p["w_br_c"], p["w_out"],
      p["g_post_mix"], p["g_pre_ffn"], p["w_ff_up"], p["w_ff_down"], p["g_post_ffn"])


def _block_tril(groups, n):
    idx = np.arange(groups * n)
    same = (idx[:, None] // n) == (idx[None, :] // n)
    return jnp.asarray((same & (idx[:, None] >= idx[None, :])).astype(np.float32), BF16)


def _constants():
    lane_head = np.arange(LANE) // HEAD
    ones = (lane_head[:, None] == lane_head[None, :]).astype(np.float32)
    half = np.stack([np.broadcast_to(lane_head == p, (HEAD, LANE)) for p in range(LANE // HEAD)]).astype(np.float32)
    ea = np.zeros((AB_PAD, W_B), np.float32)
    eb = np.zeros((AB_PAD, W_B), np.float32)
    for h in range(H_B):
        ea[h, h * HEAD:(h + 1) * HEAD] = 1.0
        eb[H_B + h, h * HEAD:(h + 1) * HEAD] = 1.0
    i = np.arange(CHUNK)[:, None]
    j = (np.arange(GROUP_W) % HEAD)[None, :]
    same = lambda size: (i // size) == (j // size)
    levels = [same(2)] + [same(4 << l) & ~same(2 << l) for l in range(N_TRI_LEVELS - 1)]
    tri_levels = np.stack(levels).astype(np.float32)
    return {"tri_levels": jnp.asarray(tri_levels), "ones": jnp.asarray(ones, BF16), "ea": jnp.asarray(ea, BF16), "eb": jnp.asarray(eb, BF16),
            "half": jnp.asarray(half, BF16)}


def _stage_layer_params(l, w_staged, g_pre_mix, g_post_mix, g_pre_ffn, g_post_ffn, rwkv_mu, rwkv_w0, rwkv_w2,
                        rwkv_a0, rwkv_a2, rwkv_g2, rwkv_kk, rwkv_ka, rwkv_rk, rwkv_ln_w, rwkv_ln_b,
                        gdn_conv_w, gdn_A_log, gdn_dt_bias, gdn_norm_w, att_rel_bias, w_br_a, w_br_b,
                        w_br_c, w_out, w_ff_up, w_ff_down):
    row = lambda a: a[l].reshape(1, -1).astype(F32)
    pad_ab = lambda a: jnp.pad(a[l].astype(F32), (0, AB_PAD - H_B)).reshape(1, AB_PAD)
    return {
        "g_pre_mix": row(g_pre_mix), "g_post_mix": row(g_post_mix),
        "g_pre_ffn": row(g_pre_ffn), "g_post_ffn": row(g_post_ffn),
        "w_in": w_staged, "layer": l,
        "rwkv_mu": row(rwkv_mu), "rwkv_w0": row(rwkv_w0), "rwkv_w2": rwkv_w2[l].astype(BF16),
        "rwkv_a0": row(rwkv_a0), "rwkv_a2": rwkv_a2[l].astype(BF16), "rwkv_g2": rwkv_g2[l].astype(BF16),
        "rwkv_kk": row(rwkv_kk), "rwkv_ka": row(rwkv_ka), "rwkv_rk": row(rwkv_rk),
        "rwkv_ln_w": row(rwkv_ln_w), "rwkv_ln_b": row(rwkv_ln_b),
        "gdn_conv_w": gdn_conv_w[l].astype(F32), "gdn_A_log": pad_ab(gdn_A_log),
        "gdn_dt_bias": pad_ab(gdn_dt_bias),
        "gdn_norm_w": jnp.tile(gdn_norm_w[l].astype(F32), H_B).reshape(1, W_B),
        "att_rel_bias": att_rel_bias[l].astype(F32),
        "w_br_a": w_br_a[l].astype(BF16), "w_br_b": w_br_b[l].astype(BF16), "w_br_c": w_br_c[l].astype(BF16),
        "w_out": w_out[l].astype(BF16), "w_ff_up": w_ff_up[l].astype(BF16),
        "w_ff_down": w_ff_down[l].astype(BF16),
    }


def _trunk_layer(x, p, consts, rwkv_shift, rwkv_s, gdn_conv, gdn_s, band_k, band_v):
    bsz, t_len, _ = x.shape
    m = bsz * t_len
    x2d = x.reshape(m, D_MODEL)
    proj2 = _in_proj(x2d, p["g_pre_mix"], p["w_in"], p["layer"])
    proj3 = proj2.reshape(bsz, t_len, N_PROJ)
    ya, s_a = _rwkv_mix(proj3, rwkv_shift.reshape(bsz, 1, RWKV_PROJ), rwkv_s, p, consts)
    yb, s_b = _gdn_mix(proj3, gdn_conv, gdn_s, p, consts)
    k_new = proj3[:, :, OFF_ATT + W_C:OFF_ATT + 2 * W_C]
    v_new = proj3[:, :, OFF_ATT + 2 * W_C:OFF_ATT + 3 * W_C]
    if band_k is None:
        lead = N_BAND_PREV * CHUNK
        yc = _band_prompt(proj3, p["att_rel_bias"])
        keep = min(lead, t_len)
        k_out, v_out = k_new[:, t_len - keep:], v_new[:, t_len - keep:]
    else:
        n_past = band_k.shape[1]
        k_all = jnp.concatenate([band_k.reshape(bsz, n_past, W_C), k_new], axis=1)
        v_all = jnp.concatenate([band_v.reshape(bsz, n_past, W_C), v_new], axis=1)
        yc = _band_step(proj3, k_all, v_all, _rel_bias_tile(p["att_rel_bias"], t_len, n_past + t_len, n_past))
        k_out, v_out = k_new, v_new
    x2 = _merge_ffn(x2d, ya.reshape(m, W_A), yb.reshape(m, W_B), yc.reshape(m, W_C), proj2, p)
    shift_new = proj3[:, t_len - 1, OFF_RWKV:OFF_RWKV + RWKV_PROJ]
    conv_new = proj3[:, t_len - (GDN_CONV - 1):, OFF_QKV:OFF_QKV + GDN_CH]
    new_state = (shift_new, s_a, conv_new, s_b,
                 k_out.reshape(bsz, -1, H_C, HEAD), v_out.reshape(bsz, -1, H_C, HEAD))
    return x2.reshape(bsz, t_len, D_MODEL), new_state


def kernel(x_prompt, x_sample, state_rwkv_shift, state_rwkv_wkv, state_gdn_conv, state_gdn_S, cache_band_k, cache_band_v, g_pre_mix, g_post_mix, g_pre_ffn, g_post_ffn, w_in, rwkv_mu, rwkv_w0, rwkv_w2, rwkv_a0, rwkv_a2, rwkv_g2, rwkv_kk, rwkv_ka, rwkv_rk, rwkv_ln_w, rwkv_ln_b, gdn_conv_w, gdn_A_log, gdn_dt_bias, gdn_norm_w, att_rel_bias, w_br_a, w_br_b, w_br_c, w_out, w_ff_up, w_ff_down):
    depth = w_in.shape[0]
    bp = x_prompt.shape[0]
    consts = _constants()
    w_staged = _stage_w_in(w_in)
    xp, xs = x_prompt, x_sample
    p_new = [[] for _ in range(6)]
    s_new = [[] for _ in range(6)]
    for l in range(depth):
        p = _stage_layer_params(l, w_staged, g_pre_mix, g_post_mix, g_pre_ffn, g_post_ffn, rwkv_mu, rwkv_w0,
                                rwkv_w2, rwkv_a0, rwkv_a2, rwkv_g2, rwkv_kk, rwkv_ka, rwkv_rk, rwkv_ln_w,
                                rwkv_ln_b, gdn_conv_w, gdn_A_log, gdn_dt_bias, gdn_norm_w, att_rel_bias,
                                w_br_a, w_br_b, w_br_c, w_out, w_ff_up, w_ff_down)
        xp, st_p = _trunk_layer(
            xp, p, consts,
            jnp.zeros((bp, RWKV_PROJ), F32), jnp.zeros((bp, H_A, HEAD, HEAD), F32),
            jnp.zeros((bp, GDN_CONV - 1, GDN_CH), F32), jnp.zeros((bp, H_B, HEAD, HEAD), F32),
            None, None)
        xs, st_s = _trunk_layer(
            xs, p, consts, state_rwkv_shift[l], state_rwkv_wkv[l], state_gdn_conv[l], state_gdn_S[l],
            cache_band_k[l], cache_band_v[l])
        for i in range(6):
            p_new[i].append(st_p[i])
            s_new[i].append(st_s[i])
    stk_p = [jnp.stack(a).astype(x_prompt.dtype) for a in p_new]
    stk_s = [jnp.stack(a).astype(x_sample.dtype) for a in s_new]
    return (xp, xs, *stk_p, *stk_s)
```

```python
import functools
import math

import numpy as np
import jax
import jax.numpy as jnp
from jax import lax
from jax.experimental import pallas as pl
from jax.experimental.pallas import tpu as pltpu

F32 = jnp.float32
BF16 = jnp.bfloat16

D_MODEL = 1024
HEAD = 64
H_A = 6
W_A = H_A * HEAD
DECAY_RANK = 64
ICLR_RANK = 64
GATE_RANK = 128
RWKV_PROJ = 3 * W_A + DECAY_RANK + ICLR_RANK + GATE_RANK
H_B = 6
W_B = H_B * HEAD
GDN_CONV = 4
GDN_CH = 3 * W_B
H_C = 4
W_C = H_C * HEAD
N_BAND_PREV = 8
MAX_REL = 128
CHUNK = 64
D_FF = 4 * D_MODEL
EPS = 1e-6
GN_EPS = 64e-5

LANE = 128
SUBLANE = 8
AB_PAD = LANE
OFF_GATE = 0
OFF_ATT = OFF_GATE + 3 * D_MODEL
OFF_Z = OFF_ATT + 3 * W_C
OFF_RWKV = OFF_Z + W_B
OFF_AB = OFF_RWKV + RWKV_PROJ
OFF_QKV = OFF_AB + AB_PAD
N_PROJ = OFF_QKV + GDN_CH
assert OFF_ATT % W_C == 0 and OFF_Z % W_B == 0 and OFF_RWKV % RWKV_PROJ == 0
assert OFF_AB % AB_PAD == 0 and OFF_QKV % GDN_CH == 0 and N_PROJ % LANE == 0

VMEM_LIMIT = 56 * 1024 * 1024
TM_IN = 512
TN_IN = 768
TM_FFN = 512
FF_CHUNK = 1024
MIX_GROUP = 8
BAND_Q_CHUNKS = 4


def _params(*sem):
    return pltpu.CompilerParams(dimension_semantics=sem, vmem_limit_bytes=VMEM_LIMIT)


def _const_spec(shape):
    nd = len(shape)
    return pl.BlockSpec(shape, lambda *_: (0,) * nd)


def _bdot(a, b):
    return jnp.dot(a, b, preferred_element_type=F32)


def _bdot_nt(a, b):
    return lax.dot_general(a, b, (((1,), (1,)), ((), ())), preferred_element_type=F32)


def _bdot_tn(a, b):
    return lax.dot_general(a, b, (((0,), (0,)), ((), ())), preferred_element_type=F32)


def _mm(a, b):
    return _bdot(a.astype(BF16), b.astype(BF16))


def _split(a):
    hi = a.astype(BF16)
    lo = (a - hi.astype(F32)).astype(BF16)
    return hi, lo


def _mm_lsplit(a, b_exact):
    hi, lo = _split(a)
    return _bdot(hi, b_exact) + _bdot(lo, b_exact)


def _mm_rsplit(a_exact, b):
    hi, lo = _split(b)
    return _bdot(a_exact, hi) + _bdot(a_exact, lo)


def _sigmoid(x):
    return 0.5 * jnp.tanh(0.5 * x) + 0.5


def _softplus(x):
    return jnp.maximum(x, 0.0) + jnp.log(1.0 + jnp.exp(-jnp.abs(x)))


def _rows_from(ref_rows, n):
    return jnp.concatenate([jnp.broadcast_to(r, (n, r.shape[-1])) for r in ref_rows], axis=0)


N_IN = RWKV_PROJ + GDN_CH + 2 * H_B + W_B + 3 * W_C + 3 * D_MODEL
SRC_QKV = RWKV_PROJ
SRC_AB = SRC_QKV + GDN_CH
SRC_Z = SRC_AB + 2 * H_B
TR_STAGE = 128
assert SRC_QKV % LANE == 0 and SRC_AB % LANE == 0


def _stage_w_in_kernel(w_ref, o_ref):
    w = w_ref[0]
    tail = w[:, SRC_Z:N_IN].astype(BF16)
    o_ref[0, :, OFF_GATE:OFF_GATE + 3 * D_MODEL] = tail[:, W_B + 3 * W_C:]
    o_ref[0, :, OFF_ATT:OFF_ATT + 3 * W_C] = tail[:, W_B:W_B + 3 * W_C]
    o_ref[0, :, OFF_Z:OFF_Z + W_B] = tail[:, 0:W_B]
    o_ref[0, :, OFF_RWKV:OFF_RWKV + RWKV_PROJ] = w[:, 0:RWKV_PROJ].astype(BF16)
    o_ref[0, :, OFF_AB:OFF_AB + AB_PAD] = jnp.zeros((w.shape[0], AB_PAD), BF16)
    o_ref[0, :, OFF_AB:OFF_AB + 2 * H_B] = w[:, SRC_AB:SRC_AB + 2 * H_B].astype(BF16)
    o_ref[0, :, OFF_QKV:OFF_QKV + GDN_CH] = w[:, SRC_QKV:SRC_QKV + GDN_CH].astype(BF16)


def _stage_w_in(w_in):
    depth, d_model, n_in = w_in.shape
    assert d_model == D_MODEL and n_in == N_IN
    return pl.pallas_call(
        _stage_w_in_kernel,
        grid=(depth, D_MODEL // TR_STAGE),
        in_specs=[pl.BlockSpec((1, TR_STAGE, N_IN), lambda l, i: (l, i, 0))],
        out_specs=pl.BlockSpec((1, TR_STAGE, N_PROJ), lambda l, i: (l, i, 0)),
        out_shape=jax.ShapeDtypeStruct((depth, D_MODEL, N_PROJ), BF16),
        compiler_params=_params("parallel", "parallel"),
        name="stage_w_in",
    )(w_in)


def _in_proj_kernel(x_ref, g_ref, w_ref, o_ref):
    x = x_ref[...]
    h = (x * lax.rsqrt(jnp.mean(x * x, axis=-1, keepdims=True) + EPS) * g_ref[...]).astype(BF16)
    for n in range(N_PROJ // TN_IN):
        cols = slice(n * TN_IN, (n + 1) * TN_IN)
        o_ref[:, cols] = _bdot(h, w_ref[:, cols])


def _in_proj(x2d, g, w_staged, layer):
    m = x2d.shape[0]
    tm = min(TM_IN, m)
    return pl.pallas_call(
        _in_proj_kernel,
        grid=(m // tm,),
        in_specs=[pl.BlockSpec((tm, D_MODEL), lambda i: (i, 0)),
                  _const_spec((1, D_MODEL)),
                  pl.BlockSpec((None, D_MODEL, N_PROJ), lambda i: (layer, 0, 0), pipeline_mode=pl.Buffered(1))],
        out_specs=pl.BlockSpec((tm, N_PROJ), lambda i: (i, 0)),
        out_shape=jax.ShapeDtypeStruct((m, N_PROJ), F32),
        compiler_params=_params("parallel"),
        name="in_proj",
    )(x2d, g, w_staged)


PACK = 4
GROUP_W = PACK * HEAD
GROUPS_PER_PAIR = 2 * H_A // PACK
N_TRI_LEVELS = int(math.log2(HEAD))
assert H_A == H_B and 2 * H_A % PACK == 0 and GROUP_W % LANE == 0 and CHUNK == HEAD


def _group_slots(pair_group):
    if pair_group < 2:
        return [(pair_group, h) for h in range(PACK)]
    return [(0, PACK), (0, PACK + 1), (1, PACK), (1, PACK + 1)]


def _to_groups(x, L):
    out = []
    for a in range(0, x.shape[0] // L, 2):
        ra, rb = slice(a * L, (a + 1) * L), slice((a + 1) * L, (a + 2) * L)
        out += [x[ra, 0:GROUP_W], x[rb, 0:GROUP_W],
                jnp.concatenate([x[ra, GROUP_W:W_A], x[rb, GROUP_W:W_A]], axis=1)]
    return out


def _from_groups(vals):
    rows = []
    half = (W_A - GROUP_W)
    for p in range(len(vals) // GROUPS_PER_PAIR):
        ga, gb, gc = vals[3 * p:3 * p + 3]
        rows.append(jnp.concatenate([ga, gc[:, 0:half]], axis=1))
        rows.append(jnp.concatenate([gb, gc[:, half:2 * half]], axis=1))
    return jnp.concatenate(rows, axis=0)


def _keep_diag(m_b, half_ref):
    per_tile = LANE // HEAD
    zero = jnp.zeros((HEAD, LANE), BF16)
    rows = []
    for h in range(PACK):
        t = h // per_tile
        tile = m_b[h * HEAD:(h + 1) * HEAD, t * LANE:(t + 1) * LANE] * half_ref[h % per_tile]
        rows.append(jnp.concatenate([tile if j == t else zero for j in range(GROUP_W // LANE)], axis=1))
    return jnp.concatenate(rows, axis=0)


def _block_diag(x_b, half_ref):
    return _keep_diag(jnp.concatenate([x_b] * PACK, axis=0), half_ref)


def _packed_tri_masks(L):
    r = lax.broadcasted_iota(jnp.int32, (L, GROUP_W), 0)
    c = jnp.bitwise_and(lax.broadcasted_iota(jnp.int32, (L, GROUP_W), 1), HEAD - 1)
    return c < r, c <= r, c == r


def _tri_inv_minus_eye_packed(a_list, mask_b, lvl_ref):
    tp = [-(a * lvl_ref[0]) for a in a_list]
    for lvl in range(1, N_TRI_LEVELS):
        m = lvl_ref[lvl]
        off = [a * m for a in a_list]
        off_bd = [_block_diag(x.astype(BF16), mask_b) for x in off]
        tp_b = [t.astype(BF16) for t in tp]
        m1 = [x + _bdot(tb, bd) for x, tb, bd in zip(off, tp_b, off_bd)]
        tp_bd = [_block_diag(tb, mask_b) for tb in tp_b]
        tp = [t - x - _bdot(x.astype(BF16), bd) for t, x, bd in zip(tp, m1, tp_bd)]
    return tp


def _seg_sum(x, ones_b, exact):
    outs = []
    for j in range(W_A // LANE):
        xs = x[:, j * LANE:(j + 1) * LANE]
        outs.append(_mm_lsplit(xs, ones_b) if exact else _bdot(xs.astype(BF16), ones_b))
    return jnp.concatenate(outs, axis=1)


def _load_state(s_scr, s0_ref):
    s_scr[...] = jnp.zeros(s_scr.shape, F32)
    for gi in range(s_scr.shape[0]):
        for j, (q, h) in enumerate(_group_slots(gi % GROUPS_PER_PAIR)):
            seq = 2 * (gi // GROUPS_PER_PAIR) + q
            s_scr[gi, j * HEAD:(j + 1) * HEAD, j * HEAD:(j + 1) * HEAD] = s0_ref[seq, h]


def _store_state(sout_ref, s_scr):
    for gi in range(s_scr.shape[0]):
        for j, (q, h) in enumerate(_group_slots(gi % GROUPS_PER_PAIR)):
            seq = 2 * (gi // GROUPS_PER_PAIR) + q
            sout_ref[seq, h] = s_scr[gi, j * HEAD:(j + 1) * HEAD, j * HEAD:(j + 1) * HEAD]


def _rwkv_kernel(c_ref, shift0_ref, s0_ref, mu_ref, w0_ref, w2_ref, a0_ref, a2_ref, g2_ref,
                 kk_ref, ka_ref, rk_ref, lnw_ref, lnb_ref, ones_ref, ltri_ref, half_ref, lvl_ref,
                 y_ref, sout_ref, s_scr, prev_scr, *, n_valid):
    G, L = c_ref.shape[0], c_ref.shape[1]
    t = pl.program_id(1)

    @pl.when(t == 0)
    def _():
        _load_state(s_scr, s0_ref)
        prev_scr[...] = shift0_ref[...]

    c = c_ref[...].reshape(G * L, RWKV_PROJ)
    row = jnp.bitwise_and(lax.broadcasted_iota(jnp.int32, c.shape, 0), L - 1)
    prev_rows = _rows_from([prev_scr[g] for g in range(G)], L)
    c_prev = jnp.where(row == 0, prev_rows, pltpu.roll(c, 1, 0))
    for g in range(G):
        prev_scr[g] = c[(g + 1) * L - 1:(g + 1) * L, :]
    cm = c + (c_prev - c) * mu_ref[...]
    r = cm[:, 0:W_A]
    k = cm[:, W_A:2 * W_A]
    v = cm[:, 2 * W_A:3 * W_A]
    o = 3 * W_A
    wd = cm[:, o:o + DECAY_RANK]
    ad = cm[:, o + DECAY_RANK:o + DECAY_RANK + ICLR_RANK]
    gd = cm[:, o + DECAY_RANK + ICLR_RANK:]

    ones_b = ones_ref[...]
    logw = -math.exp(-0.5) * _sigmoid(w0_ref[...] + _mm(jnp.tanh(wd), w2_ref[...]))
    a = _sigmoid(a0_ref[...] + _mm(ad, a2_ref[...]))
    gate = _mm(_sigmoid(gd), g2_ref[...])
    kks = k * kk_ref[...]
    kkn = kks * lax.rsqrt(_seg_sum(kks * kks, ones_b, False) + 1e-6)
    k2 = k * (1.0 + (a - 1.0) * ka_ref[...])
    if n_valid < L:
        live = jnp.bitwise_and(lax.broadcasted_iota(jnp.int32, k.shape, 0), L - 1) < n_valid
        logw = jnp.where(live, logw, 0.0)
        kkn = jnp.where(live, kkn, 0.0)
        k2 = jnp.where(live, k2, 0.0)
    b = kkn * a

    cum = _mm_rsplit(ltri_ref[...], logw)
    w_inc = jnp.exp(cum)
    w_inv = jnp.exp(-cum)
    w_last = _rows_from([w_inc[(g + 1) * L - 1:(g + 1) * L, :] for g in range(G)], L)
    kt = k2 * w_inv
    bt = b * w_inv
    groups = lambda x: _to_groups(x, L)
    rt_g = groups((r * w_inc).astype(BF16))
    at_g = groups((kkn * jnp.exp(cum - logw)).astype(BF16))
    kt_g = groups(kt.astype(BF16))
    bt_g = groups(bt.astype(BF16))
    kl_g = groups((kt * w_last).astype(BF16))
    bl_neg_g = groups((-(bt * w_last)).astype(BF16))
    v_g = groups(v.astype(BF16))
    wl_g = groups(w_last)

    mask_b = half_ref
    strict, incl, _ = _packed_tri_masks(L)
    n_groups = len(v_g)
    s_old = [s_scr[i] for i in range(n_groups)]
    s_old_b = [_keep_diag(s.astype(BF16), mask_b) for s in s_old]
    lhs_ar = [jnp.concatenate([x, y], axis=0) for x, y in zip(at_g, rt_g)]
    bt_bd = [_block_diag(x, mask_b) for x in bt_g]
    kt_bd = [_block_diag(x, mask_b) for x in kt_g]
    v_bd = [_block_diag(x, mask_b) for x in v_g]
    g_b = [_bdot_nt(x, bd) for x, bd in zip(lhs_ar, bt_bd)]
    g_k = [_bdot_nt(x, bd) for x, bd in zip(lhs_ar, kt_bd)]
    a_ab = [jnp.where(strict, x[:L], 0.0) for x in g_b]
    r_b_b = [jnp.where(incl, x[L:], 0.0).astype(BF16) for x in g_b]
    lhs_k = [jnp.concatenate([jnp.where(strict, x[:L], 0.0), jnp.where(incl, x[L:], 0.0)], axis=0).astype(BF16)
             for x in g_k]
    sv = [_bdot_nt(x, s) + _bdot(lk, vbd) for x, s, lk, vbd in zip(lhs_ar, s_old_b, lhs_k, v_bd)]
    tp = _tri_inv_minus_eye_packed(a_ab, mask_b, lvl_ref)
    u = [x[:L] + _bdot(t_.astype(BF16), _block_diag(x[:L].astype(BF16), mask_b))
         for x, t_ in zip(sv, tp)]
    u_b = [x.astype(BF16) for x in u]
    y_g = [x[L:] - _bdot(rb, _block_diag(ub, mask_b)) for x, rb, ub in zip(sv, r_b_b, u_b)]
    for i in range(n_groups):
        upd = _bdot_tn(jnp.concatenate([v_g[i], u_b[i]], axis=0),
                       jnp.concatenate([kl_g[i], bl_neg_g[i]], axis=0))
        s_scr[i] = s_old[i] * wl_g[i][0:1, :] + upd

    y = _from_groups(y_g)
    mean = _seg_sum(y, ones_b, True) * (1.0 / HEAD)
    d = y - mean
    var = _seg_sum(d * d, ones_b, False) * (1.0 / HEAD)
    yn = d * lax.rsqrt(var + GN_EPS) * lnw_ref[...] + lnb_ref[...]
    bonus = _seg_sum(r * k2 * rk_ref[...], ones_b, True) * v
    y_ref[...] = ((yn + bonus) * gate).reshape(G, L, W_A)

    @pl.when(t == pl.num_programs(1) - 1)
    def _():
        _store_state(sout_ref, s_scr)


def _mixer_rows(proj3):
    t_len = proj3.shape[1]
    if t_len >= CHUNK:
        assert t_len % CHUNK == 0
        return proj3, CHUNK
    return jnp.pad(proj3, ((0, 0), (0, CHUNK - t_len), (0, 0))), t_len


def _rwkv_mix(proj3, shift0, s0, p, consts):
    bsz, t_len, _ = proj3.shape
    proj3, n_valid = _mixer_rows(proj3)
    t_pad = proj3.shape[1]
    L, G = CHUNK, MIX_GROUP
    assert bsz % G == 0 and G % 2 == 0
    row = lambda n: _const_spec((1, n))
    y, s_new = pl.pallas_call(
        functools.partial(_rwkv_kernel, n_valid=n_valid),
        grid=(bsz // G, t_pad // L),
        in_specs=[pl.BlockSpec((G, L, RWKV_PROJ), lambda b, t: (b, t, OFF_RWKV // RWKV_PROJ)),
                  pl.BlockSpec((G, 1, RWKV_PROJ), lambda b, t: (b, 0, 0)),
                  pl.BlockSpec((G, H_A, HEAD, HEAD), lambda b, t: (b, 0, 0, 0)),
                  row(RWKV_PROJ), row(W_A), _const_spec((DECAY_RANK, W_A)), row(W_A),
                  _const_spec((ICLR_RANK, W_A)), _const_spec((GATE_RANK, W_A)),
                  row(W_A), row(W_A), row(W_A), row(W_A), row(W_A),
                  _const_spec((LANE, LANE)), _const_spec((G * L, G * L)),
                  _const_spec((LANE // HEAD, HEAD, LANE)), _const_spec((N_TRI_LEVELS, CHUNK, GROUP_W))],
        out_specs=[pl.BlockSpec((G, L, W_A), lambda b, t: (b, t, 0)),
                   pl.BlockSpec((G, H_A, HEAD, HEAD), lambda b, t: (b, 0, 0, 0))],
        out_shape=[jax.ShapeDtypeStruct((bsz, t_pad, W_A), F32),
                   jax.ShapeDtypeStruct((bsz, H_A, HEAD, HEAD), F32)],
        scratch_shapes=[pltpu.VMEM((G // 2 * GROUPS_PER_PAIR, GROUP_W, GROUP_W), F32),
                        pltpu.VMEM((G, 1, RWKV_PROJ), F32)],
        compiler_params=_params("parallel", "arbitrary"),
        name="rwkv_mix",
    )(proj3, shift0, s0, p["rwkv_mu"], p["rwkv_w0"], p["rwkv_w2"], p["rwkv_a0"], p["rwkv_a2"],
      p["rwkv_g2"], p["rwkv_kk"], p["rwkv_ka"], p["rwkv_rk"], p["rwkv_ln_w"], p["rwkv_ln_b"],
      consts["ones"], _block_tril(G, L), consts["half"], consts["tri_levels"])
    return y[:, :t_len], s_new


def _gdn_kernel(qkv_ref, ab_ref, z_ref, conv0_ref, s0_ref, cw_ref, alog_ref, dtb_ref, nw_ref,
                ones_ref, ltri_ref, ea_ref, eb_ref, half_ref, lvl_ref,
                y_ref, sout_ref, s_scr, xpad_scr, *, n_valid):
    G, L = qkv_ref.shape[0], qkv_ref.shape[1]
    t = pl.program_id(1)
    npre = GDN_CONV - 1
    base_row = SUBLANE

    @pl.when(t == 0)
    def _():
        _load_state(s_scr, s0_ref)
        xpad_scr[:, base_row - npre:base_row, :] = conv0_ref[...]

    convs = []
    for g in range(G):
        x = qkv_ref[g]
        xpad_scr[g, base_row:base_row + L, :] = x
        conv = x * cw_ref[npre:npre + 1, :]
        for j in range(npre):
            lo = base_row - npre + j
            conv = conv + xpad_scr[g, lo:lo + L, :] * cw_ref[j:j + 1, :]
        xpad_scr[g, base_row - npre:base_row, :] = xpad_scr[g, base_row + L - npre:base_row + L, :]
        convs.append(conv)
    conv = jnp.concatenate(convs, axis=0)
    qkv = conv * _sigmoid(conv)

    ones_b = ones_ref[...]
    q = qkv[:, 0:W_B]
    k = qkv[:, W_B:2 * W_B]
    v = qkv[:, 2 * W_B:]
    q = q * lax.rsqrt(_seg_sum(q * q, ones_b, False) + 1e-6) * (HEAD ** -0.5)
    k = k * lax.rsqrt(_seg_sum(k * k, ones_b, False) + 1e-6)

    ab = ab_ref[...].reshape(G * L, AB_PAD)
    g_row = -jnp.exp(alog_ref[...]) * _softplus(ab + dtb_ref[...])
    g_exp = _mm_lsplit(g_row, ea_ref[...])
    beta = _mm_lsplit(_sigmoid(ab), eb_ref[...])
    if n_valid < L:
        live = jnp.bitwise_and(lax.broadcasted_iota(jnp.int32, k.shape, 0), L - 1) < n_valid
        k = jnp.where(live, k, 0.0)
        beta = jnp.where(live, beta, 0.0)
        g_exp = jnp.where(live, g_exp, 0.0)
    gc = _mm_rsplit(ltri_ref[...], g_exp)
    g_last = _rows_from([gc[(g + 1) * L - 1:(g + 1) * L, :] for g in range(G)], L)
    e_gc = jnp.exp(gc)
    kbeta = k * beta
    vbeta = v * beta
    kw = kbeta * e_gc
    groups = lambda x: _to_groups(x, L)
    k_g = groups(k.astype(BF16))
    q_g = groups(q.astype(BF16))
    kbeta_g = groups(kbeta.astype(BF16))
    vbeta_g = groups(vbeta)
    kw_g = groups(kw)
    qe_g = groups((q * e_gc).astype(BF16))
    kd_g = groups((k * jnp.exp(g_last - gc)).astype(BF16))
    gc_g = groups(gc)
    el_g = groups(jnp.exp(g_last))

    mask_b = half_ref
    strict, incl, eye = _packed_tri_masks(L)
    n_groups = len(k_g)
    s_old = [s_scr[i] for i in range(n_groups)]
    s_old_b = [_keep_diag(s.astype(BF16), mask_b) for s in s_old]
    gamma = []
    for x in gc_g:
        g_row_j = jnp.sum(jnp.where(eye, x, 0.0), axis=0, keepdims=True)
        gamma.append(jnp.where(incl, jnp.exp(jnp.minimum(x - g_row_j, 0.0)), 0.0))
    kk = [_bdot_nt(jnp.concatenate([x, y], axis=0), _block_diag(kb, mask_b))
          for x, y, kb in zip(kbeta_g, q_g, k_g)]
    lower = [jnp.where(strict, x[:L] * gm, 0.0) for x, gm in zip(kk, gamma)]
    att_b = [(x[L:] * gm).astype(BF16) for x, gm in zip(kk, gamma)]
    tp_b = [x.astype(BF16) for x in _tri_inv_minus_eye_packed(lower, mask_b, lvl_ref)]
    u_pre = [x + _bdot(t_, _block_diag(x.astype(BF16), mask_b)) for x, t_ in zip(vbeta_g, tp_b)]
    w_b = [(x + _bdot(t_, _block_diag(x.astype(BF16), mask_b))).astype(BF16) for x, t_ in zip(kw_g, tp_b)]
    ws = [_bdot(jnp.concatenate([w_, qe], axis=0), s) for w_, qe, s in zip(w_b, qe_g, s_old_b)]
    u_b = [(up - x[:L]).astype(BF16) for up, x in zip(u_pre, ws)]
    o_g = [x[L:] + _bdot(a_, _block_diag(ub, mask_b)) for x, a_, ub in zip(ws, att_b, u_b)]
    for i in range(n_groups):
        s_scr[i] = s_old[i] * el_g[i][0:1, :] + _bdot_tn(kd_g[i], u_b[i])

    o = _from_groups(o_g)
    on = o * lax.rsqrt(_seg_sum(o * o, ones_b, False) * (1.0 / HEAD) + EPS) * nw_ref[...]
    z = z_ref[...].reshape(G * L, W_B)
    y_ref[...] = (on * (z * _sigmoid(z))).reshape(G, L, W_B)

    @pl.when(t == pl.num_programs(1) - 1)
    def _():
        _store_state(sout_ref, s_scr)


def _gdn_mix(proj3, conv0, s0, p, consts):
    bsz, t_len, _ = proj3.shape
    assert t_len >= GDN_CONV - 1
    proj3, n_valid = _mixer_rows(proj3)
    t_pad = proj3.shape[1]
    L, G = CHUNK, MIX_GROUP
    assert bsz % G == 0 and G % 2 == 0
    y, s_new = pl.pallas_call(
        functools.partial(_gdn_kernel, n_valid=n_valid),
        grid=(bsz // G, t_pad // L),
        in_specs=[pl.BlockSpec((G, L, GDN_CH), lambda b, t: (b, t, OFF_QKV // GDN_CH)),
                  pl.BlockSpec((G, L, AB_PAD), lambda b, t: (b, t, OFF_AB // AB_PAD)),
                  pl.BlockSpec((G, L, W_B), lambda b, t: (b, t, OFF_Z // W_B)),
                  pl.BlockSpec((G, GDN_CONV - 1, GDN_CH), lambda b, t: (b, 0, 0)),
                  pl.BlockSpec((G, H_B, HEAD, HEAD), lambda b, t: (b, 0, 0, 0)),
                  _const_spec((GDN_CONV, GDN_CH)), _const_spec((1, AB_PAD)), _const_spec((1, AB_PAD)),
                  _const_spec((1, W_B)), _const_spec((LANE, LANE)), _const_spec((G * L, G * L)),
                  _const_spec((AB_PAD, W_B)), _const_spec((AB_PAD, W_B)),
                  _const_spec((LANE // HEAD, HEAD, LANE)), _const_spec((N_TRI_LEVELS, CHUNK, GROUP_W))],
        out_specs=[pl.BlockSpec((G, L, W_B), lambda b, t: (b, t, 0)),
                   pl.BlockSpec((G, H_B, HEAD, HEAD), lambda b, t: (b, 0, 0, 0))],
        out_shape=[jax.ShapeDtypeStruct((bsz, t_pad, W_B), F32),
                   jax.ShapeDtypeStruct((bsz, H_B, HEAD, HEAD), F32)],
        scratch_shapes=[pltpu.VMEM((G // 2 * GROUPS_PER_PAIR, GROUP_W, GROUP_W), F32),
                        pltpu.VMEM((G, SUBLANE + L, GDN_CH), F32)],
        compiler_params=_params("parallel", "arbitrary"),
        name="gdn_mix",
    )(proj3, proj3, proj3, conv0, s0, p["gdn_conv_w"], p["gdn_A_log"], p["gdn_dt_bias"],
      p["gdn_norm_w"], consts["ones"], _block_tril(G, L), consts["ea"], consts["eb"],
      consts["half"], consts["tri_levels"])
    return y[:, :t_len], s_new


def _attend(q, kwin, vwin, bias_ref, min_col, o_ref):
    q_b = (q * (HEAD ** -0.5)).astype(BF16)
    heads = range(H_C)
    sl = lambda x, h: x[:, h * HEAD:(h + 1) * HEAD]
    s = [_bdot_nt(sl(q_b, h), sl(kwin, h)) + bias_ref[h] for h in heads]
    if min_col is not None:
        col = lax.broadcasted_iota(jnp.int32, s[0].shape, 1)
        s = [jnp.where(col >= min_col, x, -jnp.inf) for x in s]
    e = [jnp.exp(x - jnp.max(x, axis=-1, keepdims=True)) for x in s]
    denom = [jnp.sum(x, axis=-1, keepdims=True) for x in e]
    pv = [_bdot(x.astype(BF16), sl(vwin, h)) for x, h in zip(e, heads)]
    for h in heads:
        o_ref[0, :, h * HEAD:(h + 1) * HEAD] = pv[h] / denom[h]


def _band_prompt_kernel(q_ref, k_ref, v_ref, bias_ref, o_ref, kpad_scr, vpad_scr):
    step = pl.program_id(1)
    n_q = q_ref.shape[1]
    lead = N_BAND_PREV * CHUNK

    @pl.when(step == 0)
    def _():
        zeros = jnp.zeros((lead, W_C), BF16)
        kpad_scr[0:lead, :] = zeros
        vpad_scr[0:lead, :] = zeros
        kpad_scr[lead:, :] = k_ref[0].astype(BF16)
        vpad_scr[lead:, :] = v_ref[0].astype(BF16)

    start = pl.multiple_of(step * n_q, CHUNK)
    kwin = kpad_scr[pl.ds(start, lead + n_q), :]
    vwin = vpad_scr[pl.ds(start, lead + n_q), :]
    _attend(q_ref[0], kwin, vwin, bias_ref, lead - step * n_q, o_ref)


def _band_prompt(proj3, rel_bias):
    bsz, t_len, _ = proj3.shape
    lead = N_BAND_PREV * CHUNK
    n_q = min(BAND_Q_CHUNKS * CHUNK, t_len)
    assert t_len % n_q == 0 and n_q % CHUNK == 0
    n_k = lead + n_q
    first = (np.arange(n_q)[:, None] // CHUNK) * CHUNK
    cols = np.arange(n_k)[None, :]
    in_band = (cols >= first) & (cols < first + lead + CHUNK)
    bias = jnp.where(in_band[None], _rel_bias_tile(rel_bias, n_q, n_k, lead), -jnp.inf)
    cq = OFF_ATT // W_C
    return pl.pallas_call(
        _band_prompt_kernel,
        grid=(bsz, t_len // n_q),
        in_specs=[pl.BlockSpec((1, n_q, W_C), lambda b, c: (b, c, cq)),
                  pl.BlockSpec((1, t_len, W_C), lambda b, c: (b, 0, cq + 1)),
                  pl.BlockSpec((1, t_len, W_C), lambda b, c: (b, 0, cq + 2)),
                  _const_spec((H_C, n_q, n_k))],
        out_specs=pl.BlockSpec((1, n_q, W_C), lambda b, c: (b, c, 0)),
        out_shape=jax.ShapeDtypeStruct((bsz, t_len, W_C), F32),
        scratch_shapes=[pltpu.VMEM((lead + t_len, W_C), BF16),
                        pltpu.VMEM((lead + t_len, W_C), BF16)],
        compiler_params=_params("parallel", "arbitrary"),
        name="band_prompt",
    )(proj3, proj3, proj3, bias)


def _band_step_kernel(q_ref, k_ref, v_ref, bias_ref, o_ref):
    _attend(q_ref[0], k_ref[0].astype(BF16), v_ref[0].astype(BF16), bias_ref, None, o_ref)


def _band_step(proj3, k_all, v_all, bias):
    bsz, t_len, _ = proj3.shape
    n_keys = k_all.shape[1]
    return pl.pallas_call(
        _band_step_kernel,
        grid=(bsz,),
        in_specs=[pl.BlockSpec((1, t_len, W_C), lambda b: (b, 0, OFF_ATT // W_C)),
                  pl.BlockSpec((1, n_keys, W_C), lambda b: (b, 0, 0)),
                  pl.BlockSpec((1, n_keys, W_C), lambda b: (b, 0, 0)),
                  _const_spec((H_C, t_len, n_keys))],
        out_specs=pl.BlockSpec((1, t_len, W_C), lambda b: (b, 0, 0)),
        out_shape=jax.ShapeDtypeStruct((bsz, t_len, W_C), F32),
        compiler_params=_params("parallel"),
        name="band_step",
    )(proj3, k_all, v_all, bias)


def _rel_bias_tile(rel_bias, n_q, n_k, lead):
    n_d = n_q + n_k - 1
    d = np.clip(lead + n_q - 1 - np.arange(n_d), -MAX_REL, MAX_REL) + MAX_REL
    n_hi = int(np.sum(d == 2 * MAX_REL)) - 1 if d[0] == 2 * MAX_REL else 0
    n_lo = int(np.sum(d == 0)) - 1 if d[-1] == 0 else 0
    core = rel_bias[:, int(d[-1 - n_lo]):int(d[n_hi]) + 1][:, ::-1]
    rev = jnp.concatenate([jnp.repeat(rel_bias[:, -1:], n_hi, axis=1), core,
                           jnp.repeat(rel_bias[:, :1], n_lo, axis=1)], axis=1)
    heads = rel_bias.shape[0]
    flat = jnp.broadcast_to(rev[:, None, :], (heads, n_q, n_d)).reshape(heads, n_q * n_d)
    skew = flat[:, n_q - 1:n_q - 1 + n_q * (n_d - 1)].reshape(heads, n_q, n_d - 1)
    return skew[:, :, :n_k]


def _rms(x):
    return x * lax.rsqrt(jnp.mean(x * x, axis=-1, keepdims=True) + EPS)


def _merge_ffn_kernel(x_ref, ya_ref, yb_ref, yc_ref, g0_ref, g1_ref, g2_ref, wa_ref, wb_ref, wc_ref,
                      wo_ref, gmix_ref, gpre_ref, up_ref, down_ref, gpost_ref, o_ref):
    merged = (_sigmoid(g0_ref[...]) * _mm(ya_ref[...], wa_ref[...])
              + _sigmoid(g1_ref[...]) * _mm(yb_ref[...], wb_ref[...])
              + _sigmoid(g2_ref[...]) * _mm(yc_ref[...], wc_ref[...]))
    x1 = x_ref[...] + _rms(_mm(merged, wo_ref[...])) * gmix_ref[...]
    h = (_rms(x1) * gpre_ref[...]).astype(BF16)
    f = jnp.zeros(x1.shape, F32)
    for n in range(D_FF // FF_CHUNK):
        cols = slice(n * FF_CHUNK, (n + 1) * FF_CHUNK)
        act = jnp.maximum(_bdot(h, up_ref[:, cols]), 0.0)
        f = f + _bdot((act * act).astype(BF16), down_ref[cols, :])
    o_ref[...] = x1 + _rms(f) * gpost_ref[...]


def _merge_ffn(x2d, ya, yb, yc, proj2, p):
    m = x2d.shape[0]
    tm = min(TM_FFN, m)
    tok = lambda n, c=0: pl.BlockSpec((tm, n), lambda i, c=c: (i, c))
    held = lambda *shape: pl.BlockSpec(shape, lambda i: (0,) * len(shape), pipeline_mode=pl.Buffered(1))
    return pl.pallas_call(
        _merge_ffn_kernel,
        grid=(m // tm,),
        in_specs=[tok(D_MODEL), tok(W_A), tok(W_B), tok(W_C),
                  tok(D_MODEL, 0), tok(D_MODEL, 1), tok(D_MODEL, 2),
                  held(W_A, D_MODEL), held(W_B, D_MODEL), held(W_C, D_MODEL), held(D_MODEL, D_MODEL),
                  held(1, D_MODEL), held(1, D_MODEL), held(D_MODEL, D_FF), held(D_FF, D_MODEL),
                  held(1, D_MODEL)],
        out_specs=tok(D_MODEL),
        out_shape=jax.ShapeDtypeStruct((m, D_MODEL), F32),
        compiler_params=_params("parallel"),
        name="merge_ffn",
    )(x2d, ya, yb, yc, proj2, proj2, proj2, p["w_br_a"], p["w_br_b"], p["w_br_c"], p["w_out"],
      p["g_post_mix"], p["g_pre_ffn"], p["w_ff_up"], p["w_ff_down"], p["g_post_ffn"])


def _block_tril(groups, n):
    idx = np.arange(groups * n)
    same = (idx[:, None] // n) == (idx[None, :] // n)
    return jnp.asarray((same & (idx[:, None] >= idx[None, :])).astype(np.float32), BF16)


def _constants():
    lane_head = np.arange(LANE) // HEAD
    ones = (lane_head[:, None] == lane_head[None, :]).astype(np.float32)
    half = np.stack([np.broadcast_to(lane_head == p, (HEAD, LANE)) for p in range(LANE // HEAD)]).astype(np.float32)
    ea = np.zeros((AB_PAD, W_B), np.float32)
    eb = np.zeros((AB_PAD, W_B), np.float32)
    for h in range(H_B):
        ea[h, h * HEAD:(h + 1) * HEAD] = 1.0
        eb[H_B + h, h * HEAD:(h + 1) * HEAD] = 1.0
    i = np.arange(CHUNK)[:, None]
    j = (np.arange(GROUP_W) % HEAD)[None, :]
    same = lambda size: (i // size) == (j // size)
    levels = [same(2)] + [same(4 << l) & ~same(2 << l) for l in range(N_TRI_LEVELS - 1)]
    tri_levels = np.stack(levels).astype(np.float32)
    return {"tri_levels": jnp.asarray(tri_levels), "ones": jnp.asarray(ones, BF16), "ea": jnp.asarray(ea, BF16), "eb": jnp.asarray(eb, BF16),
            "half": jnp.asarray(half, BF16)}


def _stage_layer_params(l, w_staged, g_pre_mix, g_post_mix, g_pre_ffn, g_post_ffn, rwkv_mu, rwkv_w0, rwkv_w2,
                        rwkv_a0, rwkv_a2, rwkv_g2, rwkv_kk, rwkv_ka, rwkv_rk, rwkv_ln_w, rwkv_ln_b,
                        gdn_conv_w, gdn_A_log, gdn_dt_bias, gdn_norm_w, att_rel_bias, w_br_a, w_br_b,
                        w_br_c, w_out, w_ff_up, w_ff_down):
    row = lambda a: a[l].reshape(1, -1).astype(F32)
    pad_ab = lambda a: jnp.pad(a[l].astype(F32), (0, AB_PAD - H_B)).reshape(1, AB_PAD)
    return {
        "g_pre_mix": row(g_pre_mix), "g_post_mix": row(g_post_mix),
        "g_pre_ffn": row(g_pre_ffn), "g_post_ffn": row(g_post_ffn),
        "w_in": w_staged, "layer": l,
        "rwkv_mu": row(rwkv_mu), "rwkv_w0": row(rwkv_w0), "rwkv_w2": rwkv_w2[l].astype(BF16),
        "rwkv_a0": row(rwkv_a0), "rwkv_a2": rwkv_a2[l].astype(BF16), "rwkv_g2": rwkv_g2[l].astype(BF16),
        "rwkv_kk": row(rwkv_kk), "rwkv_ka": row(rwkv_ka), "rwkv_rk": row(rwkv_rk),
        "rwkv_ln_w": row(rwkv_ln_w), "rwkv_ln_b": row(rwkv_ln_b),
        "gdn_conv_w": gdn_conv_w[l].astype(F32), "gdn_A_log": pad_ab(gdn_A_log),
        "gdn_dt_bias": pad_ab(gdn_dt_bias),
        "gdn_norm_w": jnp.tile(gdn_norm_w[l].astype(F32), H_B).reshape(1, W_B),
        "att_rel_bias": att_rel_bias[l].astype(F32),
        "w_br_a": w_br_a[l].astype(BF16), "w_br_b": w_br_b[l].astype(BF16), "w_br_c": w_br_c[l].astype(BF16),
        "w_out": w_out[l].astype(BF16), "w_ff_up": w_ff_up[l].astype(BF16),
        "w_ff_down": w_ff_down[l].astype(BF16),
    }


def _trunk_layer(x, p, consts, rwkv_shift, rwkv_s, gdn_conv, gdn_s, band_k, band_v):
    bsz, t_len, _ = x.shape
    m = bsz * t_len
    x2d = x.reshape(m, D_MODEL)
    proj2 = _in_proj(x2d, p["g_pre_mix"], p["w_in"], p["layer"])
    proj3 = proj2.reshape(bsz, t_len, N_PROJ)
    ya, s_a = _rwkv_mix(proj3, rwkv_shift.reshape(bsz, 1, RWKV_PROJ), rwkv_s, p, consts)
    yb, s_b = _gdn_mix(proj3, gdn_conv, gdn_s, p, consts)
    k_new = proj3[:, :, OFF_ATT + W_C:OFF_ATT + 2 * W_C]
    v_new = proj3[:, :, OFF_ATT + 2 * W_C:OFF_ATT + 3 * W_C]
    if band_k is None:
        lead = N_BAND_PREV * CHUNK
        yc = _band_prompt(proj3, p["att_rel_bias"])
        keep = min(lead, t_len)
        k_out, v_out = k_new[:, t_len - keep:], v_new[:, t_len - keep:]
    else:
        n_past = band_k.shape[1]
        k_all = jnp.concatenate([band_k.reshape(bsz, n_past, W_C), k_new], axis=1)
        v_all = jnp.concatenate([band_v.reshape(bsz, n_past, W_C), v_new], axis=1)
        yc = _band_step(proj3, k_all, v_all, _rel_bias_tile(p["att_rel_bias"], t_len, n_past + t_len, n_past))
        k_out, v_out = k_new, v_new
    x2 = _merge_ffn(x2d, ya.reshape(m, W_A), yb.reshape(m, W_B), yc.reshape(m, W_C), proj2, p)
    shift_new = proj3[:, t_len - 1, OFF_RWKV:OFF_RWKV + RWKV_PROJ]
    conv_new = proj3[:, t_len - (GDN_CONV - 1):, OFF_QKV:OFF_QKV + GDN_CH]
    new_state = (shift_new, s_a, conv_new, s_b,
                 k_out.reshape(bsz, -1, H_C, HEAD), v_out.reshape(bsz, -1, H_C, HEAD))
    return x2.reshape(bsz, t_len, D_MODEL), new_state


def kernel(x_prompt, x_sample, state_rwkv_shift, state_rwkv_wkv, state_gdn_conv, state_gdn_S, cache_band_k, cache_band_v, g_pre_mix, g_post_mix, g_pre_ffn, g_post_ffn, w_in, rwkv_mu, rwkv_w0, rwkv_w2, rwkv_a0, rwkv_a2, rwkv_g2, rwkv_kk, rwkv_ka, rwkv_rk, rwkv_ln_w, rwkv_ln_b, gdn_conv_w, gdn_A_log, gdn_dt_bias, gdn_norm_w, att_rel_bias, w_br_a, w_br_b, w_br_c, w_out, w_ff_up, w_ff_down):
    depth = w_in.shape[0]
    bp = x_prompt.shape[0]
    consts = _constants()
    w_staged = _stage_w_in(w_in)
    xp, xs = x_prompt, x_sample
    p_new = [[] for _ in range(6)]
    s_new = [[] for _ in range(6)]
    for l in range(depth):
        p = _stage_layer_params(l, w_staged, g_pre_mix, g_post_mix, g_pre_ffn, g_post_ffn, rwkv_mu, rwkv_w0,
                                rwkv_w2, rwkv_a0, rwkv_a2, rwkv_g2, rwkv_kk, rwkv_ka, rwkv_rk, rwkv_ln_w,
                                rwkv_ln_b, gdn_conv_w, gdn_A_log, gdn_dt_bias, gdn_norm_w, att_rel_bias,
                                w_br_a, w_br_b, w_br_c, w_out, w_ff_up, w_ff_down)
        xp, st_p = _trunk_layer(
            xp, p, consts,
            jnp.zeros((bp, RWKV_PROJ), F32), jnp.zeros((bp, H_A, HEAD, HEAD), F32),
            jnp.zeros((bp, GDN_CONV - 1, GDN_CH), F32), jnp.zeros((bp, H_B, HEAD, HEAD), F32),
            None, None)
        xs, st_s = _trunk_layer(
            xs, p, consts, state_rwkv_shift[l], state_rwkv_wkv[l], state_gdn_conv[l], state_gdn_S[l],
            cache_band_k[l], cache_band_v[l])
        for i in range(6):
            p_new[i].append(st_p[i])
            s_new[i].append(st_s[i])
    stk_p = [jnp.stack(a).astype(x_prompt.dtype) for a in p_new]
    stk_s = [jnp.stack(a).astype(x_sample.dtype) for a in s_new]
    return (xp, xs, *stk_p, *stk_s)
```

```python
import functools
import math

import numpy as np
import jax
import jax.numpy as jnp
from jax import lax
from jax.experimental import pallas as pl
from jax.experimental.pallas import tpu as pltpu

F32 = jnp.float32
BF16 = jnp.bfloat16

D_MODEL = 1024
HEAD = 64
H_A = 6
W_A = H_A * HEAD
DECAY_RANK = 64
ICLR_RANK = 64
GATE_RANK = 128
RWKV_PROJ = 3 * W_A + DECAY_RANK + ICLR_RANK + GATE_RANK
H_B = 6
W_B = H_B * HEAD
GDN_CONV = 4
GDN_CH = 3 * W_B
H_C = 4
W_C = H_C * HEAD
N_BAND_PREV = 8
MAX_REL = 128
CHUNK = 64
D_FF = 4 * D_MODEL
EPS = 1e-6
GN_EPS = 64e-5

LANE = 128
SUBLANE = 8
AB_PAD = LANE
OFF_GATE = 0
OFF_ATT = OFF_GATE + 3 * D_MODEL
OFF_Z = OFF_ATT + 3 * W_C
OFF_RWKV = OFF_Z + W_B
OFF_AB = OFF_RWKV + RWKV_PROJ
OFF_QKV = OFF_AB + AB_PAD
N_PROJ = OFF_QKV + GDN_CH
assert OFF_ATT % W_C == 0 and OFF_Z % W_B == 0 and OFF_RWKV % RWKV_PROJ == 0
assert OFF_AB % AB_PAD == 0 and OFF_QKV % GDN_CH == 0 and N_PROJ % LANE == 0

VMEM_LIMIT = 56 * 1024 * 1024
TM_IN = 512
TN_IN = 768
TM_FFN = 512
FF_CHUNK = 1024
MIX_GROUP = 8
BAND_Q_CHUNKS = 4


def _params(*sem):
    return pltpu.CompilerParams(dimension_semantics=sem, vmem_limit_bytes=VMEM_LIMIT)


def _const_spec(shape):
    nd = len(shape)
    return pl.BlockSpec(shape, lambda *_: (0,) * nd)


def _bdot(a, b):
    return jnp.dot(a, b, preferred_element_type=F32)


def _bdot_nt(a, b):
    return lax.dot_general(a, b, (((1,), (1,)), ((), ())), preferred_element_type=F32)


def _bdot_tn(a, b):
    return lax.dot_general(a, b, (((0,), (0,)), ((), ())), preferred_element_type=F32)


def _mm(a, b):
    return _bdot(a.astype(BF16), b.astype(BF16))


def _split(a):
    hi = a.astype(BF16)
    lo = (a - hi.astype(F32)).astype(BF16)
    return hi, lo


def _mm_lsplit(a, b_exact):
    hi, lo = _split(a)
    return _bdot(hi, b_exact) + _bdot(lo, b_exact)


def _mm_rsplit(a_exact, b):
    hi, lo = _split(b)
    return _bdot(a_exact, hi) + _bdot(a_exact, lo)


def _sigmoid(x):
    return 0.5 * jnp.tanh(0.5 * x) + 0.5


def _softplus(x):
    return jnp.maximum(x, 0.0) + jnp.log(1.0 + jnp.exp(-jnp.abs(x)))


def _rows_from(ref_rows, n):
    return jnp.concatenate([jnp.broadcast_to(r, (n, r.shape[-1])) for r in ref_rows], axis=0)


N_IN = RWKV_PROJ + GDN_CH + 2 * H_B + W_B + 3 * W_C + 3 * D_MODEL
SRC_QKV = RWKV_PROJ
SRC_AB = SRC_QKV + GDN_CH
SRC_Z = SRC_AB + 2 * H_B
TR_STAGE = 128
assert SRC_QKV % LANE == 0 and SRC_AB % LANE == 0


def _stage_w_in_kernel(w_ref, o_ref):
    w = w_ref[0]
    tail = w[:, SRC_Z:N_IN].astype(BF16)
    o_ref[0, :, OFF_GATE:OFF_GATE + 3 * D_MODEL] = tail[:, W_B + 3 * W_C:]
    o_ref[0, :, OFF_ATT:OFF_ATT + 3 * W_C] = tail[:, W_B:W_B + 3 * W_C]
    o_ref[0, :, OFF_Z:OFF_Z + W_B] = tail[:, 0:W_B]
    o_ref[0, :, OFF_RWKV:OFF_RWKV + RWKV_PROJ] = w[:, 0:RWKV_PROJ].astype(BF16)
    o_ref[0, :, OFF_AB:OFF_AB + AB_PAD] = jnp.zeros((w.shape[0], AB_PAD), BF16)
    o_ref[0, :, OFF_AB:OFF_AB + 2 * H_B] = w[:, SRC_AB:SRC_AB + 2 * H_B].astype(BF16)
    o_ref[0, :, OFF_QKV:OFF_QKV + GDN_CH] = w[:, SRC_QKV:SRC_QKV + GDN_CH].astype(BF16)


def _stage_w_in(w_in):
    depth, d_model, n_in = w_in.shape
    assert d_model == D_MODEL and n_in == N_IN
    return pl.pallas_call(
        _stage_w_in_kernel,
        grid=(depth, D_MODEL // TR_STAGE),
        in_specs=[pl.BlockSpec((1, TR_STAGE, N_IN), lambda l, i: (l, i, 0))],
        out_specs=pl.BlockSpec((1, TR_STAGE, N_PROJ), lambda l, i: (l, i, 0)),
        out_shape=jax.ShapeDtypeStruct((depth, D_MODEL, N_PROJ), BF16),
        compiler_params=_params("parallel", "parallel"),
        name="stage_w_in",
    )(w_in)


def _in_proj_kernel(x_ref, g_ref, w_ref, o_ref):
    x = x_ref[...]
    h = (x * lax.rsqrt(jnp.mean(x * x, axis=-1, keepdims=True) + EPS) * g_ref[...]).astype(BF16)
    for n in range(N_PROJ // TN_IN):
        cols = slice(n * TN_IN, (n + 1) * TN_IN)
        o_ref[:, cols] = _bdot(h, w_ref[:, cols])


def _in_proj(x2d, g, w_staged, layer):
    m = x2d.shape[0]
    tm = min(TM_IN, m)
    return pl.pallas_call(
        _in_proj_kernel,
        grid=(m // tm,),
        in_specs=[pl.BlockSpec((tm, D_MODEL), lambda i: (i, 0)),
                  _const_spec((1, D_MODEL)),
                  pl.BlockSpec((None, D_MODEL, N_PROJ), lambda i: (layer, 0, 0), pipeline_mode=pl.Buffered(1))],
        out_specs=pl.BlockSpec((tm, N_PROJ), lambda i: (i, 0)),
        out_shape=jax.ShapeDtypeStruct((m, N_PROJ), F32),
        compiler_params=_params("parallel"),
        name="in_proj",
    )(x2d, g, w_staged)


PACK = 4
GROUP_W = PACK * HEAD
GROUPS_PER_PAIR = 2 * H_A // PACK
N_TRI_LEVELS = int(math.log2(HEAD))
assert H_A == H_B and 2 * H_A % PACK == 0 and GROUP_W % LANE == 0 and CHUNK == HEAD


def _group_slots(pair_group):
    if pair_group < 2:
        return [(pair_group, h) for h in range(PACK)]
    return [(0, PACK), (0, PACK + 1), (1, PACK), (1, PACK + 1)]


def _to_groups(x, L):
    out = []
    for a in range(0, x.shape[0] // L, 2):
        ra, rb = slice(a * L, (a + 1) * L), slice((a + 1) * L, (a + 2) * L)
        out += [x[ra, 0:GROUP_W], x[rb, 0:GROUP_W],
                jnp.concatenate([x[ra, GROUP_W:W_A], x[rb, GROUP_W:W_A]], axis=1)]
    return out


def _from_groups(vals):
    rows = []
    half = (W_A - GROUP_W)
    for p in range(len(vals) // GROUPS_PER_PAIR):
        ga, gb, gc = vals[3 * p:3 * p + 3]
        rows.append(jnp.concatenate([ga, gc[:, 0:half]], axis=1))
        rows.append(jnp.concatenate([gb, gc[:, half:2 * half]], axis=1))
    return jnp.concatenate(rows, axis=0)


def _keep_diag(m_b, half_ref):
    per_tile = LANE // HEAD
    zero = jnp.zeros((HEAD, LANE), BF16)
    rows = []
    for h in range(PACK):
        t = h // per_tile
        tile = m_b[h * HEAD:(h + 1) * HEAD, t * LANE:(t + 1) * LANE] * half_ref[h % per_tile]
        rows.append(jnp.concatenate([tile if j == t else zero for j in range(GROUP_W // LANE)], axis=1))
    return jnp.concatenate(rows, axis=0)


def _block_diag(x_b, half_ref):
    return _keep_diag(jnp.concatenate([x_b] * PACK, axis=0), half_ref)


def _packed_tri_masks(L):
    r = lax.broadcasted_iota(jnp.int32, (L, GROUP_W), 0)
    c = jnp.bitwise_and(lax.broadcasted_iota(jnp.int32, (L, GROUP_W), 1), HEAD - 1)
    return c < r, c <= r, c == r


def _tri_inv_minus_eye_packed(a_list, mask_b, lvl_ref):
    tp = [-(a * lvl_ref[0]) for a in a_list]
    for lvl in range(1, N_TRI_LEVELS):
        m = lvl_ref[lvl]
        off = [a * m for a in a_list]
        off_bd = [_block_diag(x.astype(BF16), mask_b) for x in off]
        tp_b = [t.astype(BF16) for t in tp]
        m1 = [x + _bdot(tb, bd) for x, tb, bd in zip(off, tp_b, off_bd)]
        tp_bd = [_block_diag(tb, mask_b) for tb in tp_b]
        tp = [t - x - _bdot(x.astype(BF16), bd) for t, x, bd in zip(tp, m1, tp_bd)]
    return tp


def _seg_sum(x, ones_b, exact):
    outs = []
    for j in range(W_A // LANE):
        xs = x[:, j * LANE:(j + 1) * LANE]
        outs.append(_mm_lsplit(xs, ones_b) if exact else _bdot(xs.astype(BF16), ones_b))
    return jnp.concatenate(outs, axis=1)


def _load_state(s_scr, s0_ref):
    s_scr[...] = jnp.zeros(s_scr.shape, F32)
    for gi in range(s_scr.shape[0]):
        for j, (q, h) in enumerate(_group_slots(gi % GROUPS_PER_PAIR)):
            seq = 2 * (gi // GROUPS_PER_PAIR) + q
            s_scr[gi, j * HEAD:(j + 1) * HEAD, j * HEAD:(j + 1) * HEAD] = s0_ref[seq, h]


def _store_state(sout_ref, s_scr):
    for gi in range(s_scr.shape[0]):
        for j, (q, h) in enumerate(_group_slots(gi % GROUPS_PER_PAIR)):
            seq = 2 * (gi // GROUPS_PER_PAIR) + q
            sout_ref[seq, h] = s_scr[gi, j * HEAD:(j + 1) * HEAD, j * HEAD:(j + 1) * HEAD]


def _rwkv_kernel(c_ref, shift0_ref, s0_ref, mu_ref, w0_ref, w2_ref, a0_ref, a2_ref, g2_ref,
                 kk_ref, ka_ref, rk_ref, lnw_ref, lnb_ref, ones_ref, ltri_ref, half_ref, lvl_ref,
                 y_ref, sout_ref, s_scr, prev_scr, *, n_valid):
    G, L = c_ref.shape[0], c_ref.shape[1]
    t = pl.program_id(1)

    @pl.when(t == 0)
    def _():
        _load_state(s_scr, s0_ref)
        prev_scr[...] = shift0_ref[...]

    c = c_ref[...].reshape(G * L, RWKV_PROJ)
    row = jnp.bitwise_and(lax.broadcasted_iota(jnp.int32, c.shape, 0), L - 1)
    prev_rows = _rows_from([prev_scr[g] for g in range(G)], L)
    c_prev = jnp.where(row == 0, prev_rows, pltpu.roll(c, 1, 0))
    for g in range(G):
        prev_scr[g] = c[(g + 1) * L - 1:(g + 1) * L, :]
    cm = c + (c_prev - c) * mu_ref[...]
    r = cm[:, 0:W_A]
    k = cm[:, W_A:2 * W_A]
    v = cm[:, 2 * W_A:3 * W_A]
    o = 3 * W_A
    wd = cm[:, o:o + DECAY_RANK]
    ad = cm[:, o + DECAY_RANK:o + DECAY_RANK + ICLR_RANK]
    gd = cm[:, o + DECAY_RANK + ICLR_RANK:]

    ones_b = ones_ref[...]
    logw = -math.exp(-0.5) * _sigmoid(w0_ref[...] + _mm(jnp.tanh(wd), w2_ref[...]))
    a = _sigmoid(a0_ref[...] + _mm(ad, a2_ref[...]))
    gate = _mm(_sigmoid(gd), g2_ref[...])
    kks = k * kk_ref[...]
    kkn = kks * lax.rsqrt(_seg_sum(kks * kks, ones_b, False) + 1e-6)
    k2 = k * (1.0 + (a - 1.0) * ka_ref[...])
    if n_valid < L:
        live = jnp.bitwise_and(lax.broadcasted_iota(jnp.int32, k.shape, 0), L - 1) < n_valid
        logw = jnp.where(live, logw, 0.0)
        kkn = jnp.where(live, kkn, 0.0)
        k2 = jnp.where(live, k2, 0.0)
    b = kkn * a

    cum = _mm_rsplit(ltri_ref[...], logw)
    w_inc = jnp.exp(cum)
    w_inv = jnp.exp(-cum)
    w_last = _rows_from([w_inc[(g + 1) * L - 1:(g + 1) * L, :] for g in range(G)], L)
    kt = k2 * w_inv
    bt = b * w_inv
    groups = lambda x: _to_groups(x, L)
    rt_g = groups((r * w_inc).astype(BF16))
    at_g = groups((kkn * jnp.exp(cum - logw)).astype(BF16))
    kt_g = groups(kt.astype(BF16))
    bt_g = groups(bt.astype(BF16))
    kl_g = groups((kt * w_last).astype(BF16))
    bl_neg_g = groups((-(bt * w_last)).astype(BF16))
    v_g = groups(v.astype(BF16))
    wl_g = groups(w_last)

    mask_b = half_ref
    strict, incl, _ = _packed_tri_masks(L)
    n_groups = len(v_g)
    s_old = [s_scr[i] for i in range(n_groups)]
    s_old_b = [_keep_diag(s.astype(BF16), mask_b) for s in s_old]
    lhs_ar = [jnp.concatenate([x, y], axis=0) for x, y in zip(at_g, rt_g)]
    bt_bd = [_block_diag(x, mask_b) for x in bt_g]
    kt_bd = [_block_diag(x, mask_b) for x in kt_g]
    v_bd = [_block_diag(x, mask_b) for x in v_g]
    g_b = [_bdot_nt(x, bd) for x, bd in zip(lhs_ar, bt_bd)]
    g_k = [_bdot_nt(x, bd) for x, bd in zip(lhs_ar, kt_bd)]
    a_ab = [jnp.where(strict, x[:L], 0.0) for x in g_b]
    r_b_b = [jnp.where(incl, x[L:], 0.0).astype(BF16) for x in g_b]
    lhs_k = [jnp.concatenate([jnp.where(strict, x[:L], 0.0), jnp.where(incl, x[L:], 0.0)], axis=0).astype(BF16)
             for x in g_k]
    sv = [_bdot_nt(x, s) + _bdot(lk, vbd) for x, s, lk, vbd in zip(lhs_ar, s_old_b, lhs_k, v_bd)]
    tp = _tri_inv_minus_eye_packed(a_ab, mask_b, lvl_ref)
    u = [x[:L] + _bdot(t_.astype(BF16), _block_diag(x[:L].astype(BF16), mask_b))
         for x, t_ in zip(sv, tp)]
    u_b = [x.astype(BF16) for x in u]
    y_g = [x[L:] - _bdot(rb, _block_diag(ub, mask_b)) for x, rb, ub in zip(sv, r_b_b, u_b)]
    for i in range(n_groups):
        upd = _bdot_tn(jnp.concatenate([v_g[i], u_b[i]], axis=0),
                       jnp.concatenate([kl_g[i], bl_neg_g[i]], axis=0))
        s_scr[i] = s_old[i] * wl_g[i][0:1, :] + upd

    y = _from_groups(y_g)
    mean = _seg_sum(y, ones_b, True) * (1.0 / HEAD)
    d = y - mean
    var = _seg_sum(d * d, ones_b, False) * (1.0 / HEAD)
    yn = d * lax.rsqrt(var + GN_EPS) * lnw_ref[...] + lnb_ref[...]
    bonus = _seg_sum(r * k2 * rk_ref[...], ones_b, True) * v
    y_ref[...] = ((yn + bonus) * gate).reshape(G, L, W_A)

    @pl.when(t == pl.num_programs(1) - 1)
    def _():
        _store_state(sout_ref, s_scr)


def _mixer_rows(proj3):
    t_len = proj3.shape[1]
    if t_len >= CHUNK:
        assert t_len % CHUNK == 0
        return proj3, CHUNK
    return jnp.pad(proj3, ((0, 0), (0, CHUNK - t_len), (0, 0))), t_len


def _rwkv_mix(proj3, shift0, s0, p, consts):
    bsz, t_len, _ = proj3.shape
    proj3, n_valid = _mixer_rows(proj3)
    t_pad = proj3.shape[1]
    L, G = CHUNK, MIX_GROUP
    assert bsz % G == 0 and G % 2 == 0
    row = lambda n: _const_spec((1, n))
    y, s_new = pl.pallas_call(
        functools.partial(_rwkv_kernel, n_valid=n_valid),
        grid=(bsz // G, t_pad // L),
        in_specs=[pl.BlockSpec((G, L, RWKV_PROJ), lambda b, t: (b, t, OFF_RWKV // RWKV_PROJ)),
                  pl.BlockSpec((G, 1, RWKV_PROJ), lambda b, t: (b, 0, 0)),
                  pl.BlockSpec((G, H_A, HEAD, HEAD), lambda b, t: (b, 0, 0, 0)),
                  row(RWKV_PROJ), row(W_A), _const_spec((DECAY_RANK, W_A)), row(W_A),
                  _const_spec((ICLR_RANK, W_A)), _const_spec((GATE_RANK, W_A)),
                  row(W_A), row(W_A), row(W_A), row(W_A), row(W_A),
                  _const_spec((LANE, LANE)), _const_spec((G * L, G * L)),
                  _const_spec((LANE // HEAD, HEAD, LANE)), _const_spec((N_TRI_LEVELS, CHUNK, GROUP_W))],
        out_specs=[pl.BlockSpec((G, L, W_A), lambda b, t: (b, t, 0)),
                   pl.BlockSpec((G, H_A, HEAD, HEAD), lambda b, t: (b, 0, 0, 0))],
        out_shape=[jax.ShapeDtypeStruct((bsz, t_pad, W_A), F32),
                   jax.ShapeDtypeStruct((bsz, H_A, HEAD, HEAD), F32)],
        scratch_shapes=[pltpu.VMEM((G // 2 * GROUPS_PER_PAIR, GROUP_W, GROUP_W), F32),
                        pltpu.VMEM((G, 1, RWKV_PROJ), F32)],
        compiler_params=_params("parallel", "arbitrary"),
        name="rwkv_mix",
    )(proj3, shift0, s0, p["rwkv_mu"], p["rwkv_w0"], p["rwkv_w2"], p["rwkv_a0"], p["rwkv_a2"],
      p["rwkv_g2"], p["rwkv_kk"], p["rwkv_ka"], p["rwkv_rk"], p["rwkv_ln_w"], p["rwkv_ln_b"],
      consts["ones"], _block_tril(G, L), consts["half"], consts["tri_levels"])
    return y[:, :t_len], s_new


def _gdn_kernel(qkv_ref, ab_ref, z_ref, conv0_ref, s0_ref, cw_ref, alog_ref, dtb_ref, nw_ref,
                ones_ref, ltri_ref, ea_ref, eb_ref, half_ref, lvl_ref,
                y_ref, sout_ref, s_scr, xpad_scr, *, n_valid):
    G, L = qkv_ref.shape[0], qkv_ref.shape[1]
    t = pl.program_id(1)
    npre = GDN_CONV - 1
    base_row = SUBLANE

    @pl.when(t == 0)
    def _():
        _load_state(s_scr, s0_ref)
        xpad_scr[:, base_row - npre:base_row, :] = conv0_ref[...]

    convs = []
    for g in range(G):
        x = qkv_ref[g]
        xpad_scr[g, base_row:base_row + L, :] = x
        conv = x * cw_ref[npre:npre + 1, :]
        for j in range(npre):
            lo = base_row - npre + j
            conv = conv + xpad_scr[g, lo:lo + L, :] * cw_ref[j:j + 1, :]
        xpad_scr[g, base_row - npre:base_row, :] = xpad_scr[g, base_row + L - npre:base_row + L, :]
        convs.append(conv)
    conv = jnp.concatenate(convs, axis=0)
    qkv = conv * _sigmoid(conv)

    ones_b = ones_ref[...]
    q = qkv[:, 0:W_B]
    k = qkv[:, W_B:2 * W_B]
    v = qkv[:, 2 * W_B:]
    q = q * lax.rsqrt(_seg_sum(q * q, ones_b, False) + 1e-6) * (HEAD ** -0.5)
    k = k * lax.rsqrt(_seg_sum(k * k, ones_b, False) + 1e-6)

    ab = ab_ref[...].reshape(G * L, AB_PAD)
    g_row = -jnp.exp(alog_ref[...]) * _softplus(ab + dtb_ref[...])
    g_exp = _mm_lsplit(g_row, ea_ref[...])
    beta = _mm_lsplit(_sigmoid(ab), eb_ref[...])
    if n_valid < L:
        live = jnp.bitwise_and(lax.broadcasted_iota(jnp.int32, k.shape, 0), L - 1) < n_valid
        k = jnp.where(live, k, 0.0)
        beta = jnp.where(live, beta, 0.0)
        g_exp = jnp.where(live, g_exp, 0.0)
    gc = _mm_rsplit(ltri_ref[...], g_exp)
    g_last = _rows_from([gc[(g + 1) * L - 1:(g + 1) * L, :] for g in range(G)], L)
    e_gc = jnp.exp(gc)
    kbeta = k * beta
    vbeta = v * beta
    kw = kbeta * e_gc
    groups = lambda x: _to_groups(x, L)
    k_g = groups(k.astype(BF16))
    q_g = groups(q.astype(BF16))
    kbeta_g = groups(kbeta.astype(BF16))
    vbeta_g = groups(vbeta)
    kw_g = groups(kw)
    qe_g = groups((q * e_gc).astype(BF16))
    kd_g = groups((k * jnp.exp(g_last - gc)).astype(BF16))
    gc_g = groups(gc)
    el_g = groups(jnp.exp(g_last))

    mask_b = half_ref
    strict, incl, eye = _packed_tri_masks(L)
    n_groups = len(k_g)
    s_old = [s_scr[i] for i in range(n_groups)]
    s_old_b = [_keep_diag(s.astype(BF16), mask_b) for s in s_old]
    gamma = []
    for x in gc_g:
        g_row_j = jnp.sum(jnp.where(eye, x, 0.0), axis=0, keepdims=True)
        gamma.append(jnp.where(incl, jnp.exp(jnp.minimum(x - g_row_j, 0.0)), 0.0))
    kk = [_bdot_nt(jnp.concatenate([x, y], axis=0), _block_diag(kb, mask_b))
          for x, y, kb in zip(kbeta_g, q_g, k_g)]
    lower = [jnp.where(strict, x[:L] * gm, 0.0) for x, gm in zip(kk, gamma)]
    att_b = [(x[L:] * gm).astype(BF16) for x, gm in zip(kk, gamma)]
    tp_b = [x.astype(BF16) for x in _tri_inv_minus_eye_packed(lower, mask_b, lvl_ref)]
    u_pre = [x + _bdot(t_, _block_diag(x.astype(BF16), mask_b)) for x, t_ in zip(vbeta_g, tp_b)]
    w_b = [(x + _bdot(t_, _block_diag(x.astype(BF16), mask_b))).astype(BF16) for x, t_ in zip(kw_g, tp_b)]
    ws = [_bdot(jnp.concatenate([w_, qe], axis=0), s) for w_, qe, s in zip(w_b, qe_g, s_old_b)]
    u_b = [(up - x[:L]).astype(BF16) for up, x in zip(u_pre, ws)]
    o_g = [x[L:] + _bdot(a_, _block_diag(ub, mask_b)) for x, a_, ub in zip(ws, att_b, u_b)]
    for i in range(n_groups):
        s_scr[i] = s_old[i] * el_g[i][0:1, :] + _bdot_tn(kd_g[i], u_b[i])

    o = _from_groups(o_g)
    on = o * lax.rsqrt(_seg_sum(o * o, ones_b, False) * (1.0 / HEAD) + EPS) * nw_ref[...]
    z = z_ref[...].reshape(G * L, W_B)
    y_ref[...] = (on * (z * _sigmoid(z))).reshape(G, L, W_B)

    @pl.when(t == pl.num_programs(1) - 1)
    def _():
        _store_state(sout_ref, s_scr)


def _gdn_mix(proj3, conv0, s0, p, consts):
    bsz, t_len, _ = proj3.shape
    assert t_len >= GDN_CONV - 1
    proj3, n_valid = _mixer_rows(proj3)
    t_pad = proj3.shape[1]
    L, G = CHUNK, MIX_GROUP
    assert bsz % G == 0 and G % 2 == 0
    y, s_new = pl.pallas_call(
        functools.partial(_gdn_kernel, n_valid=n_valid),
        grid=(bsz // G, t_pad // L),
        in_specs=[pl.BlockSpec((G, L, GDN_CH), lambda b, t: (b, t, OFF_QKV // GDN_CH)),
                  pl.BlockSpec((G, L, AB_PAD), lambda b, t: (b, t, OFF_AB // AB_PAD)),
                  pl.BlockSpec((G, L, W_B), lambda b, t: (b, t, OFF_Z // W_B)),
                  pl.BlockSpec((G, GDN_CONV - 1, GDN_CH), lambda b, t: (b, 0, 0)),
                  pl.BlockSpec((G, H_B, HEAD, HEAD), lambda b, t: (b, 0, 0, 0)),
                  _const_spec((GDN_CONV, GDN_CH)), _const_spec((1, AB_PAD)), _const_spec((1, AB_PAD)),
                  _const_spec((1, W_B)), _const_spec((LANE, LANE)), _const_spec((G * L, G * L)),
                  _const_spec((AB_PAD, W_B)), _const_spec((AB_PAD, W_B)),
                  _const_spec((LANE // HEAD, HEAD, LANE)), _const_spec((N_TRI_LEVELS, CHUNK, GROUP_W))],
        out_specs=[pl.BlockSpec((G, L, W_B), lambda b, t: (b, t, 0)),
                   pl.BlockSpec((G, H_B, HEAD, HEAD), lambda b, t: (b, 0, 0, 0))],
        out_shape=[jax.ShapeDtypeStruct((bsz, t_pad, W_B), F32),
                   jax.ShapeDtypeStruct((bsz, H_B, HEAD, HEAD), F32)],
        scratch_shapes=[pltpu.VMEM((G // 2 * GROUPS_PER_PAIR, GROUP_W, GROUP_W), F32),
                        pltpu.VMEM((G, SUBLANE + L, GDN_CH), F32)],
        compiler_params=_params("parallel", "arbitrary"),
        name="gdn_mix",
    )(proj3, proj3, proj3, conv0, s0, p["gdn_conv_w"], p["gdn_A_log"], p["gdn_dt_bias"],
      p["gdn_norm_w"], consts["ones"], _block_tril(G, L), consts["ea"], consts["eb"],
      consts["half"], consts["tri_levels"])
    return y[:, :t_len], s_new


def _attend_scores(q, kwin, bias_ref, min_col):
    q_b = (q * (HEAD ** -0.5)).astype(BF16)
    sl = lambda x, h: x[:, h * HEAD:(h + 1) * HEAD]
    s = [_bdot_nt(sl(q_b, h), sl(kwin, h)) + bias_ref[h] for h in range(H_C)]
    if min_col is not None:
        col = lax.broadcasted_iota(jnp.int32, s[0].shape, 1)
        s = [jnp.where(col >= min_col, x, -jnp.inf) for x in s]
    e = [jnp.exp(x - jnp.max(x, axis=-1, keepdims=True)) for x in s]
    denom = [jnp.sum(x, axis=-1, keepdims=True) for x in e]
    return [x.astype(BF16) for x in e], denom


def _attend_values(e_b, denom, vwin, o_ref):
    pv = [_bdot(e_b[h], vwin[:, h * HEAD:(h + 1) * HEAD]) for h in range(H_C)]
    for h in range(H_C):
        o_ref[:, h * HEAD:(h + 1) * HEAD] = pv[h] / denom[h]


def _in_proj_band_kernel(x_ref, g_ref, w_ref, bias_ref, o_ref, yc_ref, q_scr, k_scr, v_scr, *, tiles_per_seq):
    i = pl.program_id(0)
    tm = x_ref.shape[0]
    lead = N_BAND_PREV * CHUNK

    @pl.when(i == 0)
    def _():
        q_scr[...] = jnp.zeros(q_scr.shape, F32)
        k_scr[...] = jnp.zeros(k_scr.shape, BF16)
        v_scr[...] = jnp.zeros(v_scr.shape, BF16)

    slots = [lax.rem(i + d, 3) for d in range(3)]
    kwin = jnp.concatenate([k_scr[sl] for sl in slots], axis=0)
    vwin = jnp.concatenate([v_scr[sl] for sl in slots], axis=0)
    pos = lax.rem(i + tiles_per_seq - 1, tiles_per_seq)
    min_col = lead - pos * tm
    q_b = (q_scr[...] * (HEAD ** -0.5)).astype(BF16)
    col = lax.broadcasted_iota(jnp.int32, (tm, lead + tm), 1)

    x = x_ref[...]
    h = (x * lax.rsqrt(jnp.mean(x * x, axis=-1, keepdims=True) + EPS) * g_ref[...]).astype(BF16)
    n_col_tiles = N_PROJ // TN_IN
    share = n_col_tiles // H_C
    for hd in range(H_C):
        sl = slice(hd * HEAD, (hd + 1) * HEAD)
        sc = _bdot_nt(q_b[:, sl], kwin[:, sl]) + bias_ref[hd]
        sc = jnp.where(col >= min_col, sc, -jnp.inf)
        e = jnp.exp(sc - jnp.max(sc, axis=-1, keepdims=True))
        denom = jnp.sum(e, axis=-1, keepdims=True)
        for n in range(hd * share, n_col_tiles if hd == H_C - 1 else (hd + 1) * share):
            cols = slice(n * TN_IN, (n + 1) * TN_IN)
            o_ref[:, cols] = _bdot(h, w_ref[:, cols])
        yc_ref[:, sl] = _bdot(e.astype(BF16), vwin[:, sl]) / denom
    q_scr[...] = o_ref[:, OFF_ATT:OFF_ATT + W_C]
    k_scr[slots[0]] = o_ref[:, OFF_ATT + W_C:OFF_ATT + 2 * W_C].astype(BF16)
    v_scr[slots[0]] = o_ref[:, OFF_ATT + 2 * W_C:OFF_ATT + 3 * W_C].astype(BF16)


def _in_proj_band(x2d, g, w_staged, layer, rel_bias, t_len):
    m = x2d.shape[0]
    tm = BAND_Q_CHUNKS * CHUNK
    lead = N_BAND_PREV * CHUNK
    assert t_len % tm == 0 and m % t_len == 0 and lead == 2 * tm
    n_tiles = m // tm
    n_k = lead + tm
    first = (np.arange(tm)[:, None] // CHUNK) * CHUNK
    cols = np.arange(n_k)[None, :]
    in_band = (cols >= first) & (cols < first + lead + CHUNK)
    bias = jnp.where(in_band[None], _rel_bias_tile(rel_bias, tm, n_k, lead), -jnp.inf)
    held = lambda *shape, idx=None: pl.BlockSpec(shape, idx or (lambda i: (0,) * len(shape)),
                                                 pipeline_mode=pl.Buffered(1))
    return pl.pallas_call(
        functools.partial(_in_proj_band_kernel, tiles_per_seq=t_len // tm),
        grid=(n_tiles + 1,),
        in_specs=[pl.BlockSpec((tm, D_MODEL), lambda i: (jnp.minimum(i, n_tiles - 1), 0)),
                  _const_spec((1, D_MODEL)),
                  held(None, D_MODEL, N_PROJ, idx=lambda i: (layer, 0, 0)),
                  held(H_C, tm, n_k)],
        out_specs=[pl.BlockSpec((tm, N_PROJ), lambda i: (jnp.minimum(i, n_tiles - 1), 0)),
                   pl.BlockSpec((tm, W_C), lambda i: (jnp.maximum(i - 1, 0), 0))],
        out_shape=[jax.ShapeDtypeStruct((m, N_PROJ), F32), jax.ShapeDtypeStruct((m, W_C), F32)],
        scratch_shapes=[pltpu.VMEM((tm, W_C), F32),
                        pltpu.VMEM((3, tm, W_C), BF16),
                        pltpu.VMEM((3, tm, W_C), BF16)],
        compiler_params=_params("arbitrary"),
        name="in_proj_band",
    )(x2d, g, w_staged, bias)


def _band_step_kernel(q_ref, k_ref, v_ref, bias_ref, o_ref):
    e_b, denom = _attend_scores(q_ref[0], k_ref[0].astype(BF16), bias_ref, None)
    _attend_values(e_b, denom, v_ref[0].astype(BF16), o_ref.at[0])


def _band_step(proj3, k_all, v_all, bias):
    bsz, t_len, _ = proj3.shape
    n_keys = k_all.shape[1]
    return pl.pallas_call(
        _band_step_kernel,
        grid=(bsz,),
        in_specs=[pl.BlockSpec((1, t_len, W_C), lambda b: (b, 0, OFF_ATT // W_C)),
                  pl.BlockSpec((1, n_keys, W_C), lambda b: (b, 0, 0)),
                  pl.BlockSpec((1, n_keys, W_C), lambda b: (b, 0, 0)),
                  _const_spec((H_C, t_len, n_keys))],
        out_specs=pl.BlockSpec((1, t_len, W_C), lambda b: (b, 0, 0)),
        out_shape=jax.ShapeDtypeStruct((bsz, t_len, W_C), F32),
        compiler_params=_params("parallel"),
        name="band_step",
    )(proj3, k_all, v_all, bias)


def _rel_bias_tile(rel_bias, n_q, n_k, lead):
    n_d = n_q + n_k - 1
    d = np.clip(lead + n_q - 1 - np.arange(n_d), -MAX_REL, MAX_REL) + MAX_REL
    n_hi = int(np.sum(d == 2 * MAX_REL)) - 1 if d[0] == 2 * MAX_REL else 0
    n_lo = int(np.sum(d == 0)) - 1 if d[-1] == 0 else 0
    core = rel_bias[:, int(d[-1 - n_lo]):int(d[n_hi]) + 1][:, ::-1]
    rev = jnp.concatenate([jnp.repeat(rel_bias[:, -1:], n_hi, axis=1), core,
                           jnp.repeat(rel_bias[:, :1], n_lo, axis=1)], axis=1)
    heads = rel_bias.shape[0]
    flat = jnp.broadcast_to(rev[:, None, :], (heads, n_q, n_d)).reshape(heads, n_q * n_d)
    skew = flat[:, n_q - 1:n_q - 1 + n_q * (n_d - 1)].reshape(heads, n_q, n_d - 1)
    return skew[:, :, :n_k]


def _rms(x):
    return x * lax.rsqrt(jnp.mean(x * x, axis=-1, keepdims=True) + EPS)


def _merge_ffn_kernel(x_ref, ya_ref, yb_ref, yc_ref, g0_ref, g1_ref, g2_ref, wa_ref, wb_ref, wc_ref,
                      wo_ref, gmix_ref, gpre_ref, up_ref, down_ref, gpost_ref, o_ref):
    merged = (_sigmoid(g0_ref[...]) * _mm(ya_ref[...], wa_ref[...])
              + _sigmoid(g1_ref[...]) * _mm(yb_ref[...], wb_ref[...])
              + _sigmoid(g2_ref[...]) * _mm(yc_ref[...], wc_ref[...]))
    x1 = x_ref[...] + _rms(_mm(merged, wo_ref[...])) * gmix_ref[...]
    h = (_rms(x1) * gpre_ref[...]).astype(BF16)
    f = jnp.zeros(x1.shape, F32)
    for n in range(D_FF // FF_CHUNK):
        cols = slice(n * FF_CHUNK, (n + 1) * FF_CHUNK)
        act = jnp.maximum(_bdot(h, up_ref[:, cols]), 0.0)
        f = f + _bdot((act * act).astype(BF16), down_ref[cols, :])
    o_ref[...] = x1 + _rms(f) * gpost_ref[...]


def _merge_ffn(x2d, ya, yb, yc, proj2, p):
    m = x2d.shape[0]
    tm = min(TM_FFN, m)
    tok = lambda n, c=0: pl.BlockSpec((tm, n), lambda i, c=c: (i, c))
    held = lambda *shape: pl.BlockSpec(shape, lambda i: (0,) * len(shape), pipeline_mode=pl.Buffered(1))
    return pl.pallas_call(
        _merge_ffn_kernel,
        grid=(m // tm,),
        in_specs=[tok(D_MODEL), tok(W_A), tok(W_B), tok(W_C),
                  tok(D_MODEL, 0), tok(D_MODEL, 1), tok(D_MODEL, 2),
                  held(W_A, D_MODEL), held(W_B, D_MODEL), held(W_C, D_MODEL), held(D_MODEL, D_MODEL),
                  held(1, D_MODEL), held(1, D_MODEL), held(D_MODEL, D_FF), held(D_FF, D_MODEL),
                  held(1, D_MODEL)],
        out_specs=tok(D_MODEL),
        out_shape=jax.ShapeDtypeStruct((m, D_MODEL), F32),
        compiler_params=_params("parallel"),
        name="merge_ffn",
    )(x2d, ya, yb, yc, proj2, proj2, proj2, p["w_br_a"], p["w_br_b"], p["w_br_c"], p["w_out"],
      p["g_post_mix"], p["g_pre_ffn"], p["w_ff_up"], p["w_ff_down"], p["g_post_ffn"])


def _block_tril(groups, n):
    idx = np.arange(groups * n)
    same = (idx[:, None] // n) == (idx[None, :] // n)
    return jnp.asarray((same & (idx[:, None] >= idx[None, :])).astype(np.float32), BF16)


def _constants():
    lane_head = np.arange(LANE) // HEAD
    ones = (lane_head[:, None] == lane_head[None, :]).astype(np.float32)
    half = np.stack([np.broadcast_to(lane_head == p, (HEAD, LANE)) for p in range(LANE // HEAD)]).astype(np.float32)
    ea = np.zeros((AB_PAD, W_B), np.float32)
    eb = np.zeros((AB_PAD, W_B), np.float32)
    for h in range(H_B):
        ea[h, h * HEAD:(h + 1) * HEAD] = 1.0
        eb[H_B + h, h * HEAD:(h + 1) * HEAD] = 1.0
    i = np.arange(CHUNK)[:, None]
    j = (np.arange(GROUP_W) % HEAD)[None, :]
    same = lambda size: (i // size) == (j // size)
    levels = [same(2)] + [same(4 << l) & ~same(2 << l) for l in range(N_TRI_LEVELS - 1)]
    tri_levels = np.stack(levels).astype(np.float32)
    return {"tri_levels": jnp.asarray(tri_levels), "ones": jnp.asarray(ones, BF16), "ea": jnp.asarray(ea, BF16), "eb": jnp.asarray(eb, BF16),
            "half": jnp.asarray(half, BF16)}


def _stage_layer_params(l, w_staged, g_pre_mix, g_post_mix, g_pre_ffn, g_post_ffn, rwkv_mu, rwkv_w0, rwkv_w2,
                        rwkv_a0, rwkv_a2, rwkv_g2, rwkv_kk, rwkv_ka, rwkv_rk, rwkv_ln_w, rwkv_ln_b,
                        gdn_conv_w, gdn_A_log, gdn_dt_bias, gdn_norm_w, att_rel_bias, w_br_a, w_br_b,
                        w_br_c, w_out, w_ff_up, w_ff_down):
    row = lambda a: a[l].reshape(1, -1).astype(F32)
    pad_ab = lambda a: jnp.pad(a[l].astype(F32), (0, AB_PAD - H_B)).reshape(1, AB_PAD)
    return {
        "g_pre_mix": row(g_pre_mix), "g_post_mix": row(g_post_mix),
        "g_pre_ffn": row(g_pre_ffn), "g_post_ffn": row(g_post_ffn),
        "w_in": w_staged, "layer": l,
        "rwkv_mu": row(rwkv_mu), "rwkv_w0": row(rwkv_w0), "rwkv_w2": rwkv_w2[l].astype(BF16),
        "rwkv_a0": row(rwkv_a0), "rwkv_a2": rwkv_a2[l].astype(BF16), "rwkv_g2": rwkv_g2[l].astype(BF16),
        "rwkv_kk": row(rwkv_kk), "rwkv_ka": row(rwkv_ka), "rwkv_rk": row(rwkv_rk),
        "rwkv_ln_w": row(rwkv_ln_w), "rwkv_ln_b": row(rwkv_ln_b),
        "gdn_conv_w": gdn_conv_w[l].astype(F32), "gdn_A_log": pad_ab(gdn_A_log),
        "gdn_dt_bias": pad_ab(gdn_dt_bias),
        "gdn_norm_w": jnp.tile(gdn_norm_w[l].astype(F32), H_B).reshape(1, W_B),
        "att_rel_bias": att_rel_bias[l].astype(F32),
        "w_br_a": w_br_a[l].astype(BF16), "w_br_b": w_br_b[l].astype(BF16), "w_br_c": w_br_c[l].astype(BF16),
        "w_out": w_out[l].astype(BF16), "w_ff_up": w_ff_up[l].astype(BF16),
        "w_ff_down": w_ff_down[l].astype(BF16),
    }


def _trunk_layer(x, p, consts, rwkv_shift, rwkv_s, gdn_conv, gdn_s, band_k, band_v):
    bsz, t_len, _ = x.shape
    m = bsz * t_len
    x2d = x.reshape(m, D_MODEL)
    if band_k is None:
        proj2, yc = _in_proj_band(x2d, p["g_pre_mix"], p["w_in"], p["layer"], p["att_rel_bias"], t_len)
    else:
        proj2 = _in_proj(x2d, p["g_pre_mix"], p["w_in"], p["layer"])
    proj3 = proj2.reshape(bsz, t_len, N_PROJ)
    ya, s_a = _rwkv_mix(proj3, rwkv_shift.reshape(bsz, 1, RWKV_PROJ), rwkv_s, p, consts)
    yb, s_b = _gdn_mix(proj3, gdn_conv, gdn_s, p, consts)
    k_new = proj3[:, :, OFF_ATT + W_C:OFF_ATT + 2 * W_C]
    v_new = proj3[:, :, OFF_ATT + 2 * W_C:OFF_ATT + 3 * W_C]
    if band_k is None:
        keep = min(N_BAND_PREV * CHUNK, t_len)
        k_out, v_out = k_new[:, t_len - keep:], v_new[:, t_len - keep:]
    else:
        n_past = band_k.shape[1]
        k_all = jnp.concatenate([band_k.reshape(bsz, n_past, W_C), k_new], axis=1)
        v_all = jnp.concatenate([band_v.reshape(bsz, n_past, W_C), v_new], axis=1)
        yc = _band_step(proj3, k_all, v_all, _rel_bias_tile(p["att_rel_bias"], t_len, n_past + t_len, n_past))
        k_out, v_out = k_new, v_new
    x2 = _merge_ffn(x2d, ya.reshape(m, W_A), yb.reshape(m, W_B), yc.reshape(m, W_C), proj2, p)
    shift_new = proj3[:, t_len - 1, OFF_RWKV:OFF_RWKV + RWKV_PROJ]
    conv_new = proj3[:, t_len - (GDN_CONV - 1):, OFF_QKV:OFF_QKV + GDN_CH]
    new_state = (shift_new, s_a, conv_new, s_b,
                 k_out.reshape(bsz, -1, H_C, HEAD), v_out.reshape(bsz, -1, H_C, HEAD))
    return x2.reshape(bsz, t_len, D_MODEL), new_state


def kernel(x_prompt, x_sample, state_rwkv_shift, state_rwkv_wkv, state_gdn_conv, state_gdn_S, cache_band_k, cache_band_v, g_pre_mix, g_post_mix, g_pre_ffn, g_post_ffn, w_in, rwkv_mu, rwkv_w0, rwkv_w2, rwkv_a0, rwkv_a2, rwkv_g2, rwkv_kk, rwkv_ka, rwkv_rk, rwkv_ln_w, rwkv_ln_b, gdn_conv_w, gdn_A_log, gdn_dt_bias, gdn_norm_w, att_rel_bias, w_br_a, w_br_b, w_br_c, w_out, w_ff_up, w_ff_down):
    depth = w_in.shape[0]
    bp = x_prompt.shape[0]
    consts = _constants()
    w_staged = _stage_w_in(w_in)
    xp, xs = x_prompt, x_sample
    p_new = [[] for _ in range(6)]
    s_new = [[] for _ in range(6)]
    for l in range(depth):
        p = _stage_layer_params(l, w_staged, g_pre_mix, g_post_mix, g_pre_ffn, g_post_ffn, rwkv_mu, rwkv_w0,
                                rwkv_w2, rwkv_a0, rwkv_a2, rwkv_g2, rwkv_kk, rwkv_ka, rwkv_rk, rwkv_ln_w,
                                rwkv_ln_b, gdn_conv_w, gdn_A_log, gdn_dt_bias, gdn_norm_w, att_rel_bias,
                                w_br_a, w_br_b, w_br_c, w_out, w_ff_up, w_ff_down)
        xp, st_p = _trunk_layer(
            xp, p, consts,
            jnp.zeros((bp, RWKV_PROJ), F32), jnp.zeros((bp, H_A, HEAD, HEAD), F32),
            jnp.zeros((bp, GDN_CONV - 1, GDN_CH), F32), jnp.zeros((bp, H_B, HEAD, HEAD), F32),
            None, None)
        xs, st_s = _trunk_layer(
            xs, p, consts, state_rwkv_shift[l], state_rwkv_wkv[l], state_gdn_conv[l], state_gdn_S[l],
            cache_band_k[l], cache_band_v[l])
        for i in range(6):
            p_new[i].append(st_p[i])
            s_new[i].append(st_s[i])
    stk_p = [jnp.stack(a).astype(x_prompt.dtype) for a in p_new]
    stk_s = [jnp.stack(a).astype(x_sample.dtype) for a in s_new]
    return (xp, xs, *stk_p, *stk_s)
```

```python
import functools
import math

import numpy as np
import jax
import jax.numpy as jnp
from jax import lax
from jax.experimental import pallas as pl
from jax.experimental.pallas import tpu as pltpu

F32 = jnp.float32
BF16 = jnp.bfloat16

D_MODEL = 1024
HEAD = 64
H_A = 6
W_A = H_A * HEAD
DECAY_RANK = 64
ICLR_RANK = 64
GATE_RANK = 128
RWKV_PROJ = 3 * W_A + DECAY_RANK + ICLR_RANK + GATE_RANK
H_B = 6
W_B = H_B * HEAD
GDN_CONV = 4
GDN_CH = 3 * W_B
H_C = 4
W_C = H_C * HEAD
N_BAND_PREV = 8
MAX_REL = 128
CHUNK = 64
D_FF = 4 * D_MODEL
EPS = 1e-6
GN_EPS = 64e-5

LANE = 128
SUBLANE = 8
AB_PAD = LANE
OFF_GATE = 0
OFF_ATT = OFF_GATE + 3 * D_MODEL
OFF_Z = OFF_ATT + 3 * W_C
OFF_RWKV = OFF_Z + W_B
OFF_AB = OFF_RWKV + RWKV_PROJ
OFF_QKV = OFF_AB + AB_PAD
N_PROJ = OFF_QKV + GDN_CH
assert OFF_ATT % W_C == 0 and OFF_Z % W_B == 0 and OFF_RWKV % RWKV_PROJ == 0
assert OFF_AB % AB_PAD == 0 and OFF_QKV % GDN_CH == 0 and N_PROJ % LANE == 0

VMEM_LIMIT = 56 * 1024 * 1024
TM_IN = 512
TN_IN = 768
TM_FFN = 512
FF_CHUNK = 1024
MIX_GROUP = 8
BAND_Q_CHUNKS = 4


def _params(*sem):
    return pltpu.CompilerParams(dimension_semantics=sem, vmem_limit_bytes=VMEM_LIMIT)


def _const_spec(shape):
    nd = len(shape)
    return pl.BlockSpec(shape, lambda *_: (0,) * nd)


def _bdot(a, b):
    return jnp.dot(a, b, preferred_element_type=F32)


def _bdot_nt(a, b):
    return lax.dot_general(a, b, (((1,), (1,)), ((), ())), preferred_element_type=F32)


def _bdot_tn(a, b):
    return lax.dot_general(a, b, (((0,), (0,)), ((), ())), preferred_element_type=F32)


def _mm(a, b):
    return _bdot(a.astype(BF16), b.astype(BF16))


def _split(a):
    hi = a.astype(BF16)
    lo = (a - hi.astype(F32)).astype(BF16)
    return hi, lo


def _mm_lsplit(a, b_exact):
    hi, lo = _split(a)
    return _bdot(hi, b_exact) + _bdot(lo, b_exact)


def _mm_rsplit(a_exact, b):
    hi, lo = _split(b)
    return _bdot(a_exact, hi) + _bdot(a_exact, lo)


def _sigmoid(x):
    return 0.5 * jnp.tanh(0.5 * x) + 0.5


def _softplus(x):
    return jnp.maximum(x, 0.0) + jnp.log(1.0 + jnp.exp(-jnp.abs(x)))


def _rows_from(ref_rows, n):
    return jnp.concatenate([jnp.broadcast_to(r, (n, r.shape[-1])) for r in ref_rows], axis=0)


N_IN = RWKV_PROJ + GDN_CH + 2 * H_B + W_B + 3 * W_C + 3 * D_MODEL
SRC_QKV = RWKV_PROJ
SRC_AB = SRC_QKV + GDN_CH
SRC_Z = SRC_AB + 2 * H_B
TR_STAGE = 128
assert SRC_QKV % LANE == 0 and SRC_AB % LANE == 0


def _stage_w_in_kernel(w_ref, o_ref):
    w = w_ref[0]
    tail = w[:, SRC_Z:N_IN].astype(BF16)
    o_ref[0, :, OFF_GATE:OFF_GATE + 3 * D_MODEL] = tail[:, W_B + 3 * W_C:]
    o_ref[0, :, OFF_ATT:OFF_ATT + 3 * W_C] = tail[:, W_B:W_B + 3 * W_C]
    o_ref[0, :, OFF_Z:OFF_Z + W_B] = tail[:, 0:W_B]
    o_ref[0, :, OFF_RWKV:OFF_RWKV + RWKV_PROJ] = w[:, 0:RWKV_PROJ].astype(BF16)
    o_ref[0, :, OFF_AB:OFF_AB + AB_PAD] = jnp.zeros((w.shape[0], AB_PAD), BF16)
    o_ref[0, :, OFF_AB:OFF_AB + 2 * H_B] = w[:, SRC_AB:SRC_AB + 2 * H_B].astype(BF16)
    o_ref[0, :, OFF_QKV:OFF_QKV + GDN_CH] = w[:, SRC_QKV:SRC_QKV + GDN_CH].astype(BF16)


def _stage_w_in(w_in):
    depth, d_model, n_in = w_in.shape
    assert d_model == D_MODEL and n_in == N_IN
    return pl.pallas_call(
        _stage_w_in_kernel,
        grid=(depth, D_MODEL // TR_STAGE),
        in_specs=[pl.BlockSpec((1, TR_STAGE, N_IN), lambda l, i: (l, i, 0))],
        out_specs=pl.BlockSpec((1, TR_STAGE, N_PROJ), lambda l, i: (l, i, 0)),
        out_shape=jax.ShapeDtypeStruct((depth, D_MODEL, N_PROJ), BF16),
        compiler_params=_params("parallel", "parallel"),
        name="stage_w_in",
    )(w_in)


def _in_proj_kernel(x_ref, g_ref, w_ref, o_ref):
    x = x_ref[...]
    h = (x * lax.rsqrt(jnp.mean(x * x, axis=-1, keepdims=True) + EPS) * g_ref[...]).astype(BF16)
    for n in range(N_PROJ // TN_IN):
        cols = slice(n * TN_IN, (n + 1) * TN_IN)
        o_ref[:, cols] = _bdot(h, w_ref[:, cols])


def _in_proj(x2d, g, w_staged, layer):
    m = x2d.shape[0]
    tm = min(TM_IN, m)
    return pl.pallas_call(
        _in_proj_kernel,
        grid=(m // tm,),
        in_specs=[pl.BlockSpec((tm, D_MODEL), lambda i: (i, 0)),
                  _const_spec((1, D_MODEL)),
                  pl.BlockSpec((None, D_MODEL, N_PROJ), lambda i: (layer, 0, 0), pipeline_mode=pl.Buffered(1))],
        out_specs=pl.BlockSpec((tm, N_PROJ), lambda i: (i, 0)),
        out_shape=jax.ShapeDtypeStruct((m, N_PROJ), F32),
        compiler_params=_params("parallel"),
        name="in_proj",
    )(x2d, g, w_staged)


PACK = 4
GROUP_W = PACK * HEAD
GROUPS_PER_PAIR = 2 * H_A // PACK
N_TRI_LEVELS = int(math.log2(HEAD))
assert H_A == H_B and 2 * H_A % PACK == 0 and GROUP_W % LANE == 0 and CHUNK == HEAD


def _group_slots(pair_group):
    if pair_group < 2:
        return [(pair_group, h) for h in range(PACK)]
    return [(0, PACK), (0, PACK + 1), (1, PACK), (1, PACK + 1)]


def _to_groups(x, L):
    out = []
    for a in range(0, x.shape[0] // L, 2):
        ra, rb = slice(a * L, (a + 1) * L), slice((a + 1) * L, (a + 2) * L)
        out += [x[ra, 0:GROUP_W], x[rb, 0:GROUP_W],
                jnp.concatenate([x[ra, GROUP_W:W_A], x[rb, GROUP_W:W_A]], axis=1)]
    return out


def _from_groups(vals):
    rows = []
    half = (W_A - GROUP_W)
    for p in range(len(vals) // GROUPS_PER_PAIR):
        ga, gb, gc = vals[3 * p:3 * p + 3]
        rows.append(jnp.concatenate([ga, gc[:, 0:half]], axis=1))
        rows.append(jnp.concatenate([gb, gc[:, half:2 * half]], axis=1))
    return jnp.concatenate(rows, axis=0)


def _keep_diag(m_b, half_ref):
    per_tile = LANE // HEAD
    zero = jnp.zeros((HEAD, LANE), BF16)
    rows = []
    for h in range(PACK):
        t = h // per_tile
        tile = m_b[h * HEAD:(h + 1) * HEAD, t * LANE:(t + 1) * LANE] * half_ref[h % per_tile]
        rows.append(jnp.concatenate([tile if j == t else zero for j in range(GROUP_W // LANE)], axis=1))
    return jnp.concatenate(rows, axis=0)


def _block_diag(x_b, half_ref):
    return _keep_diag(jnp.concatenate([x_b] * PACK, axis=0), half_ref)


def _packed_tri_masks(L):
    r = lax.broadcasted_iota(jnp.int32, (L, GROUP_W), 0)
    c = jnp.bitwise_and(lax.broadcasted_iota(jnp.int32, (L, GROUP_W), 1), HEAD - 1)
    return c < r, c <= r, c == r


def _tri_inv_minus_eye_packed(a_list, mask_b, lvl_ref):
    tp = [-(a * lvl_ref[0]) for a in a_list]
    for lvl in range(1, N_TRI_LEVELS):
        m = lvl_ref[lvl]
        off = [a * m for a in a_list]
        off_bd = [_block_diag(x.astype(BF16), mask_b) for x in off]
        tp_b = [t.astype(BF16) for t in tp]
        m1 = [x + _bdot(tb, bd) for x, tb, bd in zip(off, tp_b, off_bd)]
        tp_bd = [_block_diag(tb, mask_b) for tb in tp_b]
        tp = [t - x - _bdot(x.astype(BF16), bd) for t, x, bd in zip(tp, m1, tp_bd)]
    return tp


def _seg_sum(x, ones_b, exact):
    outs = []
    for j in range(W_A // LANE):
        xs = x[:, j * LANE:(j + 1) * LANE]
        outs.append(_mm_lsplit(xs, ones_b) if exact else _bdot(xs.astype(BF16), ones_b))
    return jnp.concatenate(outs, axis=1)


def _load_state(s_scr, s0_ref):
    s_scr[...] = jnp.zeros(s_scr.shape, F32)
    for gi in range(s_scr.shape[0]):
        for j, (q, h) in enumerate(_group_slots(gi % GROUPS_PER_PAIR)):
            seq = 2 * (gi // GROUPS_PER_PAIR) + q
            s_scr[gi, j * HEAD:(j + 1) * HEAD, j * HEAD:(j + 1) * HEAD] = s0_ref[seq, h]


def _store_state(sout_ref, s_scr):
    for gi in range(s_scr.shape[0]):
        for j, (q, h) in enumerate(_group_slots(gi % GROUPS_PER_PAIR)):
            seq = 2 * (gi // GROUPS_PER_PAIR) + q
            sout_ref[seq, h] = s_scr[gi, j * HEAD:(j + 1) * HEAD, j * HEAD:(j + 1) * HEAD]


def _rwkv_kernel(c_ref, shift0_ref, s0_ref, mu_ref, w0_ref, w2_ref, a0_ref, a2_ref, g2_ref,
                 kk_ref, ka_ref, rk_ref, lnw_ref, lnb_ref, ones_ref, ltri_ref, half_ref, lvl_ref,
                 y_ref, sout_ref, s_scr, prev_scr, *, n_valid):
    G, L = c_ref.shape[0], c_ref.shape[1]
    t = pl.program_id(1)

    @pl.when(t == 0)
    def _():
        _load_state(s_scr, s0_ref)
        prev_scr[...] = shift0_ref[...]

    c = c_ref[...].reshape(G * L, RWKV_PROJ)
    row = jnp.bitwise_and(lax.broadcasted_iota(jnp.int32, c.shape, 0), L - 1)
    prev_rows = _rows_from([prev_scr[g] for g in range(G)], L)
    c_prev = jnp.where(row == 0, prev_rows, pltpu.roll(c, 1, 0))
    for g in range(G):
        prev_scr[g] = c[(g + 1) * L - 1:(g + 1) * L, :]
    cm = c + (c_prev - c) * mu_ref[...]
    r = cm[:, 0:W_A]
    k = cm[:, W_A:2 * W_A]
    v = cm[:, 2 * W_A:3 * W_A]
    o = 3 * W_A
    wd = cm[:, o:o + DECAY_RANK]
    ad = cm[:, o + DECAY_RANK:o + DECAY_RANK + ICLR_RANK]
    gd = cm[:, o + DECAY_RANK + ICLR_RANK:]

    ones_b = ones_ref[...]
    logw = -math.exp(-0.5) * _sigmoid(w0_ref[...] + _mm(jnp.tanh(wd), w2_ref[...]))
    a = _sigmoid(a0_ref[...] + _mm(ad, a2_ref[...]))
    gate = _mm(_sigmoid(gd), g2_ref[...])
    kks = k * kk_ref[...]
    kkn = kks * lax.rsqrt(_seg_sum(kks * kks, ones_b, False) + 1e-6)
    k2 = k * (1.0 + (a - 1.0) * ka_ref[...])
    if n_valid < L:
        live = jnp.bitwise_and(lax.broadcasted_iota(jnp.int32, k.shape, 0), L - 1) < n_valid
        logw = jnp.where(live, logw, 0.0)
        kkn = jnp.where(live, kkn, 0.0)
        k2 = jnp.where(live, k2, 0.0)
    b = kkn * a

    cum = _mm_rsplit(ltri_ref[...], logw)
    w_inc = jnp.exp(cum)
    w_inv = jnp.exp(-cum)
    w_last = _rows_from([w_inc[(g + 1) * L - 1:(g + 1) * L, :] for g in range(G)], L)
    kt = k2 * w_inv
    bt = b * w_inv
    groups = lambda x: _to_groups(x, L)
    rt_g = groups((r * w_inc).astype(BF16))
    at_g = groups((kkn * jnp.exp(cum - logw)).astype(BF16))
    kt_g = groups(kt.astype(BF16))
    bt_g = groups(bt.astype(BF16))
    kl_g = groups((kt * w_last).astype(BF16))
    bl_neg_g = groups((-(bt * w_last)).astype(BF16))
    v_g = groups(v.astype(BF16))
    wl_g = groups(w_last)

    mask_b = half_ref
    strict, incl, _ = _packed_tri_masks(L)
    n_groups = len(v_g)
    s_old = [s_scr[i] for i in range(n_groups)]
    s_old_b = [_keep_diag(s.astype(BF16), mask_b) for s in s_old]
    lhs_ar = [jnp.concatenate([x, y], axis=0) for x, y in zip(at_g, rt_g)]
    bt_bd = [_block_diag(x, mask_b) for x in bt_g]
    kt_bd = [_block_diag(x, mask_b) for x in kt_g]
    v_bd = [_block_diag(x, mask_b) for x in v_g]
    g_b = [_bdot_nt(x, bd) for x, bd in zip(lhs_ar, bt_bd)]
    g_k = [_bdot_nt(x, bd) for x, bd in zip(lhs_ar, kt_bd)]
    a_ab = [jnp.where(strict, x[:L], 0.0) for x in g_b]
    r_b_b = [jnp.where(incl, x[L:], 0.0).astype(BF16) for x in g_b]
    lhs_k = [jnp.concatenate([jnp.where(strict, x[:L], 0.0), jnp.where(incl, x[L:], 0.0)], axis=0).astype(BF16)
             for x in g_k]
    sv = [_bdot_nt(x, s) + _bdot(lk, vbd) for x, s, lk, vbd in zip(lhs_ar, s_old_b, lhs_k, v_bd)]
    tp = _tri_inv_minus_eye_packed(a_ab, mask_b, lvl_ref)
    u = [x[:L] + _bdot(t_.astype(BF16), _block_diag(x[:L].astype(BF16), mask_b))
         for x, t_ in zip(sv, tp)]
    u_b = [x.astype(BF16) for x in u]
    y_g = [x[L:] - _bdot(rb, _block_diag(ub, mask_b)) for x, rb, ub in zip(sv, r_b_b, u_b)]
    for i in range(n_groups):
        upd = _bdot_tn(jnp.concatenate([v_g[i], u_b[i]], axis=0),
                       jnp.concatenate([kl_g[i], bl_neg_g[i]], axis=0))
        s_scr[i] = s_old[i] * wl_g[i][0:1, :] + upd

    y = _from_groups(y_g)
    mean = _seg_sum(y, ones_b, True) * (1.0 / HEAD)
    d = y - mean
    var = _seg_sum(d * d, ones_b, False) * (1.0 / HEAD)
    yn = d * lax.rsqrt(var + GN_EPS) * lnw_ref[...] + lnb_ref[...]
    bonus = _seg_sum(r * k2 * rk_ref[...], ones_b, True) * v
    y_ref[...] = ((yn + bonus) * gate).reshape(G, L, W_A)

    @pl.when(t == pl.num_programs(1) - 1)
    def _():
        _store_state(sout_ref, s_scr)


def _mixer_rows(proj3):
    t_len = proj3.shape[1]
    if t_len >= CHUNK:
        assert t_len % CHUNK == 0
        return proj3, CHUNK
    return jnp.pad(proj3, ((0, 0), (0, CHUNK - t_len), (0, 0))), t_len


def _rwkv_mix(proj3, shift0, s0, p, consts):
    bsz, t_len, _ = proj3.shape
    proj3, n_valid = _mixer_rows(proj3)
    t_pad = proj3.shape[1]
    L, G = CHUNK, MIX_GROUP
    assert bsz % G == 0 and G % 2 == 0
    row = lambda n: _const_spec((1, n))
    y, s_new = pl.pallas_call(
        functools.partial(_rwkv_kernel, n_valid=n_valid),
        grid=(bsz // G, t_pad // L),
        in_specs=[pl.BlockSpec((G, L, RWKV_PROJ), lambda b, t: (b, t, OFF_RWKV // RWKV_PROJ)),
                  pl.BlockSpec((G, 1, RWKV_PROJ), lambda b, t: (b, 0, 0)),
                  pl.BlockSpec((G, H_A, HEAD, HEAD), lambda b, t: (b, 0, 0, 0)),
                  row(RWKV_PROJ), row(W_A), _const_spec((DECAY_RANK, W_A)), row(W_A),
                  _const_spec((ICLR_RANK, W_A)), _const_spec((GATE_RANK, W_A)),
                  row(W_A), row(W_A), row(W_A), row(W_A), row(W_A),
                  _const_spec((LANE, LANE)), _const_spec((G * L, G * L)),
                  _const_spec((LANE // HEAD, HEAD, LANE)), _const_spec((N_TRI_LEVELS, CHUNK, GROUP_W))],
        out_specs=[pl.BlockSpec((G, L, W_A), lambda b, t: (b, t, 0)),
                   pl.BlockSpec((G, H_A, HEAD, HEAD), lambda b, t: (b, 0, 0, 0))],
        out_shape=[jax.ShapeDtypeStruct((bsz, t_pad, W_A), F32),
                   jax.ShapeDtypeStruct((bsz, H_A, HEAD, HEAD), F32)],
        scratch_shapes=[pltpu.VMEM((G // 2 * GROUPS_PER_PAIR, GROUP_W, GROUP_W), F32),
                        pltpu.VMEM((G, 1, RWKV_PROJ), F32)],
        compiler_params=_params("parallel", "arbitrary"),
        name="rwkv_mix",
    )(proj3, shift0, s0, p["rwkv_mu"], p["rwkv_w0"], p["rwkv_w2"], p["rwkv_a0"], p["rwkv_a2"],
      p["rwkv_g2"], p["rwkv_kk"], p["rwkv_ka"], p["rwkv_rk"], p["rwkv_ln_w"], p["rwkv_ln_b"],
      consts["ones"], _block_tril(G, L), consts["half"], consts["tri_levels"])
    return y[:, :t_len], s_new


def _gdn_kernel(qkv_ref, ab_ref, z_ref, conv0_ref, s0_ref, cw_ref, alog_ref, dtb_ref, nw_ref,
                ones_ref, ltri_ref, ea_ref, eb_ref, half_ref, lvl_ref,
                y_ref, sout_ref, s_scr, xpad_scr, *, n_valid, preconv):
    G, L = qkv_ref.shape[0], qkv_ref.shape[1]
    t = pl.program_id(1)
    npre = GDN_CONV - 1
    base_row = SUBLANE

    @pl.when(t == 0)
    def _():
        _load_state(s_scr, s0_ref)
        xpad_scr[:, base_row - npre:base_row, :] = conv0_ref[...]

    if preconv:
        qkv = qkv_ref[...].reshape(G * L, GDN_CH)
    else:
        convs = []
        for g in range(G):
            x = qkv_ref[g]
            xpad_scr[g, base_row:base_row + L, :] = x
            conv = x * cw_ref[npre:npre + 1, :]
            for j in range(npre):
                lo = base_row - npre + j
                conv = conv + xpad_scr[g, lo:lo + L, :] * cw_ref[j:j + 1, :]
            xpad_scr[g, base_row - npre:base_row, :] = xpad_scr[g, base_row + L - npre:base_row + L, :]
            convs.append(conv)
        conv = jnp.concatenate(convs, axis=0)
        qkv = conv * _sigmoid(conv)

    ones_b = ones_ref[...]
    q = qkv[:, 0:W_B]
    k = qkv[:, W_B:2 * W_B]
    v = qkv[:, 2 * W_B:]
    q = q * lax.rsqrt(_seg_sum(q * q, ones_b, False) + 1e-6) * (HEAD ** -0.5)
    k = k * lax.rsqrt(_seg_sum(k * k, ones_b, False) + 1e-6)

    ab = ab_ref[...].reshape(G * L, AB_PAD)
    g_row = -jnp.exp(alog_ref[...]) * _softplus(ab + dtb_ref[...])
    g_exp = _mm_lsplit(g_row, ea_ref[...])
    beta = _mm_lsplit(_sigmoid(ab), eb_ref[...])
    if n_valid < L:
        live = jnp.bitwise_and(lax.broadcasted_iota(jnp.int32, k.shape, 0), L - 1) < n_valid
        k = jnp.where(live, k, 0.0)
        beta = jnp.where(live, beta, 0.0)
        g_exp = jnp.where(live, g_exp, 0.0)
    gc = _mm_rsplit(ltri_ref[...], g_exp)
    g_last = _rows_from([gc[(g + 1) * L - 1:(g + 1) * L, :] for g in range(G)], L)
    e_gc = jnp.exp(gc)
    kbeta = k * beta
    vbeta = v * beta
    kw = kbeta * e_gc
    groups = lambda x: _to_groups(x, L)
    k_g = groups(k.astype(BF16))
    q_g = groups(q.astype(BF16))
    kbeta_g = groups(kbeta.astype(BF16))
    vbeta_g = groups(vbeta)
    kw_g = groups(kw)
    qe_g = groups((q * e_gc).astype(BF16))
    kd_g = groups((k * jnp.exp(g_last - gc)).astype(BF16))
    gc_g = groups(gc)
    el_g = groups(jnp.exp(g_last))

    mask_b = half_ref
    strict, incl, eye = _packed_tri_masks(L)
    n_groups = len(k_g)
    s_old = [s_scr[i] for i in range(n_groups)]
    s_old_b = [_keep_diag(s.astype(BF16), mask_b) for s in s_old]
    gamma = []
    for x in gc_g:
        g_row_j = jnp.sum(jnp.where(eye, x, 0.0), axis=0, keepdims=True)
        gamma.append(jnp.where(incl, jnp.exp(jnp.minimum(x - g_row_j, 0.0)), 0.0))
    kk = [_bdot_nt(jnp.concatenate([x, y], axis=0), _block_diag(kb, mask_b))
          for x, y, kb in zip(kbeta_g, q_g, k_g)]
    lower = [jnp.where(strict, x[:L] * gm, 0.0) for x, gm in zip(kk, gamma)]
    att_b = [(x[L:] * gm).astype(BF16) for x, gm in zip(kk, gamma)]
    tp_b = [x.astype(BF16) for x in _tri_inv_minus_eye_packed(lower, mask_b, lvl_ref)]
    u_pre = [x + _bdot(t_, _block_diag(x.astype(BF16), mask_b)) for x, t_ in zip(vbeta_g, tp_b)]
    w_b = [(x + _bdot(t_, _block_diag(x.astype(BF16), mask_b))).astype(BF16) for x, t_ in zip(kw_g, tp_b)]
    ws = [_bdot(jnp.concatenate([w_, qe], axis=0), s) for w_, qe, s in zip(w_b, qe_g, s_old_b)]
    u_b = [(up - x[:L]).astype(BF16) for up, x in zip(u_pre, ws)]
    o_g = [x[L:] + _bdot(a_, _block_diag(ub, mask_b)) for x, a_, ub in zip(ws, att_b, u_b)]
    for i in range(n_groups):
        s_scr[i] = s_old[i] * el_g[i][0:1, :] + _bdot_tn(kd_g[i], u_b[i])

    o = _from_groups(o_g)
    on = o * lax.rsqrt(_seg_sum(o * o, ones_b, False) * (1.0 / HEAD) + EPS) * nw_ref[...]
    z = z_ref[...].reshape(G * L, W_B)
    y_ref[...] = (on * (z * _sigmoid(z))).reshape(G, L, W_B)

    @pl.when(t == pl.num_programs(1) - 1)
    def _():
        _store_state(sout_ref, s_scr)


def _gdn_mix(proj3, conv0, s0, p, consts, preconv=False):
    bsz, t_len, _ = proj3.shape
    assert t_len >= GDN_CONV - 1
    proj3, n_valid = _mixer_rows(proj3)
    t_pad = proj3.shape[1]
    L, G = CHUNK, MIX_GROUP
    assert bsz % G == 0 and G % 2 == 0
    y, s_new = pl.pallas_call(
        functools.partial(_gdn_kernel, n_valid=n_valid, preconv=preconv),
        grid=(bsz // G, t_pad // L),
        in_specs=[pl.BlockSpec((G, L, GDN_CH), lambda b, t: (b, t, OFF_QKV // GDN_CH)),
                  pl.BlockSpec((G, L, AB_PAD), lambda b, t: (b, t, OFF_AB // AB_PAD)),
                  pl.BlockSpec((G, L, W_B), lambda b, t: (b, t, OFF_Z // W_B)),
                  pl.BlockSpec((G, GDN_CONV - 1, GDN_CH), lambda b, t: (b, 0, 0)),
                  pl.BlockSpec((G, H_B, HEAD, HEAD), lambda b, t: (b, 0, 0, 0)),
                  _const_spec((GDN_CONV, GDN_CH)), _const_spec((1, AB_PAD)), _const_spec((1, AB_PAD)),
                  _const_spec((1, W_B)), _const_spec((LANE, LANE)), _const_spec((G * L, G * L)),
                  _const_spec((AB_PAD, W_B)), _const_spec((AB_PAD, W_B)),
                  _const_spec((LANE // HEAD, HEAD, LANE)), _const_spec((N_TRI_LEVELS, CHUNK, GROUP_W))],
        out_specs=[pl.BlockSpec((G, L, W_B), lambda b, t: (b, t, 0)),
                   pl.BlockSpec((G, H_B, HEAD, HEAD), lambda b, t: (b, 0, 0, 0))],
        out_shape=[jax.ShapeDtypeStruct((bsz, t_pad, W_B), F32),
                   jax.ShapeDtypeStruct((bsz, H_B, HEAD, HEAD), F32)],
        scratch_shapes=[pltpu.VMEM((G // 2 * GROUPS_PER_PAIR, GROUP_W, GROUP_W), F32),
                        pltpu.VMEM((G, SUBLANE + L, GDN_CH), F32)],
        compiler_params=_params("parallel", "arbitrary"),
        name="gdn_mix",
    )(proj3, proj3, proj3, conv0, s0, p["gdn_conv_w"], p["gdn_A_log"], p["gdn_dt_bias"],
      p["gdn_norm_w"], consts["ones"], _block_tril(G, L), consts["ea"], consts["eb"],
      consts["half"], consts["tri_levels"])
    return y[:, :t_len], s_new


def _attend_scores(q, kwin, bias_ref, min_col):
    q_b = (q * (HEAD ** -0.5)).astype(BF16)
    sl = lambda x, h: x[:, h * HEAD:(h + 1) * HEAD]
    s = [_bdot_nt(sl(q_b, h), sl(kwin, h)) + bias_ref[h] for h in range(H_C)]
    if min_col is not None:
        col = lax.broadcasted_iota(jnp.int32, s[0].shape, 1)
        s = [jnp.where(col >= min_col, x, -jnp.inf) for x in s]
    e = [jnp.exp(x - jnp.max(x, axis=-1, keepdims=True)) for x in s]
    denom = [jnp.sum(x, axis=-1, keepdims=True) for x in e]
    return [x.astype(BF16) for x in e], denom


def _attend_values(e_b, denom, vwin, o_ref):
    pv = [_bdot(e_b[h], vwin[:, h * HEAD:(h + 1) * HEAD]) for h in range(H_C)]
    for h in range(H_C):
        o_ref[:, h * HEAD:(h + 1) * HEAD] = pv[h] / denom[h]


def _causal_conv_silu(x, carry, cw_ref):
    npre = GDN_CONV - 1
    row = lax.broadcasted_iota(jnp.int32, x.shape, 0)
    carry_rows = jnp.concatenate([carry] * (x.shape[0] // SUBLANE), axis=0)
    conv = x * cw_ref[npre:npre + 1, :]
    for d in range(1, npre + 1):
        shifted = jnp.where(row >= d, pltpu.roll(x, d, 0), pltpu.roll(carry_rows, d, 0))
        conv = conv + shifted * cw_ref[npre - d:npre - d + 1, :]
    return conv * _sigmoid(conv)


def _in_proj_band_kernel(x_ref, g_ref, w_ref, bias_ref, cw_ref, o_ref, yc_ref, tail_ref,
                         q_scr, k_scr, v_scr, carry_scr, *, tiles_per_seq, n_tiles):
    i = pl.program_id(0)
    tm = x_ref.shape[0]
    lead = N_BAND_PREV * CHUNK

    @pl.when(i == 0)
    def _():
        q_scr[...] = jnp.zeros(q_scr.shape, F32)
        k_scr[...] = jnp.zeros(k_scr.shape, BF16)
        v_scr[...] = jnp.zeros(v_scr.shape, BF16)
        carry_scr[...] = jnp.zeros(carry_scr.shape, F32)

    slots = [lax.rem(i + d, 3) for d in range(3)]
    kwin = jnp.concatenate([k_scr[sl] for sl in slots], axis=0)
    vwin = jnp.concatenate([v_scr[sl] for sl in slots], axis=0)
    pos = lax.rem(i + tiles_per_seq - 1, tiles_per_seq)
    min_col = lead - pos * tm
    q_b = (q_scr[...] * (HEAD ** -0.5)).astype(BF16)
    col = lax.broadcasted_iota(jnp.int32, (tm, lead + tm), 1)

    x = x_ref[...]
    h = (x * lax.rsqrt(jnp.mean(x * x, axis=-1, keepdims=True) + EPS) * g_ref[...]).astype(BF16)
    n_col_tiles = N_PROJ // TN_IN
    first_qkv = OFF_QKV // TN_IN
    order = list(range(first_qkv, n_col_tiles)) + list(range(first_qkv))
    share = n_col_tiles // H_C
    for hd in range(H_C):
        sl = slice(hd * HEAD, (hd + 1) * HEAD)
        sc = _bdot_nt(q_b[:, sl], kwin[:, sl]) + bias_ref[hd]
        sc = jnp.where(col >= min_col, sc, -jnp.inf)
        e = jnp.exp(sc - jnp.max(sc, axis=-1, keepdims=True))
        denom = jnp.sum(e, axis=-1, keepdims=True)
        for n in order[hd * share:n_col_tiles if hd == H_C - 1 else (hd + 1) * share]:
            cols = slice(n * TN_IN, (n + 1) * TN_IN)
            o_ref[:, cols] = _bdot(h, w_ref[:, cols])
        yc_ref[:, sl] = _bdot(e.astype(BF16), vwin[:, sl]) / denom
        if hd == 0:
            qkv_cols = slice(OFF_QKV, OFF_QKV + GDN_CH)
            raw = o_ref[:, qkv_cols]
            seq_start = lax.rem(jnp.minimum(i, n_tiles - 1), tiles_per_seq) == 0
            carry = jnp.where(seq_start, 0.0, carry_scr[...])
            o_ref[:, qkv_cols] = _causal_conv_silu(raw, carry, cw_ref)
            tail_ref[...] = raw[tm - SUBLANE:, :]
            carry_scr[...] = jnp.where(i < n_tiles - 1, raw[tm - SUBLANE:, :], carry_scr[...])
    q_scr[...] = o_ref[:, OFF_ATT:OFF_ATT + W_C]
    k_scr[slots[0]] = o_ref[:, OFF_ATT + W_C:OFF_ATT + 2 * W_C].astype(BF16)
    v_scr[slots[0]] = o_ref[:, OFF_ATT + 2 * W_C:OFF_ATT + 3 * W_C].astype(BF16)


def _in_proj_band(x2d, g, w_staged, layer, rel_bias, conv_w, t_len):
    m = x2d.shape[0]
    tm = BAND_Q_CHUNKS * CHUNK
    lead = N_BAND_PREV * CHUNK
    assert t_len % tm == 0 and m % t_len == 0 and lead == 2 * tm and OFF_QKV // TN_IN >= N_PROJ // TN_IN // H_C
    n_tiles = m // tm
    tiles_per_seq = t_len // tm
    n_k = lead + tm
    first = (np.arange(tm)[:, None] // CHUNK) * CHUNK
    cols = np.arange(n_k)[None, :]
    in_band = (cols >= first) & (cols < first + lead + CHUNK)
    bias = jnp.where(in_band[None], _rel_bias_tile(rel_bias, tm, n_k, lead), -jnp.inf)
    held = lambda *shape, idx=None: pl.BlockSpec(shape, idx or (lambda i: (0,) * len(shape)),
                                                 pipeline_mode=pl.Buffered(1))
    return pl.pallas_call(
        functools.partial(_in_proj_band_kernel, tiles_per_seq=tiles_per_seq, n_tiles=n_tiles),
        grid=(n_tiles + 1,),
        in_specs=[pl.BlockSpec((tm, D_MODEL), lambda i: (jnp.minimum(i, n_tiles - 1), 0)),
                  _const_spec((1, D_MODEL)),
                  held(None, D_MODEL, N_PROJ, idx=lambda i: (layer, 0, 0)),
                  held(H_C, tm, n_k),
                  _const_spec((GDN_CONV, GDN_CH))],
        out_specs=[pl.BlockSpec((tm, N_PROJ), lambda i: (jnp.minimum(i, n_tiles - 1), 0)),
                   pl.BlockSpec((tm, W_C), lambda i: (jnp.maximum(i - 1, 0), 0)),
                   pl.BlockSpec((None, SUBLANE, GDN_CH),
                                lambda i: (jnp.minimum(i, n_tiles - 1) // tiles_per_seq, 0, 0))],
        out_shape=[jax.ShapeDtypeStruct((m, N_PROJ), F32), jax.ShapeDtypeStruct((m, W_C), F32),
                   jax.ShapeDtypeStruct((m // t_len, SUBLANE, GDN_CH), F32)],
        scratch_shapes=[pltpu.VMEM((tm, W_C), F32),
                        pltpu.VMEM((3, tm, W_C), BF16),
                        pltpu.VMEM((3, tm, W_C), BF16),
                        pltpu.VMEM((SUBLANE, GDN_CH), F32)],
        compiler_params=_params("arbitrary"),
        name="in_proj_band",
    )(x2d, g, w_staged, bias, conv_w)


def _band_step_kernel(q_ref, k_ref, v_ref, bias_ref, o_ref):
    e_b, denom = _attend_scores(q_ref[0], k_ref[0].astype(BF16), bias_ref, None)
    _attend_values(e_b, denom, v_ref[0].astype(BF16), o_ref.at[0])


def _band_step(proj3, k_all, v_all, bias):
    bsz, t_len, _ = proj3.shape
    n_keys = k_all.shape[1]
    return pl.pallas_call(
        _band_step_kernel,
        grid=(bsz,),
        in_specs=[pl.BlockSpec((1, t_len, W_C), lambda b: (b, 0, OFF_ATT // W_C)),
                  pl.BlockSpec((1, n_keys, W_C), lambda b: (b, 0, 0)),
                  pl.BlockSpec((1, n_keys, W_C), lambda b: (b, 0, 0)),
                  _const_spec((H_C, t_len, n_keys))],
        out_specs=pl.BlockSpec((1, t_len, W_C), lambda b: (b, 0, 0)),
        out_shape=jax.ShapeDtypeStruct((bsz, t_len, W_C), F32),
        compiler_params=_params("parallel"),
        name="band_step",
    )(proj3, k_all, v_all, bias)


def _rel_bias_tile(rel_bias, n_q, n_k, lead):
    n_d = n_q + n_k - 1
    d = np.clip(lead + n_q - 1 - np.arange(n_d), -MAX_REL, MAX_REL) + MAX_REL
    n_hi = int(np.sum(d == 2 * MAX_REL)) - 1 if d[0] == 2 * MAX_REL else 0
    n_lo = int(np.sum(d == 0)) - 1 if d[-1] == 0 else 0
    core = rel_bias[:, int(d[-1 - n_lo]):int(d[n_hi]) + 1][:, ::-1]
    rev = jnp.concatenate([jnp.repeat(rel_bias[:, -1:], n_hi, axis=1), core,
                           jnp.repeat(rel_bias[:, :1], n_lo, axis=1)], axis=1)
    heads = rel_bias.shape[0]
    flat = jnp.broadcast_to(rev[:, None, :], (heads, n_q, n_d)).reshape(heads, n_q * n_d)
    skew = flat[:, n_q - 1:n_q - 1 + n_q * (n_d - 1)].reshape(heads, n_q, n_d - 1)
    return skew[:, :, :n_k]


def _rms(x):
    return x * lax.rsqrt(jnp.mean(x * x, axis=-1, keepdims=True) + EPS)


def _merge_ffn_kernel(x_ref, ya_ref, yb_ref, yc_ref, g0_ref, g1_ref, g2_ref, wa_ref, wb_ref, wc_ref,
                      wo_ref, gmix_ref, gpre_ref, up_ref, down_ref, gpost_ref, o_ref):
    merged = (_sigmoid(g0_ref[...]) * _mm(ya_ref[...], wa_ref[...])
              + _sigmoid(g1_ref[...]) * _mm(yb_ref[...], wb_ref[...])
              + _sigmoid(g2_ref[...]) * _mm(yc_ref[...], wc_ref[...]))
    x1 = x_ref[...] + _rms(_mm(merged, wo_ref[...])) * gmix_ref[...]
    h = (_rms(x1) * gpre_ref[...]).astype(BF16)
    f = jnp.zeros(x1.shape, F32)
    for n in range(D_FF // FF_CHUNK):
        cols = slice(n * FF_CHUNK, (n + 1) * FF_CHUNK)
        act = jnp.maximum(_bdot(h, up_ref[:, cols]), 0.0)
        f = f + _bdot((act * act).astype(BF16), down_ref[cols, :])
    o_ref[...] = x1 + _rms(f) * gpost_ref[...]


def _merge_ffn(x2d, ya, yb, yc, proj2, p):
    m = x2d.shape[0]
    tm = min(TM_FFN, m)
    tok = lambda n, c=0: pl.BlockSpec((tm, n), lambda i, c=c: (i, c))
    held = lambda *shape: pl.BlockSpec(shape, lambda i: (0,) * len(shape), pipeline_mode=pl.Buffered(1))
    return pl.pallas_call(
        _merge_ffn_kernel,
        grid=(m // tm,),
        in_specs=[tok(D_MODEL), tok(W_A), tok(W_B), tok(W_C),
                  tok(D_MODEL, 0), tok(D_MODEL, 1), tok(D_MODEL, 2),
                  held(W_A, D_MODEL), held(W_B, D_MODEL), held(W_C, D_MODEL), held(D_MODEL, D_MODEL),
                  held(1, D_MODEL), held(1, D_MODEL), held(D_MODEL, D_FF), held(D_FF, D_MODEL),
                  held(1, D_MODEL)],
        out_specs=tok(D_MODEL),
        out_shape=jax.ShapeDtypeStruct((m, D_MODEL), F32),
        compiler_params=_params("parallel"),
        name="merge_ffn",
    )(x2d, ya, yb, yc, proj2, proj2, proj2, p["w_br_a"], p["w_br_b"], p["w_br_c"], p["w_out"],
      p["g_post_mix"], p["g_pre_ffn"], p["w_ff_up"], p["w_ff_down"], p["g_post_ffn"])


def _block_tril(groups, n):
    idx = np.arange(groups * n)
    same = (idx[:, None] // n) == (idx[None, :] // n)
    return jnp.asarray((same & (idx[:, None] >= idx[None, :])).astype(np.float32), BF16)


def _constants():
    lane_head = np.arange(LANE) // HEAD
    ones = (lane_head[:, None] == lane_head[None, :]).astype(np.float32)
    half = np.stack([np.broadcast_to(lane_head == p, (HEAD, LANE)) for p in range(LANE // HEAD)]).astype(np.float32)
    ea = np.zeros((AB_PAD, W_B), np.float32)
    eb = np.zeros((AB_PAD, W_B), np.float32)
    for h in range(H_B):
        ea[h, h * HEAD:(h + 1) * HEAD] = 1.0
        eb[H_B + h, h * HEAD:(h + 1) * HEAD] = 1.0
    i = np.arange(CHUNK)[:, None]
    j = (np.arange(GROUP_W) % HEAD)[None, :]
    same = lambda size: (i // size) == (j // size)
    levels = [same(2)] + [same(4 << l) & ~same(2 << l) for l in range(N_TRI_LEVELS - 1)]
    tri_levels = np.stack(levels).astype(np.float32)
    return {"tri_levels": jnp.asarray(tri_levels), "ones": jnp.asarray(ones, BF16), "ea": jnp.asarray(ea, BF16), "eb": jnp.asarray(eb, BF16),
            "half": jnp.asarray(half, BF16)}


def _stage_layer_params(l, w_staged, g_pre_mix, g_post_mix, g_pre_ffn, g_post_ffn, rwkv_mu, rwkv_w0, rwkv_w2,
                        rwkv_a0, rwkv_a2, rwkv_g2, rwkv_kk, rwkv_ka, rwkv_rk, rwkv_ln_w, rwkv_ln_b,
                        gdn_conv_w, gdn_A_log, gdn_dt_bias, gdn_norm_w, att_rel_bias, w_br_a, w_br_b,
                        w_br_c, w_out, w_ff_up, w_ff_down):
    row = lambda a: a[l].reshape(1, -1).astype(F32)
    pad_ab = lambda a: jnp.pad(a[l].astype(F32), (0, AB_PAD - H_B)).reshape(1, AB_PAD)
    return {
        "g_pre_mix": row(g_pre_mix), "g_post_mix": row(g_post_mix),
        "g_pre_ffn": row(g_pre_ffn), "g_post_ffn": row(g_post_ffn),
        "w_in": w_staged, "layer": l,
        "rwkv_mu": row(rwkv_mu), "rwkv_w0": row(rwkv_w0), "rwkv_w2": rwkv_w2[l].astype(BF16),
        "rwkv_a0": row(rwkv_a0), "rwkv_a2": rwkv_a2[l].astype(BF16), "rwkv_g2": rwkv_g2[l].astype(BF16),
        "rwkv_kk": row(rwkv_kk), "rwkv_ka": row(rwkv_ka), "rwkv_rk": row(rwkv_rk),
        "rwkv_ln_w": row(rwkv_ln_w), "rwkv_ln_b": row(rwkv_ln_b),
        "gdn_conv_w": gdn_conv_w[l].astype(F32), "gdn_A_log": pad_ab(gdn_A_log),
        "gdn_dt_bias": pad_ab(gdn_dt_bias),
        "gdn_norm_w": jnp.tile(gdn_norm_w[l].astype(F32), H_B).reshape(1, W_B),
        "att_rel_bias": att_rel_bias[l].astype(F32),
        "w_br_a": w_br_a[l].astype(BF16), "w_br_b": w_br_b[l].astype(BF16), "w_br_c": w_br_c[l].astype(BF16),
        "w_out": w_out[l].astype(BF16), "w_ff_up": w_ff_up[l].astype(BF16),
        "w_ff_down": w_ff_down[l].astype(BF16),
    }


def _trunk_layer(x, p, consts, rwkv_shift, rwkv_s, gdn_conv, gdn_s, band_k, band_v):
    bsz, t_len, _ = x.shape
    m = bsz * t_len
    x2d = x.reshape(m, D_MODEL)
    if band_k is None:
        proj2, yc, qkv_tail = _in_proj_band(x2d, p["g_pre_mix"], p["w_in"], p["layer"], p["att_rel_bias"],
                                            p["gdn_conv_w"], t_len)
    else:
        proj2 = _in_proj(x2d, p["g_pre_mix"], p["w_in"], p["layer"])
    proj3 = proj2.reshape(bsz, t_len, N_PROJ)
    ya, s_a = _rwkv_mix(proj3, rwkv_shift.reshape(bsz, 1, RWKV_PROJ), rwkv_s, p, consts)
    yb, s_b = _gdn_mix(proj3, gdn_conv, gdn_s, p, consts, preconv=band_k is None)
    k_new = proj3[:, :, OFF_ATT + W_C:OFF_ATT + 2 * W_C]
    v_new = proj3[:, :, OFF_ATT + 2 * W_C:OFF_ATT + 3 * W_C]
    if band_k is None:
        keep = min(N_BAND_PREV * CHUNK, t_len)
        k_out, v_out = k_new[:, t_len - keep:], v_new[:, t_len - keep:]
    else:
        n_past = band_k.shape[1]
        k_all = jnp.concatenate([band_k.reshape(bsz, n_past, W_C), k_new], axis=1)
        v_all = jnp.concatenate([band_v.reshape(bsz, n_past, W_C), v_new], axis=1)
        yc = _band_step(proj3, k_all, v_all, _rel_bias_tile(p["att_rel_bias"], t_len, n_past + t_len, n_past))
        k_out, v_out = k_new, v_new
    x2 = _merge_ffn(x2d, ya.reshape(m, W_A), yb.reshape(m, W_B), yc.reshape(m, W_C), proj2, p)
    shift_new = proj3[:, t_len - 1, OFF_RWKV:OFF_RWKV + RWKV_PROJ]
    if band_k is None:
        conv_new = qkv_tail[:, SUBLANE - (GDN_CONV - 1):, :]
    else:
        conv_new = proj3[:, t_len - (GDN_CONV - 1):, OFF_QKV:OFF_QKV + GDN_CH]
    new_state = (shift_new, s_a, conv_new, s_b,
                 k_out.reshape(bsz, -1, H_C, HEAD), v_out.reshape(bsz, -1, H_C, HEAD))
    return x2.reshape(bsz, t_len, D_MODEL), new_state


def kernel(x_prompt, x_sample, state_rwkv_shift, state_rwkv_wkv, state_gdn_conv, state_gdn_S, cache_band_k, cache_band_v, g_pre_mix, g_post_mix, g_pre_ffn, g_post_ffn, w_in, rwkv_mu, rwkv_w0, rwkv_w2, rwkv_a0, rwkv_a2, rwkv_g2, rwkv_kk, rwkv_ka, rwkv_rk, rwkv_ln_w, rwkv_ln_b, gdn_conv_w, gdn_A_log, gdn_dt_bias, gdn_norm_w, att_rel_bias, w_br_a, w_br_b, w_br_c, w_out, w_ff_up, w_ff_down):
    depth = w_in.shape[0]
    bp = x_prompt.shape[0]
    consts = _constants()
    w_staged = _stage_w_in(w_in)
    xp, xs = x_prompt, x_sample
    p_new = [[] for _ in range(6)]
    s_new = [[] for _ in range(6)]
    for l in range(depth):
        p = _stage_layer_params(l, w_staged, g_pre_mix, g_post_mix, g_pre_ffn, g_post_ffn, rwkv_mu, rwkv_w0,
                                rwkv_w2, rwkv_a0, rwkv_a2, rwkv_g2, rwkv_kk, rwkv_ka, rwkv_rk, rwkv_ln_w,
                                rwkv_ln_b, gdn_conv_w, gdn_A_log, gdn_dt_bias, gdn_norm_w, att_rel_bias,
                                w_br_a, w_br_b, w_br_c, w_out, w_ff_up, w_ff_down)
        xp, st_p = _trunk_layer(
            xp, p, consts,
            jnp.zeros((bp, RWKV_PROJ), F32), jnp.zeros((bp, H_A, HEAD, HEAD), F32),
            jnp.zeros((bp, GDN_CONV - 1, GDN_CH), F32), jnp.zeros((bp, H_B, HEAD, HEAD), F32),
            None, None)
        xs, st_s = _trunk_layer(
            xs, p, consts, state_rwkv_shift[l], state_rwkv_wkv[l], state_gdn_conv[l], state_gdn_S[l],
            cache_band_k[l], cache_band_v[l])
        for i in range(6):
            p_new[i].append(st_p[i])
            s_new[i].append(st_s[i])
    stk_p = [jnp.stack(a).astype(x_prompt.dtype) for a in p_new]
    stk_s = [jnp.stack(a).astype(x_sample.dtype) for a in s_new]
    return (xp, xs, *stk_p, *stk_s)
```

```python
import functools
import math

import numpy as np
import jax
import jax.numpy as jnp
from jax import lax
from jax.experimental import pallas as pl
from jax.experimental.pallas import tpu as pltpu

F32 = jnp.float32
BF16 = jnp.bfloat16

D_MODEL = 1024
HEAD = 64
H_A = 6
W_A = H_A * HEAD
DECAY_RANK = 64
ICLR_RANK = 64
GATE_RANK = 128
RWKV_PROJ = 3 * W_A + DECAY_RANK + ICLR_RANK + GATE_RANK
H_B = 6
W_B = H_B * HEAD
GDN_CONV = 4
GDN_CH = 3 * W_B
H_C = 4
W_C = H_C * HEAD
N_BAND_PREV = 8
MAX_REL = 128
CHUNK = 64
D_FF = 4 * D_MODEL
EPS = 1e-6
GN_EPS = 64e-5

LANE = 128
SUBLANE = 8
AB_PAD = LANE
OFF_GATE = 0
OFF_ATT = OFF_GATE + 3 * D_MODEL
OFF_Z = OFF_ATT + 3 * W_C
OFF_RWKV = OFF_Z + W_B
OFF_AB = OFF_RWKV + RWKV_PROJ
OFF_QKV = OFF_AB + AB_PAD
N_PROJ = OFF_QKV + GDN_CH
assert OFF_ATT % W_C == 0 and OFF_Z % W_B == 0 and OFF_RWKV % RWKV_PROJ == 0
assert OFF_AB % AB_PAD == 0 and OFF_QKV % GDN_CH == 0 and N_PROJ % LANE == 0

VMEM_LIMIT = 56 * 1024 * 1024
TM_IN = 512
TN_IN = 768
TM_FFN = 512
FF_CHUNK = 1024
MIX_GROUP = 8
BAND_Q_CHUNKS = 4


def _params(*sem, fuse_inputs=None):
    return pltpu.CompilerParams(dimension_semantics=sem, vmem_limit_bytes=VMEM_LIMIT, allow_input_fusion=fuse_inputs)


def _const_spec(shape):
    nd = len(shape)
    return pl.BlockSpec(shape, lambda *_: (0,) * nd)


def _bdot(a, b):
    return jnp.dot(a, b, preferred_element_type=F32)


def _bdot_nt(a, b):
    return lax.dot_general(a, b, (((1,), (1,)), ((), ())), preferred_element_type=F32)


def _bdot_tn(a, b):
    return lax.dot_general(a, b, (((0,), (0,)), ((), ())), preferred_element_type=F32)


def _mm(a, b):
    return _bdot(a.astype(BF16), b.astype(BF16))


def _split(a):
    hi = a.astype(BF16)
    lo = (a - hi.astype(F32)).astype(BF16)
    return hi, lo


def _mm_lsplit(a, b_exact):
    hi, lo = _split(a)
    return _bdot(hi, b_exact) + _bdot(lo, b_exact)


def _mm_rsplit(a_exact, b):
    hi, lo = _split(b)
    return _bdot(a_exact, hi) + _bdot(a_exact, lo)


def _sigmoid(x):
    return 0.5 * jnp.tanh(0.5 * x) + 0.5


def _softplus(x):
    return jnp.maximum(x, 0.0) + jnp.log(1.0 + jnp.exp(-jnp.abs(x)))


def _rows_from(ref_rows, n):
    return jnp.concatenate([jnp.broadcast_to(r, (n, r.shape[-1])) for r in ref_rows], axis=0)


N_IN = RWKV_PROJ + GDN_CH + 2 * H_B + W_B + 3 * W_C + 3 * D_MODEL
SRC_QKV = RWKV_PROJ
SRC_AB = SRC_QKV + GDN_CH
SRC_Z = SRC_AB + 2 * H_B
TR_STAGE = 128
assert SRC_QKV % LANE == 0 and SRC_AB % LANE == 0


def _stage_w_in_kernel(w_ref, o_ref):
    w = w_ref[0]
    tail = w[:, SRC_Z:N_IN].astype(BF16)
    o_ref[0, :, OFF_GATE:OFF_GATE + 3 * D_MODEL] = tail[:, W_B + 3 * W_C:]
    o_ref[0, :, OFF_ATT:OFF_ATT + 3 * W_C] = tail[:, W_B:W_B + 3 * W_C]
    o_ref[0, :, OFF_Z:OFF_Z + W_B] = tail[:, 0:W_B]
    o_ref[0, :, OFF_RWKV:OFF_RWKV + RWKV_PROJ] = w[:, 0:RWKV_PROJ].astype(BF16)
    o_ref[0, :, OFF_AB:OFF_AB + AB_PAD] = jnp.zeros((w.shape[0], AB_PAD), BF16)
    o_ref[0, :, OFF_AB:OFF_AB + 2 * H_B] = w[:, SRC_AB:SRC_AB + 2 * H_B].astype(BF16)
    o_ref[0, :, OFF_QKV:OFF_QKV + GDN_CH] = w[:, SRC_QKV:SRC_QKV + GDN_CH].astype(BF16)


def _stage_w_in(w_in):
    depth, d_model, n_in = w_in.shape
    assert d_model == D_MODEL and n_in == N_IN
    return pl.pallas_call(
        _stage_w_in_kernel,
        grid=(depth, D_MODEL // TR_STAGE),
        in_specs=[pl.BlockSpec((1, TR_STAGE, N_IN), lambda l, i: (l, i, 0))],
        out_specs=pl.BlockSpec((1, TR_STAGE, N_PROJ), lambda l, i: (l, i, 0)),
        out_shape=jax.ShapeDtypeStruct((depth, D_MODEL, N_PROJ), BF16),
        compiler_params=_params("parallel", "parallel"),
        name="stage_w_in",
    )(w_in)


def _in_proj_kernel(x_ref, g_ref, w_ref, o_ref):
    x = x_ref[...]
    h = (x * lax.rsqrt(jnp.mean(x * x, axis=-1, keepdims=True) + EPS) * g_ref[...]).astype(BF16)
    for n in range(N_PROJ // TN_IN):
        cols = slice(n * TN_IN, (n + 1) * TN_IN)
        o_ref[:, cols] = _bdot(h, w_ref[:, cols])


def _in_proj(x2d, g, w_staged, layer):
    m = x2d.shape[0]
    tm = min(TM_IN, m)
    return pl.pallas_call(
        _in_proj_kernel,
        grid=(m // tm,),
        in_specs=[pl.BlockSpec((tm, D_MODEL), lambda i: (i, 0)),
                  _const_spec((1, D_MODEL)),
                  pl.BlockSpec((None, D_MODEL, N_PROJ), lambda i: (layer, 0, 0), pipeline_mode=pl.Buffered(1))],
        out_specs=pl.BlockSpec((tm, N_PROJ), lambda i: (i, 0)),
        out_shape=jax.ShapeDtypeStruct((m, N_PROJ), F32),
        compiler_params=_params("parallel"),
        name="in_proj",
    )(x2d, g, w_staged)


PACK = 4
GROUP_W = PACK * HEAD
GROUPS_PER_PAIR = 2 * H_A // PACK
N_TRI_LEVELS = int(math.log2(HEAD))
assert H_A == H_B and 2 * H_A % PACK == 0 and GROUP_W % LANE == 0 and CHUNK == HEAD


def _group_slots(pair_group):
    if pair_group < 2:
        return [(pair_group, h) for h in range(PACK)]
    return [(0, PACK), (0, PACK + 1), (1, PACK), (1, PACK + 1)]


def _to_groups(x, L):
    out = []
    for a in range(0, x.shape[0] // L, 2):
        ra, rb = slice(a * L, (a + 1) * L), slice((a + 1) * L, (a + 2) * L)
        out += [x[ra, 0:GROUP_W], x[rb, 0:GROUP_W],
                jnp.concatenate([x[ra, GROUP_W:W_A], x[rb, GROUP_W:W_A]], axis=1)]
    return out


def _from_groups(vals):
    rows = []
    half = (W_A - GROUP_W)
    for p in range(len(vals) // GROUPS_PER_PAIR):
        ga, gb, gc = vals[3 * p:3 * p + 3]
        rows.append(jnp.concatenate([ga, gc[:, 0:half]], axis=1))
        rows.append(jnp.concatenate([gb, gc[:, half:2 * half]], axis=1))
    return jnp.concatenate(rows, axis=0)


def _keep_diag(m_b, half_ref):
    per_tile = LANE // HEAD
    zero = jnp.zeros((HEAD, LANE), BF16)
    rows = []
    for h in range(PACK):
        t = h // per_tile
        tile = m_b[h * HEAD:(h + 1) * HEAD, t * LANE:(t + 1) * LANE] * half_ref[h % per_tile]
        rows.append(jnp.concatenate([tile if j == t else zero for j in range(GROUP_W // LANE)], axis=1))
    return jnp.concatenate(rows, axis=0)


def _block_diag(x_b, half_ref):
    return _keep_diag(jnp.concatenate([x_b] * PACK, axis=0), half_ref)


def _packed_tri_masks(L):
    r = lax.broadcasted_iota(jnp.int32, (L, GROUP_W), 0)
    c = jnp.bitwise_and(lax.broadcasted_iota(jnp.int32, (L, GROUP_W), 1), HEAD - 1)
    return c < r, c <= r, c == r


def _tri_inv_minus_eye_packed(a_list, mask_b, lvl_ref):
    tp = [-(a * lvl_ref[0]) for a in a_list]
    for lvl in range(1, N_TRI_LEVELS):
        m = lvl_ref[lvl]
        off = [a * m for a in a_list]
        off_bd = [_block_diag(x.astype(BF16), mask_b) for x in off]
        tp_b = [t.astype(BF16) for t in tp]
        m1 = [x + _bdot(tb, bd) for x, tb, bd in zip(off, tp_b, off_bd)]
        tp_bd = [_block_diag(tb, mask_b) for tb in tp_b]
        tp = [t - x - _bdot(x.astype(BF16), bd) for t, x, bd in zip(tp, m1, tp_bd)]
    return tp


def _seg_sum(x, ones_b, exact):
    outs = []
    for j in range(W_A // LANE):
        xs = x[:, j * LANE:(j + 1) * LANE]
        outs.append(_mm_lsplit(xs, ones_b) if exact else _bdot(xs.astype(BF16), ones_b))
    return jnp.concatenate(outs, axis=1)


def _load_state(s_scr, s0_ref):
    s_scr[...] = jnp.zeros(s_scr.shape, F32)
    for gi in range(s_scr.shape[0]):
        for j, (q, h) in enumerate(_group_slots(gi % GROUPS_PER_PAIR)):
            seq = 2 * (gi // GROUPS_PER_PAIR) + q
            s_scr[gi, j * HEAD:(j + 1) * HEAD, j * HEAD:(j + 1) * HEAD] = s0_ref[seq, h]


def _store_state(sout_ref, s_scr):
    for gi in range(s_scr.shape[0]):
        for j, (q, h) in enumerate(_group_slots(gi % GROUPS_PER_PAIR)):
            seq = 2 * (gi // GROUPS_PER_PAIR) + q
            sout_ref[seq, h] = s_scr[gi, j * HEAD:(j + 1) * HEAD, j * HEAD:(j + 1) * HEAD]


def _rwkv_kernel(c_ref, shift0_ref, s0_ref, mu_ref, w0_ref, w2_ref, a0_ref, a2_ref, g2_ref,
                 kk_ref, ka_ref, rk_ref, lnw_ref, lnb_ref, ones_ref, ltri_ref, half_ref, lvl_ref,
                 y_ref, sout_ref, s_scr, prev_scr, *, n_valid):
    G, L = c_ref.shape[0], c_ref.shape[1]
    t = pl.program_id(1)

    @pl.when(t == 0)
    def _():
        _load_state(s_scr, s0_ref)
        prev_scr[...] = shift0_ref[...]

    c = c_ref[...].reshape(G * L, RWKV_PROJ)
    row = jnp.bitwise_and(lax.broadcasted_iota(jnp.int32, c.shape, 0), L - 1)
    prev_rows = _rows_from([prev_scr[g] for g in range(G)], L)
    c_prev = jnp.where(row == 0, prev_rows, pltpu.roll(c, 1, 0))
    for g in range(G):
        prev_scr[g] = c[(g + 1) * L - 1:(g + 1) * L, :]
    cm = c + (c_prev - c) * mu_ref[...]
    r = cm[:, 0:W_A]
    k = cm[:, W_A:2 * W_A]
    v = cm[:, 2 * W_A:3 * W_A]
    o = 3 * W_A
    wd = cm[:, o:o + DECAY_RANK]
    ad = cm[:, o + DECAY_RANK:o + DECAY_RANK + ICLR_RANK]
    gd = cm[:, o + DECAY_RANK + ICLR_RANK:]

    ones_b = ones_ref[...]
    logw = -math.exp(-0.5) * _sigmoid(w0_ref[...] + _mm(jnp.tanh(wd), w2_ref[...]))
    a = _sigmoid(a0_ref[...] + _mm(ad, a2_ref[...]))
    gate = _mm(_sigmoid(gd), g2_ref[...])
    kks = k * kk_ref[...]
    kkn = kks * lax.rsqrt(_seg_sum(kks * kks, ones_b, False) + 1e-6)
    k2 = k * (1.0 + (a - 1.0) * ka_ref[...])
    if n_valid < L:
        live = jnp.bitwise_and(lax.broadcasted_iota(jnp.int32, k.shape, 0), L - 1) < n_valid
        logw = jnp.where(live, logw, 0.0)
        kkn = jnp.where(live, kkn, 0.0)
        k2 = jnp.where(live, k2, 0.0)
    b = kkn * a

    cum = _mm_rsplit(ltri_ref[...], logw)
    w_inc = jnp.exp(cum)
    w_inv = jnp.exp(-cum)
    w_last = _rows_from([w_inc[(g + 1) * L - 1:(g + 1) * L, :] for g in range(G)], L)
    kt = k2 * w_inv
    bt = b * w_inv
    groups = lambda x: _to_groups(x, L)
    rt_g = groups((r * w_inc).astype(BF16))
    at_g = groups((kkn * jnp.exp(cum - logw)).astype(BF16))
    kt_g = groups(kt.astype(BF16))
    bt_g = groups(bt.astype(BF16))
    kl_g = groups((kt * w_last).astype(BF16))
    bl_neg_g = groups((-(bt * w_last)).astype(BF16))
    v_g = groups(v.astype(BF16))
    wl_g = groups(w_last)

    mask_b = half_ref
    strict, incl, _ = _packed_tri_masks(L)
    n_groups = len(v_g)
    s_old = [s_scr[i] for i in range(n_groups)]
    s_old_b = [_keep_diag(s.astype(BF16), mask_b) for s in s_old]
    lhs_ar = [jnp.concatenate([x, y], axis=0) for x, y in zip(at_g, rt_g)]
    bt_bd = [_block_diag(x, mask_b) for x in bt_g]
    kt_bd = [_block_diag(x, mask_b) for x in kt_g]
    v_bd = [_block_diag(x, mask_b) for x in v_g]
    g_b = [_bdot_nt(x, bd) for x, bd in zip(lhs_ar, bt_bd)]
    g_k = [_bdot_nt(x, bd) for x, bd in zip(lhs_ar, kt_bd)]
    a_ab = [jnp.where(strict, x[:L], 0.0) for x in g_b]
    r_b_b = [jnp.where(incl, x[L:], 0.0).astype(BF16) for x in g_b]
    lhs_k = [jnp.concatenate([jnp.where(strict, x[:L], 0.0), jnp.where(incl, x[L:], 0.0)], axis=0).astype(BF16)
             for x in g_k]
    sv = [_bdot_nt(x, s) + _bdot(lk, vbd) for x, s, lk, vbd in zip(lhs_ar, s_old_b, lhs_k, v_bd)]
    tp = _tri_inv_minus_eye_packed(a_ab, mask_b, lvl_ref)
    u = [x[:L] + _bdot(t_.astype(BF16), _block_diag(x[:L].astype(BF16), mask_b))
         for x, t_ in zip(sv, tp)]
    u_b = [x.astype(BF16) for x in u]
    y_g = [x[L:] - _bdot(rb, _block_diag(ub, mask_b)) for x, rb, ub in zip(sv, r_b_b, u_b)]
    for i in range(n_groups):
        upd = _bdot_tn(jnp.concatenate([v_g[i], u_b[i]], axis=0),
                       jnp.concatenate([kl_g[i], bl_neg_g[i]], axis=0))
        s_scr[i] = s_old[i] * wl_g[i][0:1, :] + upd

    y = _from_groups(y_g)
    mean = _seg_sum(y, ones_b, True) * (1.0 / HEAD)
    d = y - mean
    var = _seg_sum(d * d, ones_b, False) * (1.0 / HEAD)
    yn = d * lax.rsqrt(var + GN_EPS) * lnw_ref[...] + lnb_ref[...]
    bonus = _seg_sum(r * k2 * rk_ref[...], ones_b, True) * v
    y_ref[...] = ((yn + bonus) * gate).reshape(G, L, W_A)

    @pl.when(t == pl.num_programs(1) - 1)
    def _():
        _store_state(sout_ref, s_scr)


def _mixer_rows(proj3):
    t_len = proj3.shape[1]
    if t_len >= CHUNK:
        assert t_len % CHUNK == 0
        return proj3, CHUNK
    return jnp.pad(proj3, ((0, 0), (0, CHUNK - t_len), (0, 0))), t_len


def _rwkv_mix(proj3, shift0, s0, p, consts):
    bsz, t_len, _ = proj3.shape
    proj3, n_valid = _mixer_rows(proj3)
    t_pad = proj3.shape[1]
    L, G = CHUNK, MIX_GROUP
    assert bsz % G == 0 and G % 2 == 0
    row = lambda n: _const_spec((1, n))
    y, s_new = pl.pallas_call(
        functools.partial(_rwkv_kernel, n_valid=n_valid),
        grid=(bsz // G, t_pad // L),
        in_specs=[pl.BlockSpec((G, L, RWKV_PROJ), lambda b, t: (b, t, OFF_RWKV // RWKV_PROJ)),
                  pl.BlockSpec((G, 1, RWKV_PROJ), lambda b, t: (b, 0, 0)),
                  pl.BlockSpec((G, H_A, HEAD, HEAD), lambda b, t: (b, 0, 0, 0)),
                  row(RWKV_PROJ), row(W_A), _const_spec((DECAY_RANK, W_A)), row(W_A),
                  _const_spec((ICLR_RANK, W_A)), _const_spec((GATE_RANK, W_A)),
                  row(W_A), row(W_A), row(W_A), row(W_A), row(W_A),
                  _const_spec((LANE, LANE)), _const_spec((G * L, G * L)),
                  _const_spec((LANE // HEAD, HEAD, LANE)), _const_spec((N_TRI_LEVELS, CHUNK, GROUP_W))],
        out_specs=[pl.BlockSpec((G, L, W_A), lambda b, t: (b, t, 0)),
                   pl.BlockSpec((G, H_A, HEAD, HEAD), lambda b, t: (b, 0, 0, 0))],
        out_shape=[jax.ShapeDtypeStruct((bsz, t_pad, W_A), F32),
                   jax.ShapeDtypeStruct((bsz, H_A, HEAD, HEAD), F32)],
        scratch_shapes=[pltpu.VMEM((G // 2 * GROUPS_PER_PAIR, GROUP_W, GROUP_W), F32),
                        pltpu.VMEM((G, 1, RWKV_PROJ), F32)],
        compiler_params=_params("parallel", "arbitrary", fuse_inputs=[True] + [False] * 18),
        name="rwkv_mix",
    )(proj3, shift0, s0, p["rwkv_mu"], p["rwkv_w0"], p["rwkv_w2"], p["rwkv_a0"], p["rwkv_a2"],
      p["rwkv_g2"], p["rwkv_kk"], p["rwkv_ka"], p["rwkv_rk"], p["rwkv_ln_w"], p["rwkv_ln_b"],
      consts["ones"], _block_tril(G, L), consts["half"], consts["tri_levels"])
    return y[:, :t_len], s_new


def _gdn_kernel(qkv_ref, ab_ref, z_ref, conv0_ref, s0_ref, cw_ref, alog_ref, dtb_ref, nw_ref,
                ones_ref, ltri_ref, ea_ref, eb_ref, half_ref, lvl_ref,
                y_ref, sout_ref, s_scr, xpad_scr, *, n_valid, preconv):
    G, L = qkv_ref.shape[0], qkv_ref.shape[1]
    t = pl.program_id(1)
    npre = GDN_CONV - 1
    base_row = SUBLANE

    @pl.when(t == 0)
    def _():
        _load_state(s_scr, s0_ref)
        xpad_scr[:, base_row - npre:base_row, :] = conv0_ref[...]

    if preconv:
        qkv = qkv_ref[...].reshape(G * L, GDN_CH)
    else:
        convs = []
        for g in range(G):
            x = qkv_ref[g]
            xpad_scr[g, base_row:base_row + L, :] = x
            conv = x * cw_ref[npre:npre + 1, :]
            for j in range(npre):
                lo = base_row - npre + j
                conv = conv + xpad_scr[g, lo:lo + L, :] * cw_ref[j:j + 1, :]
            xpad_scr[g, base_row - npre:base_row, :] = xpad_scr[g, base_row + L - npre:base_row + L, :]
            convs.append(conv)
        conv = jnp.concatenate(convs, axis=0)
        qkv = conv * _sigmoid(conv)

    ones_b = ones_ref[...]
    q = qkv[:, 0:W_B]
    k = qkv[:, W_B:2 * W_B]
    v = qkv[:, 2 * W_B:]
    q = q * lax.rsqrt(_seg_sum(q * q, ones_b, False) + 1e-6) * (HEAD ** -0.5)
    k = k * lax.rsqrt(_seg_sum(k * k, ones_b, False) + 1e-6)

    ab = ab_ref[...].reshape(G * L, AB_PAD)
    g_row = -jnp.exp(alog_ref[...]) * _softplus(ab + dtb_ref[...])
    g_exp = _mm_lsplit(g_row, ea_ref[...])
    beta = _mm_lsplit(_sigmoid(ab), eb_ref[...])
    if n_valid < L:
        live = jnp.bitwise_and(lax.broadcasted_iota(jnp.int32, k.shape, 0), L - 1) < n_valid
        k = jnp.where(live, k, 0.0)
        beta = jnp.where(live, beta, 0.0)
        g_exp = jnp.where(live, g_exp, 0.0)
    gc = _mm_rsplit(ltri_ref[...], g_exp)
    g_last = _rows_from([gc[(g + 1) * L - 1:(g + 1) * L, :] for g in range(G)], L)
    e_gc = jnp.exp(gc)
    kbeta = k * beta
    vbeta = v * beta
    kw = kbeta * e_gc
    groups = lambda x: _to_groups(x, L)
    k_g = groups(k.astype(BF16))
    q_g = groups(q.astype(BF16))
    kbeta_g = groups(kbeta.astype(BF16))
    vbeta_g = groups(vbeta)
    kw_g = groups(kw)
    qe_g = groups((q * e_gc).astype(BF16))
    kd_g = groups((k * jnp.exp(g_last - gc)).astype(BF16))
    gc_g = groups(gc)
    el_g = groups(jnp.exp(g_last))

    mask_b = half_ref
    strict, incl, eye = _packed_tri_masks(L)
    n_groups = len(k_g)
    s_old = [s_scr[i] for i in range(n_groups)]
    s_old_b = [_keep_diag(s.astype(BF16), mask_b) for s in s_old]
    gamma = []
    for x in gc_g:
        g_row_j = jnp.sum(jnp.where(eye, x, 0.0), axis=0, keepdims=True)
        gamma.append(jnp.where(incl, jnp.exp(jnp.minimum(x - g_row_j, 0.0)), 0.0))
    kk = [_bdot_nt(jnp.concatenate([x, y], axis=0), _block_diag(kb, mask_b))
          for x, y, kb in zip(kbeta_g, q_g, k_g)]
    lower = [jnp.where(strict, x[:L] * gm, 0.0) for x, gm in zip(kk, gamma)]
    att_b = [(x[L:] * gm).astype(BF16) for x, gm in zip(kk, gamma)]
    tp_b = [x.astype(BF16) for x in _tri_inv_minus_eye_packed(lower, mask_b, lvl_ref)]
    u_pre = [x + _bdot(t_, _block_diag(x.astype(BF16), mask_b)) for x, t_ in zip(vbeta_g, tp_b)]
    w_b = [(x + _bdot(t_, _block_diag(x.astype(BF16), mask_b))).astype(BF16) for x, t_ in zip(kw_g, tp_b)]
    ws = [_bdot(jnp.concatenate([w_, qe], axis=0), s) for w_, qe, s in zip(w_b, qe_g, s_old_b)]
    u_b = [(up - x[:L]).astype(BF16) for up, x in zip(u_pre, ws)]
    o_g = [x[L:] + _bdot(a_, _block_diag(ub, mask_b)) for x, a_, ub in zip(ws, att_b, u_b)]
    for i in range(n_groups):
        s_scr[i] = s_old[i] * el_g[i][0:1, :] + _bdot_tn(kd_g[i], u_b[i])

    o = _from_groups(o_g)
    on = o * lax.rsqrt(_seg_sum(o * o, ones_b, False) * (1.0 / HEAD) + EPS) * nw_ref[...]
    z = z_ref[...].reshape(G * L, W_B)
    y_ref[...] = (on * (z * _sigmoid(z))).reshape(G, L, W_B)

    @pl.when(t == pl.num_programs(1) - 1)
    def _():
        _store_state(sout_ref, s_scr)


def _gdn_mix(proj3, conv0, s0, p, consts, preconv=False):
    bsz, t_len, _ = proj3.shape
    assert t_len >= GDN_CONV - 1
    proj3, n_valid = _mixer_rows(proj3)
    t_pad = proj3.shape[1]
    L, G = CHUNK, MIX_GROUP
    assert bsz % G == 0 and G % 2 == 0
    y, s_new = pl.pallas_call(
        functools.partial(_gdn_kernel, n_valid=n_valid, preconv=preconv),
        grid=(bsz // G, t_pad // L),
        in_specs=[pl.BlockSpec((G, L, GDN_CH), lambda b, t: (b, t, OFF_QKV // GDN_CH)),
                  pl.BlockSpec((G, L, AB_PAD), lambda b, t: (b, t, OFF_AB // AB_PAD)),
                  pl.BlockSpec((G, L, W_B), lambda b, t: (b, t, OFF_Z // W_B)),
                  pl.BlockSpec((G, GDN_CONV - 1, GDN_CH), lambda b, t: (b, 0, 0)),
                  pl.BlockSpec((G, H_B, HEAD, HEAD), lambda b, t: (b, 0, 0, 0)),
                  _const_spec((GDN_CONV, GDN_CH)), _const_spec((1, AB_PAD)), _const_spec((1, AB_PAD)),
                  _const_spec((1, W_B)), _const_spec((LANE, LANE)), _const_spec((G * L, G * L)),
                  _const_spec((AB_PAD, W_B)), _const_spec((AB_PAD, W_B)),
                  _const_spec((LANE // HEAD, HEAD, LANE)), _const_spec((N_TRI_LEVELS, CHUNK, GROUP_W))],
        out_specs=[pl.BlockSpec((G, L, W_B), lambda b, t: (b, t, 0)),
                   pl.BlockSpec((G, H_B, HEAD, HEAD), lambda b, t: (b, 0, 0, 0))],
        out_shape=[jax.ShapeDtypeStruct((bsz, t_pad, W_B), F32),
                   jax.ShapeDtypeStruct((bsz, H_B, HEAD, HEAD), F32)],
        scratch_shapes=[pltpu.VMEM((G // 2 * GROUPS_PER_PAIR, GROUP_W, GROUP_W), F32),
                        pltpu.VMEM((G, SUBLANE + L, GDN_CH), F32)],
        compiler_params=_params("parallel", "arbitrary", fuse_inputs=[True] * 3 + [False] * 12),
        name="gdn_mix",
    )(proj3, proj3, proj3, conv0, s0, p["gdn_conv_w"], p["gdn_A_log"], p["gdn_dt_bias"],
      p["gdn_norm_w"], consts["ones"], _block_tril(G, L), consts["ea"], consts["eb"],
      consts["half"], consts["tri_levels"])
    return y[:, :t_len], s_new


def _attend_scores(q, kwin, bias_ref, min_col):
    q_b = (q * (HEAD ** -0.5)).astype(BF16)
    sl = lambda x, h: x[:, h * HEAD:(h + 1) * HEAD]
    s = [_bdot_nt(sl(q_b, h), sl(kwin, h)) + bias_ref[h] for h in range(H_C)]
    if min_col is not None:
        col = lax.broadcasted_iota(jnp.int32, s[0].shape, 1)
        s = [jnp.where(col >= min_col, x, -jnp.inf) for x in s]
    e = [jnp.exp(x - jnp.max(x, axis=-1, keepdims=True)) for x in s]
    denom = [jnp.sum(x, axis=-1, keepdims=True) for x in e]
    return [x.astype(BF16) for x in e], denom


def _attend_values(e_b, denom, vwin, o_ref):
    pv = [_bdot(e_b[h], vwin[:, h * HEAD:(h + 1) * HEAD]) for h in range(H_C)]
    for h in range(H_C):
        o_ref[:, h * HEAD:(h + 1) * HEAD] = pv[h] / denom[h]


def _causal_conv_silu(x, carry, cw_ref):
    npre = GDN_CONV - 1
    row = lax.broadcasted_iota(jnp.int32, x.shape, 0)
    carry_rows = jnp.concatenate([carry] * (x.shape[0] // SUBLANE), axis=0)
    conv = x * cw_ref[npre:npre + 1, :]
    for d in range(1, npre + 1):
        shifted = jnp.where(row >= d, pltpu.roll(x, d, 0), pltpu.roll(carry_rows, d, 0))
        conv = conv + shifted * cw_ref[npre - d:npre - d + 1, :]
    return conv * _sigmoid(conv)


def _in_proj_band_kernel(x_ref, g_ref, w_ref, bias_ref, cw_ref, o_ref, yc_ref, tail_ref,
                         q_scr, k_scr, v_scr, carry_scr, *, tiles_per_seq, n_tiles):
    i = pl.program_id(0)
    tm = x_ref.shape[0]
    lead = N_BAND_PREV * CHUNK

    @pl.when(i == 0)
    def _():
        q_scr[...] = jnp.zeros(q_scr.shape, F32)
        k_scr[...] = jnp.zeros(k_scr.shape, BF16)
        v_scr[...] = jnp.zeros(v_scr.shape, BF16)
        carry_scr[...] = jnp.zeros(carry_scr.shape, F32)

    slots = [lax.rem(i + d, 3) for d in range(3)]
    kwin = jnp.concatenate([k_scr[sl] for sl in slots], axis=0)
    vwin = jnp.concatenate([v_scr[sl] for sl in slots], axis=0)
    pos = lax.rem(i + tiles_per_seq - 1, tiles_per_seq)
    min_col = lead - pos * tm
    q_b = (q_scr[...] * (HEAD ** -0.5)).astype(BF16)
    col = lax.broadcasted_iota(jnp.int32, (tm, lead + tm), 1)

    x = x_ref[...]
    h = (x * lax.rsqrt(jnp.mean(x * x, axis=-1, keepdims=True) + EPS) * g_ref[...]).astype(BF16)
    n_col_tiles = N_PROJ // TN_IN
    first_qkv = OFF_QKV // TN_IN
    order = list(range(first_qkv, n_col_tiles)) + list(range(first_qkv))
    share = n_col_tiles // H_C
    for hd in range(H_C):
        sl = slice(hd * HEAD, (hd + 1) * HEAD)
        sc = _bdot_nt(q_b[:, sl], kwin[:, sl]) + bias_ref[hd]
        sc = jnp.where(col >= min_col, sc, -jnp.inf)
        e = jnp.exp(sc - jnp.max(sc, axis=-1, keepdims=True))
        denom = jnp.sum(e, axis=-1, keepdims=True)
        for n in order[hd * share:n_col_tiles if hd == H_C - 1 else (hd + 1) * share]:
            cols = slice(n * TN_IN, (n + 1) * TN_IN)
            o_ref[:, cols] = _bdot(h, w_ref[:, cols])
        yc_ref[:, sl] = _bdot(e.astype(BF16), vwin[:, sl]) / denom
        if hd == 0:
            qkv_cols = slice(OFF_QKV, OFF_QKV + GDN_CH)
            raw = o_ref[:, qkv_cols]
            seq_start = lax.rem(jnp.minimum(i, n_tiles - 1), tiles_per_seq) == 0
            carry = jnp.where(seq_start, 0.0, carry_scr[...])
            o_ref[:, qkv_cols] = _causal_conv_silu(raw, carry, cw_ref)
            tail_ref[...] = raw[tm - SUBLANE:, :]
            carry_scr[...] = jnp.where(i < n_tiles - 1, raw[tm - SUBLANE:, :], carry_scr[...])
    q_scr[...] = o_ref[:, OFF_ATT:OFF_ATT + W_C]
    k_scr[slots[0]] = o_ref[:, OFF_ATT + W_C:OFF_ATT + 2 * W_C].astype(BF16)
    v_scr[slots[0]] = o_ref[:, OFF_ATT + 2 * W_C:OFF_ATT + 3 * W_C].astype(BF16)


def _in_proj_band(x2d, g, w_staged, layer, rel_bias, conv_w, t_len):
    m = x2d.shape[0]
    tm = BAND_Q_CHUNKS * CHUNK
    lead = N_BAND_PREV * CHUNK
    assert t_len % tm == 0 and m % t_len == 0 and lead == 2 * tm and OFF_QKV // TN_IN >= N_PROJ // TN_IN // H_C
    n_tiles = m // tm
    tiles_per_seq = t_len // tm
    n_k = lead + tm
    first = (np.arange(tm)[:, None] // CHUNK) * CHUNK
    cols = np.arange(n_k)[None, :]
    in_band = (cols >= first) & (cols < first + lead + CHUNK)
    bias = jnp.where(in_band[None], _rel_bias_tile(rel_bias, tm, n_k, lead), -jnp.inf)
    held = lambda *shape, idx=None: pl.BlockSpec(shape, idx or (lambda i: (0,) * len(shape)),
                                                 pipeline_mode=pl.Buffered(1))
    return pl.pallas_call(
        functools.partial(_in_proj_band_kernel, tiles_per_seq=tiles_per_seq, n_tiles=n_tiles),
        grid=(n_tiles + 1,),
        in_specs=[pl.BlockSpec((tm, D_MODEL), lambda i: (jnp.minimum(i, n_tiles - 1), 0)),
                  _const_spec((1, D_MODEL)),
                  held(None, D_MODEL, N_PROJ, idx=lambda i: (layer, 0, 0)),
                  held(H_C, tm, n_k),
                  _const_spec((GDN_CONV, GDN_CH))],
        out_specs=[pl.BlockSpec((tm, N_PROJ), lambda i: (jnp.minimum(i, n_tiles - 1), 0)),
                   pl.BlockSpec((tm, W_C), lambda i: (jnp.maximum(i - 1, 0), 0)),
                   pl.BlockSpec((None, SUBLANE, GDN_CH),
                                lambda i: (jnp.minimum(i, n_tiles - 1) // tiles_per_seq, 0, 0))],
        out_shape=[jax.ShapeDtypeStruct((m, N_PROJ), F32), jax.ShapeDtypeStruct((m, W_C), F32),
                   jax.ShapeDtypeStruct((m // t_len, SUBLANE, GDN_CH), F32)],
        scratch_shapes=[pltpu.VMEM((tm, W_C), F32),
                        pltpu.VMEM((3, tm, W_C), BF16),
                        pltpu.VMEM((3, tm, W_C), BF16),
                        pltpu.VMEM((SUBLANE, GDN_CH), F32)],
        compiler_params=_params("arbitrary"),
        name="in_proj_band",
    )(x2d, g, w_staged, bias, conv_w)


def _band_step_kernel(q_ref, k_ref, v_ref, bias_ref, o_ref):
    e_b, denom = _attend_scores(q_ref[0], k_ref[0].astype(BF16), bias_ref, None)
    _attend_values(e_b, denom, v_ref[0].astype(BF16), o_ref.at[0])


def _band_step(proj3, k_all, v_all, bias):
    bsz, t_len, _ = proj3.shape
    n_keys = k_all.shape[1]
    return pl.pallas_call(
        _band_step_kernel,
        grid=(bsz,),
        in_specs=[pl.BlockSpec((1, t_len, W_C), lambda b: (b, 0, OFF_ATT // W_C)),
                  pl.BlockSpec((1, n_keys, W_C), lambda b: (b, 0, 0)),
                  pl.BlockSpec((1, n_keys, W_C), lambda b: (b, 0, 0)),
                  _const_spec((H_C, t_len, n_keys))],
        out_specs=pl.BlockSpec((1, t_len, W_C), lambda b: (b, 0, 0)),
        out_shape=jax.ShapeDtypeStruct((bsz, t_len, W_C), F32),
        compiler_params=_params("parallel"),
        name="band_step",
    )(proj3, k_all, v_all, bias)


def _rel_bias_tile(rel_bias, n_q, n_k, lead):
    n_d = n_q + n_k - 1
    d = np.clip(lead + n_q - 1 - np.arange(n_d), -MAX_REL, MAX_REL) + MAX_REL
    n_hi = int(np.sum(d == 2 * MAX_REL)) - 1 if d[0] == 2 * MAX_REL else 0
    n_lo = int(np.sum(d == 0)) - 1 if d[-1] == 0 else 0
    core = rel_bias[:, int(d[-1 - n_lo]):int(d[n_hi]) + 1][:, ::-1]
    rev = jnp.concatenate([jnp.repeat(rel_bias[:, -1:], n_hi, axis=1), core,
                           jnp.repeat(rel_bias[:, :1], n_lo, axis=1)], axis=1)
    heads = rel_bias.shape[0]
    flat = jnp.broadcast_to(rev[:, None, :], (heads, n_q, n_d)).reshape(heads, n_q * n_d)
    skew = flat[:, n_q - 1:n_q - 1 + n_q * (n_d - 1)].reshape(heads, n_q, n_d - 1)
    return skew[:, :, :n_k]


def _rms(x):
    return x * lax.rsqrt(jnp.mean(x * x, axis=-1, keepdims=True) + EPS)


def _merge_ffn_kernel(x_ref, ya_ref, yb_ref, yc_ref, g0_ref, g1_ref, g2_ref, wa_ref, wb_ref, wc_ref,
                      wo_ref, gmix_ref, gpre_ref, up_ref, down_ref, gpost_ref, o_ref):
    merged = (_sigmoid(g0_ref[...]) * _mm(ya_ref[...], wa_ref[...])
              + _sigmoid(g1_ref[...]) * _mm(yb_ref[...], wb_ref[...])
              + _sigmoid(g2_ref[...]) * _mm(yc_ref[...], wc_ref[...]))
    x1 = x_ref[...] + _rms(_mm(merged, wo_ref[...])) * gmix_ref[...]
    h = (_rms(x1) * gpre_ref[...]).astype(BF16)
    f = jnp.zeros(x1.shape, F32)
    for n in range(D_FF // FF_CHUNK):
        cols = slice(n * FF_CHUNK, (n + 1) * FF_CHUNK)
        act = jnp.maximum(_bdot(h, up_ref[:, cols]), 0.0)
        f = f + _bdot((act * act).astype(BF16), down_ref[cols, :])
    o_ref[...] = x1 + _rms(f) * gpost_ref[...]


def _merge_ffn(x2d, ya, yb, yc, proj2, p):
    m = x2d.shape[0]
    tm = min(TM_FFN, m)
    tok = lambda n, c=0: pl.BlockSpec((tm, n), lambda i, c=c: (i, c))
    held = lambda *shape: pl.BlockSpec(shape, lambda i: (0,) * len(shape), pipeline_mode=pl.Buffered(1))
    return pl.pallas_call(
        _merge_ffn_kernel,
        grid=(m // tm,),
        in_specs=[tok(D_MODEL), tok(W_A), tok(W_B), tok(W_C),
                  tok(D_MODEL, 0), tok(D_MODEL, 1), tok(D_MODEL, 2),
                  held(W_A, D_MODEL), held(W_B, D_MODEL), held(W_C, D_MODEL), held(D_MODEL, D_MODEL),
                  held(1, D_MODEL), held(1, D_MODEL), held(D_MODEL, D_FF), held(D_FF, D_MODEL),
                  held(1, D_MODEL)],
        out_specs=tok(D_MODEL),
        out_shape=jax.ShapeDtypeStruct((m, D_MODEL), F32),
        compiler_params=_params("parallel"),
        name="merge_ffn",
    )(x2d, ya, yb, yc, proj2, proj2, proj2, p["w_br_a"], p["w_br_b"], p["w_br_c"], p["w_out"],
      p["g_post_mix"], p["g_pre_ffn"], p["w_ff_up"], p["w_ff_down"], p["g_post_ffn"])


def _block_tril(groups, n):
    idx = np.arange(groups * n)
    same = (idx[:, None] // n) == (idx[None, :] // n)
    return jnp.asarray((same & (idx[:, None] >= idx[None, :])).astype(np.float32), BF16)


def _constants():
    lane_head = np.arange(LANE) // HEAD
    ones = (lane_head[:, None] == lane_head[None, :]).astype(np.float32)
    half = np.stack([np.broadcast_to(lane_head == p, (HEAD, LANE)) for p in range(LANE // HEAD)]).astype(np.float32)
    ea = np.zeros((AB_PAD, W_B), np.float32)
    eb = np.zeros((AB_PAD, W_B), np.float32)
    for h in range(H_B):
        ea[h, h * HEAD:(h + 1) * HEAD] = 1.0
        eb[H_B + h, h * HEAD:(h + 1) * HEAD] = 1.0
    i = np.arange(CHUNK)[:, None]
    j = (np.arange(GROUP_W) % HEAD)[None, :]
    same = lambda size: (i // size) == (j // size)
    levels = [same(2)] + [same(4 << l) & ~same(2 << l) for l in range(N_TRI_LEVELS - 1)]
    tri_levels = np.stack(levels).astype(np.float32)
    return {"tri_levels": jnp.asarray(tri_levels), "ones": jnp.asarray(ones, BF16), "ea": jnp.asarray(ea, BF16), "eb": jnp.asarray(eb, BF16),
            "half": jnp.asarray(half, BF16)}


def _stage_layer_params(l, w_staged, g_pre_mix, g_post_mix, g_pre_ffn, g_post_ffn, rwkv_mu, rwkv_w0, rwkv_w2,
                        rwkv_a0, rwkv_a2, rwkv_g2, rwkv_kk, rwkv_ka, rwkv_rk, rwkv_ln_w, rwkv_ln_b,
                        gdn_conv_w, gdn_A_log, gdn_dt_bias, gdn_norm_w, att_rel_bias, w_br_a, w_br_b,
                        w_br_c, w_out, w_ff_up, w_ff_down):
    row = lambda a: a[l].reshape(1, -1).astype(F32)
    pad_ab = lambda a: jnp.pad(a[l].astype(F32), (0, AB_PAD - H_B)).reshape(1, AB_PAD)
    return {
        "g_pre_mix": row(g_pre_mix), "g_post_mix": row(g_post_mix),
        "g_pre_ffn": row(g_pre_ffn), "g_post_ffn": row(g_post_ffn),
        "w_in": w_staged, "layer": l,
        "rwkv_mu": row(rwkv_mu), "rwkv_w0": row(rwkv_w0), "rwkv_w2": rwkv_w2[l].astype(BF16),
        "rwkv_a0": row(rwkv_a0), "rwkv_a2": rwkv_a2[l].astype(BF16), "rwkv_g2": rwkv_g2[l].astype(BF16),
        "rwkv_kk": row(rwkv_kk), "rwkv_ka": row(rwkv_ka), "rwkv_rk": row(rwkv_rk),
        "rwkv_ln_w": row(rwkv_ln_w), "rwkv_ln_b": row(rwkv_ln_b),
        "gdn_conv_w": gdn_conv_w[l].astype(F32), "gdn_A_log": pad_ab(gdn_A_log),
        "gdn_dt_bias": pad_ab(gdn_dt_bias),
        "gdn_norm_w": jnp.tile(gdn_norm_w[l].astype(F32), H_B).reshape(1, W_B),
        "att_rel_bias": att_rel_bias[l].astype(F32),
        "w_br_a": w_br_a[l].astype(BF16), "w_br_b": w_br_b[l].astype(BF16), "w_br_c": w_br_c[l].astype(BF16),
        "w_out": w_out[l].astype(BF16), "w_ff_up": w_ff_up[l].astype(BF16),
        "w_ff_down": w_ff_down[l].astype(BF16),
    }


def _trunk_layer(x, p, consts, rwkv_shift, rwkv_s, gdn_conv, gdn_s, band_k, band_v):
    bsz, t_len, _ = x.shape
    m = bsz * t_len
    x2d = x.reshape(m, D_MODEL)
    if band_k is None:
        proj2, yc, qkv_tail = _in_proj_band(x2d, p["g_pre_mix"], p["w_in"], p["layer"], p["att_rel_bias"],
                                            p["gdn_conv_w"], t_len)
    else:
        proj2 = _in_proj(x2d, p["g_pre_mix"], p["w_in"], p["layer"])
    proj3 = proj2.reshape(bsz, t_len, N_PROJ)
    ya, s_a = _rwkv_mix(proj3, rwkv_shift.reshape(bsz, 1, RWKV_PROJ), rwkv_s, p, consts)
    yb, s_b = _gdn_mix(proj3, gdn_conv, gdn_s, p, consts, preconv=band_k is None)
    k_new = proj3[:, :, OFF_ATT + W_C:OFF_ATT + 2 * W_C]
    v_new = proj3[:, :, OFF_ATT + 2 * W_C:OFF_ATT + 3 * W_C]
    if band_k is None:
        keep = min(N_BAND_PREV * CHUNK, t_len)
        k_out, v_out = k_new[:, t_len - keep:], v_new[:, t_len - keep:]
    else:
        n_past = band_k.shape[1]
        k_all = jnp.concatenate([band_k.reshape(bsz, n_past, W_C), k_new], axis=1)
        v_all = jnp.concatenate([band_v.reshape(bsz, n_past, W_C), v_new], axis=1)
        yc = _band_step(proj3, k_all, v_all, _rel_bias_tile(p["att_rel_bias"], t_len, n_past + t_len, n_past))
        k_out, v_out = k_new, v_new
    x2 = _merge_ffn(x2d, ya.reshape(m, W_A), yb.reshape(m, W_B), yc.reshape(m, W_C), proj2, p)
    shift_new = proj3[:, t_len - 1, OFF_RWKV:OFF_RWKV + RWKV_PROJ]
    if band_k is None:
        conv_new = qkv_tail[:, SUBLANE - (GDN_CONV - 1):, :]
    else:
        conv_new = proj3[:, t_len - (GDN_CONV - 1):, OFF_QKV:OFF_QKV + GDN_CH]
    new_state = (shift_new, s_a, conv_new, s_b,
                 k_out.reshape(bsz, -1, H_C, HEAD), v_out.reshape(bsz, -1, H_C, HEAD))
    return x2.reshape(bsz, t_len, D_MODEL), new_state


def kernel(x_prompt, x_sample, state_rwkv_shift, state_rwkv_wkv, state_gdn_conv, state_gdn_S, cache_band_k, cache_band_v, g_pre_mix, g_post_mix, g_pre_ffn, g_post_ffn, w_in, rwkv_mu, rwkv_w0, rwkv_w2, rwkv_a0, rwkv_a2, rwkv_g2, rwkv_kk, rwkv_ka, rwkv_rk, rwkv_ln_w, rwkv_ln_b, gdn_conv_w, gdn_A_log, gdn_dt_bias, gdn_norm_w, att_rel_bias, w_br_a, w_br_b, w_br_c, w_out, w_ff_up, w_ff_down):
    depth = w_in.shape[0]
    bp = x_prompt.shape[0]
    consts = _constants()
    w_staged = _stage_w_in(w_in)
    xp, xs = x_prompt, x_sample
    p_new = [[] for _ in range(6)]
    s_new = [[] for _ in range(6)]
    for l in range(depth):
        p = _stage_layer_params(l, w_staged, g_pre_mix, g_post_mix, g_pre_ffn, g_post_ffn, rwkv_mu, rwkv_w0,
                                rwkv_w2, rwkv_a0, rwkv_a2, rwkv_g2, rwkv_kk, rwkv_ka, rwkv_rk, rwkv_ln_w,
                                rwkv_ln_b, gdn_conv_w, gdn_A_log, gdn_dt_bias, gdn_norm_w, att_rel_bias,
                                w_br_a, w_br_b, w_br_c, w_out, w_ff_up, w_ff_down)
        xp, st_p = _trunk_layer(
            xp, p, consts,
            jnp.zeros((bp, RWKV_PROJ), F32), jnp.zeros((bp, H_A, HEAD, HEAD), F32),
            jnp.zeros((bp, GDN_CONV - 1, GDN_CH), F32), jnp.zeros((bp, H_B, HEAD, HEAD), F32),
            None, None)
        xs, st_s = _trunk_layer(
            xs, p, consts, state_rwkv_shift[l], state_rwkv_wkv[l], state_gdn_conv[l], state_gdn_S[l],
            cache_band_k[l], cache_band_v[l])
        for i in range(6):
            p_new[i].append(st_p[i])
            s_new[i].append(st_s[i])
    stk_p = [jnp.stack(a).astype(x_prompt.dtype) for a in p_new]
    stk_s = [jnp.stack(a).astype(x_sample.dtype) for a in s_new]
    return (xp, xs, *stk_p, *stk_s)
```
